```python
import jax, jax.numpy as jnp
from jax import lax
import numpy as np

D_MODEL = 1024
BATCH = 8
SEQ = 4096
DEPTH = 4

N_MIXERS = 2
N_MLSTM = (DEPTH + 1) // 2
N_HGRN = DEPTH // 2
CHUNK = 64
M_HEADS = 8
M_DV = D_MODEL // M_HEADS
M_DQK = M_DV // 2
M_QK = M_HEADS * M_DQK
M_V = M_HEADS * M_DV
M_PROJ = 2 * M_QK + 2 * M_V + 2 * M_HEADS
CONV_K = 4
IGATE_CAP = 15.0
HG_EXPAND = 128
HG_HEADS = D_MODEL // HG_EXPAND
HG_DK = HG_EXPAND
HG_DV = D_MODEL // HG_HEADS
HG_PROJ = 2 * HG_HEADS * HG_DK + 2 * D_MODEL
D_FF = 2816
N_EXPERTS = 8
TOP_K = 2
D_FF_EXPERT = D_FF // 2
EPS = 1e-6
NEG = -1e30
F_FLOOR = 1e-30

kernel_name = 'hybrid_mlstm_hgrn2_adaln_moe'


def rmsnorm(x, w):
    x32 = x.astype(jnp.float32)
    y = x32 * lax.rsqrt(jnp.mean(x32 * x32, axis=-1, keepdims=True) + EPS)
    return (y * w.astype(jnp.float32)).astype(x.dtype)


def head_rmsnorm(h, w):
    nh, dh = h.shape[1], h.shape[3]
    y = h * lax.rsqrt(jnp.mean(h * h, axis=-1, keepdims=True) + EPS)
    return y * w.astype(jnp.float32).reshape(1, nh, 1, dh)


def split_heads(a, nh):
    b, s, _ = a.shape
    return a.reshape(b, s, nh, -1).transpose(0, 2, 1, 3)


def merge_heads(a):
    b, h, s, d = a.shape
    return a.transpose(0, 2, 1, 3).reshape(b, s, h * d)


def to_chunks(a):
    b, h, s = a.shape[:3]
    return jnp.moveaxis(a.reshape((b, h, s // CHUNK, CHUNK) + a.shape[3:]), 2, 0)


def from_chunks(a):
    a = jnp.moveaxis(a, 0, 2)
    b, h, nc, l = a.shape[:4]
    return a.reshape((b, h, nc * l) + a.shape[4:])


def causal_conv(x, w, b):
    ch = x.shape[-1]
    y = lax.conv_general_dilated(x, w[:, None, :].astype(x.dtype), window_strides=(1,),
                                 padding=[(w.shape[0] - 1, 0)],
                                 dimension_numbers=('NWC', 'WIO', 'NWC'),
                                 feature_group_count=ch)
    return y + b.astype(x.dtype)


def mlstm_chunkwise(q, k, v, li, f_pre):
    bsz, nh, _, dk = q.shape
    dv = v.shape[-1]
    lf = jax.nn.log_sigmoid(f_pre)
    mask = jnp.tril(jnp.ones((CHUNK, CHUNK), dtype=bool))

    def step(carry, xs):
        c_st, n_st, m_st = carry
        qc, kc, vc, lic, lfc = xs
        b = jnp.cumsum(lfc, axis=-1)
        d = jnp.where(mask, b[..., :, None] - b[..., None, :] + lic[..., None, :], NEG)
        inter = b + m_st[..., None]
        m_t = jnp.maximum(inter, jnp.max(d, axis=-1))
        w_inter = jnp.exp(inter - m_t)
        s = jnp.einsum('bhtd,bhsd->bhts', qc, kc) * jnp.exp(d - m_t[..., None])
        num = (w_inter[..., None] * jnp.einsum('bhtd,bhdv->bhtv', qc, c_st)
               + jnp.einsum('bhts,bhsv->bhtv', s, vc))
        den = w_inter * jnp.einsum('bhtd,bhd->bht', qc, n_st) + jnp.sum(s, axis=-1)
        h = num / jnp.maximum(jnp.abs(den), jnp.exp(-m_t))[..., None]
        b_last = b[..., -1]
        dl = b_last[..., None] - b + lic
        m_new = jnp.maximum(b_last + m_st, jnp.max(dl, axis=-1))
        wk = jnp.exp(dl - m_new[..., None])
        decay = jnp.exp(b_last + m_st - m_new)
        c_new = decay[..., None, None] * c_st + jnp.einsum('bhs,bhsd,bhsv->bhdv', wk, kc, vc)
        n_new = decay[..., None] * n_st + jnp.einsum('bhs,bhsd->bhd', wk, kc)
        return (c_new, n_new, m_new), h

    init = (jnp.zeros((bsz, nh, dk, dv), jnp.float32),
            jnp.zeros((bsz, nh, dk), jnp.float32),
            jnp.zeros((bsz, nh), jnp.float32))
    _, hs = lax.scan(step, init, (to_chunks(q), to_chunks(k), to_chunks(v), to_chunks(li), to_chunks(lf)))
    return from_chunks(hs)


def hgrn2_chunkwise(q, k, log_f, v):
    bsz, nh, _, dk = q.shape
    dv = v.shape[-1]
    mask = jnp.tril(jnp.ones((CHUNK, CHUNK), dtype=bool))

    def step(s_st, xs):
        qc, kc, gc, vc = xs
        bc = jnp.cumsum(gc, axis=2)
        diff = bc[:, :, :, None, :] - bc[:, :, None, :, :]
        decay = jnp.exp(jnp.where(mask[:, :, None], diff, NEG))
        a = jnp.einsum('bhtd,bhsd,bhtsd->bhts', qc, kc, decay)
        o = (jnp.einsum('bhtd,bhdv->bhtv', qc * jnp.exp(bc), s_st)
             + jnp.einsum('bhts,bhsv->bhtv', a, vc))
        b_last = bc[:, :, -1]
        s_new = (jnp.exp(b_last)[..., None] * s_st
                 + jnp.einsum('bhsd,bhsv->bhdv', kc * jnp.exp(b_last[:, :, None] - bc), vc))
        return s_new, o

    init = jnp.zeros((bsz, nh, dk, dv), jnp.float32)
    _, os_ = lax.scan(step, init, (to_chunks(q), to_chunks(k), to_chunks(log_f), to_chunks(v)))
    return from_chunks(os_)


def mlstm_mixer(h, w_in, i_bias, f_bias, conv_w, conv_b, norm_w, w_out):
    proj = h @ w_in
    qk = jax.nn.silu(causal_conv(proj[..., :2 * M_QK], conv_w, conv_b))
    q = split_heads(qk[..., :M_QK], M_HEADS).astype(jnp.float32)
    k = split_heads(qk[..., M_QK:], M_HEADS).astype(jnp.float32) * (M_DQK ** -0.5)
    v = split_heads(proj[..., 2 * M_QK:2 * M_QK + M_V], M_HEADS).astype(jnp.float32)
    o = proj[..., 2 * M_QK + M_V:2 * M_QK + 2 * M_V]
    gates = proj[..., 2 * M_QK + 2 * M_V:].astype(jnp.float32)
    i_pre = gates[..., :M_HEADS] + i_bias.astype(jnp.float32)
    i_pre = IGATE_CAP * jnp.tanh(i_pre / IGATE_CAP)
    f_pre = gates[..., M_HEADS:] + f_bias.astype(jnp.float32)
    hh = mlstm_chunkwise(q, k, v, jnp.swapaxes(i_pre, 1, 2), jnp.swapaxes(f_pre, 1, 2))
    hh = merge_heads(head_rmsnorm(hh, norm_w)).astype(h.dtype) * jax.nn.sigmoid(o)
    return hh @ w_out


def hgrn2_mixer(h, w_in, lb, norm_w, w_out):
    proj = h @ w_in
    fd = HG_HEADS * HG_DK
    q = jax.nn.silu(proj[..., :fd]).astype(jnp.float32)
    f = proj[..., fd:2 * fd].astype(jnp.float32)
    i = proj[..., 2 * fd:2 * fd + D_MODEL].astype(jnp.float32)
    g = proj[..., 2 * fd + D_MODEL:]
    lb = lb.astype(jnp.float32)
    f_t = lb + (1.0 - lb) * jax.nn.sigmoid(f)
    log_f = jnp.log(jnp.maximum(f_t, F_FLOOR))
    k = (1.0 - lb) * jax.nn.sigmoid(-f)
    o = hgrn2_chunkwise(split_heads(q, HG_HEADS), split_heads(k, HG_HEADS),
                        split_heads(log_f, HG_HEADS), split_heads(i, HG_HEADS))
    o = merge_heads(head_rmsnorm(o, norm_w)).astype(h.dtype) * jax.nn.silu(g)
    return o @ w_out


def swiglu(t, wg, wu, wd):
    return (jax.nn.silu(t @ wg) * (t @ wu)) @ wd


def moe_swiglu(h, router_w, wg, wu, wd):
    bsz, s, d = h.shape
    t = h.reshape(bsz * s, d)
    logits = (t @ router_w).astype(jnp.float32)
    top_logits, top_idx = lax.top_k(logits, TOP_K)
    top_w = jax.nn.softmax(top_logits, axis=-1)
    gates = jnp.einsum('tk,tke->te', top_w, jax.nn.one_hot(top_idx, N_EXPERTS, dtype=jnp.float32)).astype(h.dtype)
    y = jnp.zeros_like(t)
    for e in range(N_EXPERTS):
        y = y + gates[:, e:e + 1] * swiglu(t, wg[e], wu[e], wd[e])
    return y.reshape(bsz, s, d)


def setup_inputs(seed: int = 0) -> dict:
    key = jax.random.key(seed)
    ks = iter(jax.random.split(key, 32))

    def nrm(shape, scale):
        return jax.random.normal(next(ks), shape, jnp.float32) * scale

    d = D_MODEL
    return {
        'x': nrm((BATCH, SEQ, d), 1.0),
        'c': nrm((BATCH, d), 1.0),
        'ada_w': nrm((DEPTH, d, 6 * d), 0.5 * d ** -0.5),
        'ada_b': nrm((DEPTH, 6 * d), 0.02),
        'norm1_w': 1.0 + nrm((DEPTH, d), 0.02),
        'norm2_w': 1.0 + nrm((DEPTH, d), 0.02),
        'final_norm_w': 1.0 + nrm((d,), 0.02),
        'm_w_in': nrm((N_MLSTM, d, M_PROJ), d ** -0.5),
        'm_i_bias': nrm((N_MLSTM, M_HEADS), 0.1),
        'm_f_bias': jnp.linspace(3.0, 6.0, M_HEADS, dtype=jnp.float32)[None, :] + nrm((N_MLSTM, M_HEADS), 0.1),
        'm_conv_w': nrm((N_MLSTM, CONV_K, 2 * M_QK), CONV_K ** -0.5),
        'm_conv_b': nrm((N_MLSTM, 2 * M_QK), 0.01),
        'm_norm_w': 1.0 + nrm((N_MLSTM, M_V), 0.02),
        'm_w_out': nrm((N_MLSTM, M_V, d), M_V ** -0.5),
        'h_w_in': nrm((N_HGRN, d, HG_PROJ), d ** -0.5),
        'h_lb_logits': nrm((N_HGRN, HG_HEADS * HG_DK), 0.5),
        'h_norm_w': 1.0 + nrm((N_HGRN, d), 0.02),
        'h_w_out': nrm((N_HGRN, d, d), d ** -0.5),
        'ffn_w_gate': nrm((N_MLSTM, d, D_FF), d ** -0.5),
        'ffn_w_up': nrm((N_MLSTM, d, D_FF), d ** -0.5),
        'ffn_w_down': nrm((N_MLSTM, D_FF, d), D_FF ** -0.5),
        'moe_router': nrm((N_HGRN, d, N_EXPERTS), d ** -0.5),
        'moe_w_gate': nrm((N_HGRN, N_EXPERTS, d, D_FF_EXPERT), d ** -0.5),
        'moe_w_up': nrm((N_HGRN, N_EXPERTS, d, D_FF_EXPERT), d ** -0.5),
        'moe_w_down': nrm((N_HGRN, N_EXPERTS, D_FF_EXPERT, d), D_FF_EXPERT ** -0.5),
    }


def reference(x, c, ada_w, ada_b, norm1_w, norm2_w, final_norm_w,
              m_w_in, m_i_bias, m_f_bias, m_conv_w, m_conv_b, m_norm_w, m_w_out,
              h_w_in, h_lb_logits, h_norm_w, h_w_out,
              ffn_w_gate, ffn_w_up, ffn_w_down,
              moe_router, moe_w_gate, moe_w_up, moe_w_down):
    cond = jax.nn.silu(c)
    gam = jax.nn.softmax(h_lb_logits.astype(jnp.float32), axis=0)
    lower_bounds = jnp.cumsum(gam, axis=0) - gam[:1]
    for layer in range(DEPTH):
        j = layer // N_MIXERS
        mod = (cond @ ada_w[layer] + ada_b[layer])[:, None, :]
        shift1, scale1, gate1, shift2, scale2, gate2 = jnp.split(mod, 6, axis=-1)
        hn = rmsnorm(x, norm1_w[layer]) * (1 + scale1) + shift1
        if layer % N_MIXERS == 0:
            y = mlstm_mixer(hn, m_w_in[j], m_i_bias[j], m_f_bias[j], m_conv_w[j], m_conv_b[j],
                            m_norm_w[j], m_w_out[j])
        else:
            y = hgrn2_mixer(hn, h_w_in[j], lower_bounds[j], h_norm_w[j], h_w_out[j])
        x = x + gate1 * y
        hn = rmsnorm(x, norm2_w[layer]) * (1 + scale2) + shift2
        if layer % 2 == 0:
            y = swiglu(hn, ffn_w_gate[j], ffn_w_up[j], ffn_w_down[j])
        else:
            y = moe_swiglu(hn, moe_router[j], moe_w_gate[j], moe_w_up[j], moe_w_down[j])
        x = x + gate2 * y
    return rmsnorm(x, final_norm_w)
```

```python
import functools

import jax
import jax.numpy as jnp
from jax import lax
from jax.experimental import pallas as pl
from jax.experimental.pallas import tpu as pltpu

F32 = jnp.float32
BF16 = jnp.bfloat16

DEPTH = 4
D = 1024
N_HEADS = 8
HEAD_DV = 128
M_DQK = 64
CONV_K = 4
IGATE_CAP = 15.0
N_EXPERTS = 8
EPS = 1e-6
NEG = -1e30
F_FLOOR = 1e-30
LANES = 128
CHUNK = 128
SUB = 16
VMEM_LIMIT = 56 * 1024 * 1024


def _cparams(*sem):
    return pltpu.CompilerParams(dimension_semantics=sem, vmem_limit_bytes=VMEM_LIMIT)


def _sigmoid(x):
    return 1.0 / (1.0 + jnp.exp(-x))


def _silu(x):
    return x * _sigmoid(x)


def _dot(a, b):
    return jnp.dot(a, b, preferred_element_type=F32)


def _dot_nt(a, b):
    return lax.dot_general(a, b, (((1,), (1,)), ((), ())), preferred_element_type=F32)


def _dot_f32(a, b):
    return jnp.dot(a, b, preferred_element_type=F32, precision=lax.Precision.HIGHEST)


def _norm_mod(x, norm_w, scale, shift):
    y = x * lax.rsqrt(jnp.mean(x * x, axis=-1, keepdims=True) + EPS)
    return (y * norm_w) * (1.0 + scale) + shift


def _ada_kernel(c_ref, w_ref, b_ref, o_ref):
    c = c_ref[...]
    o_ref[0] = _dot_f32(_silu(c), w_ref[0]) + b_ref[0]


def ada_modulation(c, ada_w, ada_b):
    bsz = c.shape[0]
    tn = 1536
    return pl.pallas_call(
        _ada_kernel,
        grid=(DEPTH, 6 * D // tn),
        in_specs=[pl.BlockSpec((bsz, D), lambda l, j: (0, 0)),
                  pl.BlockSpec((1, D, tn), lambda l, j: (l, 0, j)),
                  pl.BlockSpec((1, 1, tn), lambda l, j: (l, 0, j))],
        out_specs=pl.BlockSpec((1, bsz, tn), lambda l, j: (l, 0, j)),
        out_shape=jax.ShapeDtypeStruct((DEPTH, bsz, 6 * D), F32),
        compiler_params=_cparams("arbitrary", "arbitrary"),
        name="ada_modulation",
    )(c, ada_w, ada_b.reshape(DEPTH, 1, 6 * D))


def _in_proj_kernel(x_ref, mod_ref, nw_ref, w_ref, wx_ref, o_ref, ox_ref, *, n_chunk):
    hn = _norm_mod(x_ref[...], nw_ref[...], mod_ref[0, 1:2, :], mod_ref[0, 0:1, :])
    ox_ref[...] = _dot_f32(hn, wx_ref[...])
    hb = hn.astype(BF16)
    for n0 in range(0, o_ref.shape[1], n_chunk):
        o_ref[:, n0:n0 + n_chunk] = _dot(hb, w_ref[:, n0:n0 + n_chunk])


def in_proj(x, mod, norm_w, w, w_extra, *, seq, tm):
    t = x.shape[0]
    n = w.shape[1]
    per_b = seq // tm
    return pl.pallas_call(
        functools.partial(_in_proj_kernel, n_chunk=512),
        grid=(t // tm,),
        in_specs=[pl.BlockSpec((tm, D), lambda i: (i, 0)),
                  pl.BlockSpec((1, 8, D), lambda i: (i // per_b, 0, 0)),
                  pl.BlockSpec((1, D), lambda i: (0, 0)),
                  pl.BlockSpec((D, n), lambda i: (0, 0)),
                  pl.BlockSpec((D, LANES), lambda i: (0, 0))],
        out_specs=[pl.BlockSpec((tm, n), lambda i: (i, 0)),
                   pl.BlockSpec((tm, LANES), lambda i: (i, 0))],
        out_shape=[jax.ShapeDtypeStruct((t, n), F32),
                   jax.ShapeDtypeStruct((t, LANES), F32)],
        compiler_params=_cparams("arbitrary"),
        name="in_proj",
    )(x, mod, norm_w, w, w_extra)


def _mlstm_kernel(qk_ref, v_ref, o_ref, g_ref, cw_ref, cb_ref, gb_ref, nw_ref, out_ref,
                  tail_ref, c_ref, n_ref, m_ref):
    L = CHUNK

    @pl.when(pl.program_id(1) == 0)
    def _():
        tail_ref[...] = jnp.zeros_like(tail_ref)
        c_ref[...] = jnp.zeros_like(c_ref)
        n_ref[...] = jnp.zeros_like(n_ref)
        m_ref[...] = jnp.zeros_like(m_ref)

    raw = qk_ref[...]
    xx = jnp.concatenate([tail_ref[...], raw], axis=0)
    tail_ref[...] = raw[L - 8:, :]
    y = cb_ref[...] + cw_ref[CONV_K - 1:CONV_K, :] * raw
    for j in range(CONV_K - 1):
        y = y + cw_ref[j:j + 1, :] * pltpu.roll(xx, CONV_K - 1 - j, axis=0)[8:, :]
    qk = _silu(y)
    half = N_HEADS * M_DQK
    q_all = qk[:, :half]
    k_all = qk[:, half:] * (M_DQK ** -0.5)

    g = g_ref[...] + gb_ref[...]
    lane_g = lax.broadcasted_iota(jnp.int32, g.shape, 1)
    li_all = IGATE_CAP * jnp.tanh(g * (1.0 / IGATE_CAP))
    lf_all = jnp.minimum(g, 0.0) - jnp.log1p(jnp.exp(-jnp.abs(g)))
    gates = jnp.where(lane_g < N_HEADS, li_all, lf_all)
    row = lax.broadcasted_iota(jnp.int32, (L, L), 0)
    col = lax.broadcasted_iota(jnp.int32, (L, L), 1)
    tril = (col <= row)
    tri_f = tril.astype(F32)
    csum = _dot_f32(tri_f, gates)
    gates_t = gates.T
    csum_t = csum.T

    lane = lax.broadcasted_iota(jnp.int32, (L, LANES), 1)
    outs = []
    for h in range(N_HEADS):
        p, hf = h // 2, h % 2
        in_head = (lane >= hf * M_DQK) & (lane < (hf + 1) * M_DQK)
        q_m = jnp.where(in_head, q_all[:, p * LANES:(p + 1) * LANES], 0.0)
        k_m = jnp.where(in_head, k_all[:, p * LANES:(p + 1) * LANES], 0.0)
        v_h = v_ref[:, h * HEAD_DV:(h + 1) * HEAD_DV]
        b_col = csum[:, N_HEADS + h:N_HEADS + h + 1]
        b_row = csum_t[N_HEADS + h:N_HEADS + h + 1, :]
        li_col = gates[:, h:h + 1]
        li_row = gates_t[h:h + 1, :]
        m_st = m_ref[h:h + 1, 0:1]
        c_st = c_ref[h]
        n_st = n_ref[h:h + 1, :]

        dmat = jnp.where(tril, b_col - b_row + li_row, NEG)
        inter = b_col + m_st
        m_t = jnp.maximum(inter, jnp.max(dmat, axis=-1, keepdims=True))
        w_inter = jnp.exp(inter - m_t)
        qb = q_m.astype(BF16)
        kb = k_m.astype(BF16)
        vb = v_h.astype(BF16)
        s = _dot_nt(qb, kb) * jnp.exp(dmat - m_t)
        num = w_inter * _dot(qb, c_st.astype(BF16)) + _dot(s.astype(BF16), vb)
        den = (w_inter * jnp.sum(q_m * n_st, axis=-1, keepdims=True)
               + jnp.sum(s, axis=-1, keepdims=True))
        hh = num / jnp.maximum(jnp.abs(den), jnp.exp(-m_t))

        b_last = b_col[L - 1:L, :]
        dl_col = b_last - b_col + li_col
        dl_row = b_last - b_row + li_row
        m_new = jnp.maximum(b_last + m_st, jnp.max(dl_row, axis=-1, keepdims=True))
        wk = jnp.exp(dl_col - m_new)
        decay = jnp.exp(b_last + m_st - m_new)
        kw = k_m * wk
        c_ref[h] = decay * c_st + _dot(kw.T.astype(BF16), vb)
        n_ref[h:h + 1, :] = decay * n_st + jnp.sum(kw, axis=0, keepdims=True)
        m_ref[h:h + 1, :] = jnp.broadcast_to(m_new, (1, LANES))

        hn = hh * lax.rsqrt(jnp.mean(hh * hh, axis=-1, keepdims=True) + EPS)
        outs.append(hn)
    merged = jnp.concatenate(outs, axis=-1) * nw_ref[...]
    out_ref[...] = (merged * _sigmoid(o_ref[...])).astype(out_ref.dtype)


def mlstm_mix(proj, gates, conv_w, conv_b, gate_bias, norm_w, *, bsz, seq):
    t = bsz * seq
    nblk = seq // CHUNK
    dv = N_HEADS * HEAD_DV
    row = lambda b, s: b * nblk + s
    return pl.pallas_call(
        _mlstm_kernel,
        grid=(bsz, nblk),
        in_specs=[pl.BlockSpec((CHUNK, D), lambda b, s: (row(b, s), 0)),
                  pl.BlockSpec((CHUNK, dv), lambda b, s: (row(b, s), 1)),
                  pl.BlockSpec((CHUNK, dv), lambda b, s: (row(b, s), 2)),
                  pl.BlockSpec((CHUNK, LANES), lambda b, s: (row(b, s), 0)),
                  pl.BlockSpec((CONV_K, D), lambda b, s: (0, 0)),
                  pl.BlockSpec((1, D), lambda b, s: (0, 0)),
                  pl.BlockSpec((1, LANES), lambda b, s: (0, 0)),
                  pl.BlockSpec((1, dv), lambda b, s: (0, 0))],
        out_specs=pl.BlockSpec((CHUNK, dv), lambda b, s: (row(b, s), 0)),
        out_shape=jax.ShapeDtypeStruct((t, dv), BF16),
        scratch_shapes=[pltpu.VMEM((8, D), F32),
                        pltpu.VMEM((N_HEADS, LANES, HEAD_DV), F32),
                        pltpu.VMEM((N_HEADS, LANES), F32),
                        pltpu.VMEM((N_HEADS, LANES), F32)],
        compiler_params=_cparams("arbitrary", "arbitrary"),
        name="mlstm_mix",
    )(proj, proj, proj, gates, conv_w, conv_b, gate_bias, norm_w)


def _hgrn_kernel(q_ref, f_ref, v_ref, g_ref, lb_ref, nw_ref, out_ref, st_ref):
    L = CHUNK

    @pl.when(pl.program_id(1) == 0)
    def _():
        st_ref[...] = jnp.zeros_like(st_ref)

    lb = lb_ref[...]
    f = f_ref[...]
    sig = _sigmoid(f)
    log_f = jnp.log(jnp.maximum(lb + (1.0 - lb) * sig, F_FLOOR))
    k_all = (1.0 - lb) * _sigmoid(-f)
    q_all = _silu(q_ref[...])
    row = lax.broadcasted_iota(jnp.int32, (L, L), 0)
    col = lax.broadcasted_iota(jnp.int32, (L, L), 1)
    tri_f = (col <= row).astype(F32)
    bc_all = _dot_f32(tri_f, log_f)
    sub_row = lax.broadcasted_iota(jnp.int32, (SUB, LANES), 0)
    tok = lax.broadcasted_iota(jnp.int32, (L, LANES), 0)

    outs = []
    for h in range(N_HEADS):
        sl = slice(h * HEAD_DV, (h + 1) * HEAD_DV)
        q = q_all[:, sl]
        k = k_all[:, sl]
        bc = bc_all[:, sl]
        v = v_ref[:, sl]
        vb = v.astype(BF16)
        st = st_ref[h]
        b_last = bc[L - 1:L, :]

        o_parts = _dot_nt((q * jnp.exp(bc)).astype(BF16), st.astype(BF16))

        a_rows = []
        for i in range(L // SUB):
            r0 = i * SUB
            if i == 0:
                a_rows.append(jnp.zeros((SUB, L), F32))
                continue
            p_i = bc[r0 - 1:r0, :]
            q_i = q[r0:r0 + SUB, :] * jnp.exp(bc[r0:r0 + SUB, :] - p_i)
            k_i = k * jnp.exp(jnp.where(tok < r0, p_i - bc, NEG))
            a_rows.append(_dot_nt(q_i.astype(BF16), k_i.astype(BF16)))
        a_off = jnp.concatenate(a_rows, axis=0)
        o_parts = o_parts + _dot(a_off.astype(BF16), vb)

        o_diag = []
        for i in range(L // SUB):
            r0 = i * SUB
            q_i = q[r0:r0 + SUB, :]
            bc_i = bc[r0:r0 + SUB, :]
            acc = jnp.zeros((SUB, HEAD_DV), F32)
            for s in range(SUB):
                dec = jnp.exp(jnp.where(sub_row >= s, bc_i - bc[r0 + s:r0 + s + 1, :], NEG))
                a_s = jnp.sum(q_i * dec * k[r0 + s:r0 + s + 1, :], axis=-1, keepdims=True)
                acc = acc + a_s * v[r0 + s:r0 + s + 1, :]
            o_diag.append(acc)
        o_h = o_parts + jnp.concatenate(o_diag, axis=0)

        k_dec = k * jnp.exp(b_last - bc)
        st_ref[h] = st * jnp.exp(b_last) + _dot(v.T.astype(BF16), k_dec.astype(BF16))

        outs.append(o_h * lax.rsqrt(jnp.mean(o_h * o_h, axis=-1, keepdims=True) + EPS))
    merged = jnp.concatenate(outs, axis=-1) * nw_ref[...]
    out_ref[...] = (merged * _silu(g_ref[...])).astype(out_ref.dtype)


def hgrn_mix(proj, lb, norm_w, *, bsz, seq):
    t = bsz * seq
    nblk = seq // CHUNK
    row = lambda b, s: b * nblk + s
    return pl.pallas_call(
        _hgrn_kernel,
        grid=(bsz, nblk),
        in_specs=[pl.BlockSpec((CHUNK, D), lambda b, s: (row(b, s), 0)),
                  pl.BlockSpec((CHUNK, D), lambda b, s: (row(b, s), 1)),
                  pl.BlockSpec((CHUNK, D), lambda b, s: (row(b, s), 2)),
                  pl.BlockSpec((CHUNK, D), lambda b, s: (row(b, s), 3)),
                  pl.BlockSpec((1, D), lambda b, s: (0, 0)),
                  pl.BlockSpec((1, D), lambda b, s: (0, 0))],
        out_specs=pl.BlockSpec((CHUNK, D), lambda b, s: (row(b, s), 0)),
        out_shape=jax.ShapeDtypeStruct((t, D), BF16),
        scratch_shapes=[pltpu.VMEM((N_HEADS, HEAD_DV, HEAD_DV), F32)],
        compiler_params=_cparams("arbitrary", "arbitrary"),
        name="hgrn_mix",
    )(proj, proj, proj, proj, lb, norm_w)


def _out_proj_kernel(y_ref, w_ref, x_ref, mod_ref, o_ref, *, gate_row):
    o_ref[...] = x_ref[...] + mod_ref[0, gate_row:gate_row + 1, :] * _dot(y_ref[...], w_ref[...])


def out_proj(y, w, x, mod, *, seq, tm, gate_row):
    t = x.shape[0]
    per_b = seq // tm
    return pl.pallas_call(
        functools.partial(_out_proj_kernel, gate_row=gate_row),
        grid=(t // tm,),
        in_specs=[pl.BlockSpec((tm, D), lambda i: (i, 0)),
                  pl.BlockSpec((D, D), lambda i: (0, 0)),
                  pl.BlockSpec((tm, D), lambda i: (i, 0)),
                  pl.BlockSpec((1, 8, D), lambda i: (i // per_b, 0, 0))],
        out_specs=pl.BlockSpec((tm, D), lambda i: (i, 0)),
        out_shape=jax.ShapeDtypeStruct((t, D), F32),
        compiler_params=_cparams("arbitrary"),
        name="out_proj",
    )(y, w, x, mod)


def _ffn_kernel(x_ref, mod_ref, nw_ref, wg_ref, wu_ref, wd_ref, o_ref, hn_ref, acc_ref):
    f = pl.program_id(1)

    @pl.when(f == 0)
    def _():
        hn = _norm_mod(x_ref[...], nw_ref[...], mod_ref[0, 4:5, :], mod_ref[0, 3:4, :])
        hn_ref[...] = hn.astype(BF16)
        acc_ref[...] = jnp.zeros_like(acc_ref)

    hb = hn_ref[...]
    hid = _silu(_dot(hb, wg_ref[...])) * _dot(hb, wu_ref[...])
    acc_ref[...] += _dot(hid.astype(BF16), wd_ref[...])

    @pl.when(f == pl.num_programs(1) - 1)
    def _():
        o_ref[...] = x_ref[...] + mod_ref[0, 5:6, :] * acc_ref[...]


def ffn(x, mod, norm_w, wg, wu, wd, *, seq, tm, tf):
    t = x.shape[0]
    dff = wg.shape[1]
    per_b = seq // tm
    return pl.pallas_call(
        _ffn_kernel,
        grid=(t // tm, dff // tf),
        in_specs=[pl.BlockSpec((tm, D), lambda i, f: (i, 0)),
                  pl.BlockSpec((1, 8, D), lambda i, f: (i // per_b, 0, 0)),
                  pl.BlockSpec((1, D), lambda i, f: (0, 0)),
                  pl.BlockSpec((D, tf), lambda i, f: (0, f)),
                  pl.BlockSpec((D, tf), lambda i, f: (0, f)),
                  pl.BlockSpec((tf, D), lambda i, f: (f, 0))],
        out_specs=pl.BlockSpec((tm, D), lambda i, f: (i, 0)),
        out_shape=jax.ShapeDtypeStruct((t, D), F32),
        scratch_shapes=[pltpu.VMEM((tm, D), BF16), pltpu.VMEM((tm, D), F32)],
        compiler_params=_cparams("arbitrary", "arbitrary"),
        name="ffn",
    )(x, mod, norm_w, wg, wu, wd)


def _router_gates(logits):
    lane = lax.broadcasted_iota(jnp.int32, logits.shape, 1)
    lg = jnp.where(lane < N_EXPERTS, logits, NEG)
    m1 = jnp.max(lg, axis=-1, keepdims=True)
    i1 = jnp.min(jnp.where(lg == m1, lane, LANES), axis=-1, keepdims=True)
    lg2 = jnp.where(lane == i1, NEG, lg)
    m2 = jnp.max(lg2, axis=-1, keepdims=True)
    i2 = jnp.min(jnp.where(lg2 == m2, lane, LANES), axis=-1, keepdims=True)
    e2 = jnp.exp(m2 - m1)
    w1 = 1.0 / (1.0 + e2)
    return jnp.where(lane == i1, w1, 0.0) + jnp.where(lane == i2, e2 * w1, 0.0)


def _moe_kernel(x_ref, mod_ref, nw_ref, lg_ref, wg_ref, wu_ref, wd_ref, o_ref, hn_ref, acc_ref, gate_ref):
    e = pl.program_id(1)
    f = pl.program_id(2)

    @pl.when((e == 0) & (f == 0))
    def _():
        hn = _norm_mod(x_ref[...], nw_ref[...], mod_ref[0, 4:5, :], mod_ref[0, 3:4, :])
        hn_ref[...] = hn.astype(BF16)
        acc_ref[...] = jnp.zeros_like(acc_ref)
        gate_ref[...] = _router_gates(lg_ref[...])

    gates = gate_ref[...]
    lane = lax.broadcasted_iota(jnp.int32, gates.shape, 1)
    g_e = jnp.sum(jnp.where(lane == e, gates, 0.0), axis=-1, keepdims=True)
    hb = hn_ref[...]
    hid = _silu(_dot(hb, wg_ref[0])) * _dot(hb, wu_ref[0]) * g_e
    acc_ref[...] += _dot(hid.astype(BF16), wd_ref[0])

    @pl.when((e == pl.num_programs(1) - 1) & (f == pl.num_programs(2) - 1))
    def _():
        o_ref[...] = x_ref[...] + mod_ref[0, 5:6, :] * acc_ref[...]


def moe(x, mod, norm_w, logits, wg, wu, wd, *, seq, tm, tf):
    t = x.shape[0]
    dfe = wg.shape[2]
    per_b = seq // tm
    return pl.pallas_call(
        _moe_kernel,
        grid=(t // tm, N_EXPERTS, dfe // tf),
        in_specs=[pl.BlockSpec((tm, D), lambda i, e, f: (i, 0)),
                  pl.BlockSpec((1, 8, D), lambda i, e, f: (i // per_b, 0, 0)),
                  pl.BlockSpec((1, D), lambda i, e, f: (0, 0)),
                  pl.BlockSpec((tm, LANES), lambda i, e, f: (i, 0)),
                  pl.BlockSpec((1, D, tf), lambda i, e, f: (e, 0, f)),
                  pl.BlockSpec((1, D, tf), lambda i, e, f: (e, 0, f)),
                  pl.BlockSpec((1, tf, D), lambda i, e, f: (e, f, 0))],
        out_specs=pl.BlockSpec((tm, D), lambda i, e, f: (i, 0)),
        out_shape=jax.ShapeDtypeStruct((t, D), F32),
        scratch_shapes=[pltpu.VMEM((tm, D), BF16), pltpu.VMEM((tm, D), F32), pltpu.VMEM((tm, LANES), F32)],
        compiler_params=_cparams("arbitrary", "arbitrary", "arbitrary"),
        name="moe",
    )(x, mod, norm_w, logits, wg, wu, wd)


def _router_kernel(x_ref, mod_ref, nw_ref, wr_ref, o_ref):
    hn = _norm_mod(x_ref[...], nw_ref[...], mod_ref[0, 4:5, :], mod_ref[0, 3:4, :])
    o_ref[...] = _dot_f32(hn, wr_ref[...])


def router_logits(x, mod, norm_w, wr, *, seq, tm):
    t = x.shape[0]
    per_b = seq // tm
    return pl.pallas_call(
        _router_kernel,
        grid=(t // tm,),
        in_specs=[pl.BlockSpec((tm, D), lambda i: (i, 0)),
                  pl.BlockSpec((1, 8, D), lambda i: (i // per_b, 0, 0)),
                  pl.BlockSpec((1, D), lambda i: (0, 0)),
                  pl.BlockSpec((D, LANES), lambda i: (0, 0))],
        out_specs=pl.BlockSpec((tm, LANES), lambda i: (i, 0)),
        out_shape=jax.ShapeDtypeStruct((t, LANES), F32),
        compiler_params=_cparams("arbitrary"),
        name="router_logits",
    )(x, mod, norm_w, wr)


def _final_norm_kernel(x_ref, w_ref, o_ref):
    x = x_ref[...]
    o_ref[...] = x * lax.rsqrt(jnp.mean(x * x, axis=-1, keepdims=True) + EPS) * w_ref[...]


def final_norm(x, w, *, tm):
    t = x.shape[0]
    return pl.pallas_call(
        _final_norm_kernel,
        grid=(t // tm,),
        in_specs=[pl.BlockSpec((tm, D), lambda i: (i, 0)), pl.BlockSpec((1, D), lambda i: (0, 0))],
        out_specs=pl.BlockSpec((tm, D), lambda i: (i, 0)),
        out_shape=jax.ShapeDtypeStruct((t, D), F32),
        compiler_params=_cparams("arbitrary"),
        name="final_norm",
    )(x, w)


def _pad_cols(w, n):
    return jnp.pad(w, ((0, 0), (0, n - w.shape[1])))


def kernel(x, c, ada_w, ada_b, norm1_w, norm2_w, final_norm_w, m_w_in, m_i_bias, m_f_bias, m_conv_w, m_conv_b, m_norm_w, m_w_out, h_w_in, h_lb_logits, h_norm_w, h_w_out, ffn_w_gate, ffn_w_up, ffn_w_down, moe_router, moe_w_gate, moe_w_up, moe_w_down):
    bsz, seq, _ = x.shape
    t = bsz * seq
    tm = min(512, seq)
    xt = x.reshape(t, D)

    mod_all = ada_modulation(c, ada_w, ada_b)
    mod_all = jnp.pad(mod_all.reshape(DEPTH, bsz, 6, D), ((0, 0), (0, 0), (0, 2), (0, 0)))

    gam = jax.nn.softmax(h_lb_logits.astype(F32), axis=0)
    lower_bounds = jnp.cumsum(gam, axis=0) - gam[:1]

    n_main = 2 * N_HEADS * M_DQK + 2 * N_HEADS * HEAD_DV
    zero_extra = jnp.zeros((D, LANES), F32)
    for layer in range(DEPTH):
        j = layer // 2
        mod = mod_all[layer]
        n1 = norm1_w[layer].reshape(1, D)
        n2 = norm2_w[layer].reshape(1, D)
        if layer % 2 == 0:
            w_in = m_w_in[j]
            proj, gates = in_proj(xt, mod, n1, w_in[:, :n_main].astype(BF16), _pad_cols(w_in[:, n_main:], LANES),
                                  seq=seq, tm=tm)
            gate_bias = _pad_cols(jnp.concatenate([m_i_bias[j], m_f_bias[j]]).reshape(1, -1), LANES)
            y = mlstm_mix(proj, gates, m_conv_w[j], m_conv_b[j].reshape(1, -1), gate_bias,
                          m_norm_w[j].reshape(1, -1), bsz=bsz, seq=seq)
            xt = out_proj(y, m_w_out[j].astype(BF16), xt, mod, seq=seq, tm=tm, gate_row=2)
            xt = ffn(xt, mod, n2, ffn_w_gate[j].astype(BF16), ffn_w_up[j].astype(BF16),
                     ffn_w_down[j].astype(BF16), seq=seq, tm=tm, tf=1408)
        else:
            proj, _ = in_proj(xt, mod, n1, h_w_in[j].astype(BF16), zero_extra, seq=seq, tm=tm)
            y = hgrn_mix(proj, lower_bounds[j].reshape(1, D), h_norm_w[j].reshape(1, D), bsz=bsz, seq=seq)
            xt = out_proj(y, h_w_out[j].astype(BF16), xt, mod, seq=seq, tm=tm, gate_row=2)
            logits = router_logits(xt, mod, n2, _pad_cols(moe_router[j], LANES), seq=seq, tm=tm)
            xt = moe(xt, mod, n2, logits, moe_w_gate[j].astype(BF16), moe_w_up[j].astype(BF16),
                     moe_w_down[j].astype(BF16), seq=seq, tm=tm, tf=1408)
    out = final_norm(xt, final_norm_w.reshape(1, D), tm=tm)
    return out.reshape(bsz, seq, D)
```

```python
import functools

import jax
import jax.numpy as jnp
from jax import lax
from jax.experimental import pallas as pl
from jax.experimental.pallas import tpu as pltpu

F32 = jnp.float32
BF16 = jnp.bfloat16

DEPTH = 4
D = 1024
N_HEADS = 8
HEAD_DV = 128
M_DQK = 64
CONV_K = 4
IGATE_CAP = 15.0
N_EXPERTS = 8
EPS = 1e-6
NEG = -1e30
F_FLOOR = 1e-30
LANES = 128
CHUNK = 128
SUB = 16
VMEM_LIMIT = 56 * 1024 * 1024


def _cparams(*sem):
    return pltpu.CompilerParams(dimension_semantics=sem, vmem_limit_bytes=VMEM_LIMIT)


def _sigmoid(x):
    return 1.0 / (1.0 + jnp.exp(-x))


def _silu(x):
    return x * _sigmoid(x)


def _dot(a, b):
    return jnp.dot(a, b, preferred_element_type=F32)


def _dot_nt(a, b):
    return lax.dot_general(a, b, (((1,), (1,)), ((), ())), preferred_element_type=F32)


def _dot_f32(a, b):
    return jnp.dot(a, b, preferred_element_type=F32, precision=lax.Precision.HIGHEST)


def _norm_mod(x, norm_w, scale, shift):
    y = x * lax.rsqrt(jnp.mean(x * x, axis=-1, keepdims=True) + EPS)
    return (y * norm_w) * (1.0 + scale) + shift


def _ada_kernel(c_ref, w_ref, b_ref, o_ref):
    c = c_ref[...]
    o_ref[0] = _dot_f32(_silu(c), w_ref[0]) + b_ref[0]


def ada_modulation(c, ada_w, ada_b):
    bsz = c.shape[0]
    tn = 1536
    return pl.pallas_call(
        _ada_kernel,
        grid=(DEPTH, 6 * D // tn),
        in_specs=[pl.BlockSpec((bsz, D), lambda l, j: (0, 0)),
                  pl.BlockSpec((1, D, tn), lambda l, j: (l, 0, j)),
                  pl.BlockSpec((1, 1, tn), lambda l, j: (l, 0, j))],
        out_specs=pl.BlockSpec((1, bsz, tn), lambda l, j: (l, 0, j)),
        out_shape=jax.ShapeDtypeStruct((DEPTH, bsz, 6 * D), F32),
        compiler_params=_cparams("arbitrary", "arbitrary"),
        name="ada_modulation",
    )(c, ada_w, ada_b.reshape(DEPTH, 1, 6 * D))


def _in_proj_kernel(x_ref, mod_ref, nw_ref, w_ref, wx_ref, o_ref, ox_ref, *, n_chunk):
    hn = _norm_mod(x_ref[...], nw_ref[...], mod_ref[0, 1:2, :], mod_ref[0, 0:1, :])
    ox_ref[...] = _dot_f32(hn, wx_ref[...])
    hb = hn.astype(BF16)
    for n0 in range(0, o_ref.shape[1], n_chunk):
        o_ref[:, n0:n0 + n_chunk] = _dot(hb, w_ref[:, n0:n0 + n_chunk])


def in_proj(x, mod, norm_w, w, w_extra, *, seq, tm):
    t = x.shape[0]
    n = w.shape[1]
    per_b = seq // tm
    return pl.pallas_call(
        functools.partial(_in_proj_kernel, n_chunk=512),
        grid=(t // tm,),
        in_specs=[pl.BlockSpec((tm, D), lambda i: (i, 0)),
                  pl.BlockSpec((1, 8, D), lambda i: (i // per_b, 0, 0)),
                  pl.BlockSpec((1, D), lambda i: (0, 0)),
                  pl.BlockSpec((D, n), lambda i: (0, 0)),
                  pl.BlockSpec((D, LANES), lambda i: (0, 0))],
        out_specs=[pl.BlockSpec((tm, n), lambda i: (i, 0)),
                   pl.BlockSpec((tm, LANES), lambda i: (i, 0))],
        out_shape=[jax.ShapeDtypeStruct((t, n), F32),
                   jax.ShapeDtypeStruct((t, LANES), F32)],
        compiler_params=_cparams("arbitrary"),
        name="in_proj",
    )(x, mod, norm_w, w, w_extra)


def _mlstm_kernel(qk_ref, v_ref, o_ref, g_ref, cw_ref, cb_ref, gb_ref, nw_ref, out_ref,
                  tail_ref, c_ref, n_ref, m_ref):
    L = CHUNK

    @pl.when(pl.program_id(1) == 0)
    def _():
        tail_ref[...] = jnp.zeros_like(tail_ref)
        c_ref[...] = jnp.zeros_like(c_ref)
        n_ref[...] = jnp.zeros_like(n_ref)
        m_ref[...] = jnp.zeros_like(m_ref)

    raw = qk_ref[...]
    xx = jnp.concatenate([tail_ref[...], raw], axis=0)
    tail_ref[...] = raw[L - 8:, :]
    y = cb_ref[...] + cw_ref[CONV_K - 1:CONV_K, :] * raw
    for j in range(CONV_K - 1):
        y = y + cw_ref[j:j + 1, :] * pltpu.roll(xx, CONV_K - 1 - j, axis=0)[8:, :]
    qk = _silu(y)
    half = N_HEADS * M_DQK
    q_all = qk[:, :half]
    k_all = qk[:, half:] * (M_DQK ** -0.5)

    g = g_ref[...] + gb_ref[...]
    lane_g = lax.broadcasted_iota(jnp.int32, g.shape, 1)
    li_all = IGATE_CAP * jnp.tanh(g * (1.0 / IGATE_CAP))
    lf_all = jnp.minimum(g, 0.0) - jnp.log1p(jnp.exp(-jnp.abs(g)))
    gates = jnp.where(lane_g < N_HEADS, li_all, lf_all)
    row = lax.broadcasted_iota(jnp.int32, (L, L), 0)
    col = lax.broadcasted_iota(jnp.int32, (L, L), 1)
    tril = (col <= row)
    tri_f = tril.astype(F32)
    csum = _dot_f32(tri_f, gates)
    gates_t = gates.T
    csum_t = csum.T

    lane = lax.broadcasted_iota(jnp.int32, (L, LANES), 1)
    outs = []
    for h in range(N_HEADS):
        p, hf = h // 2, h % 2
        in_head = (lane >= hf * M_DQK) & (lane < (hf + 1) * M_DQK)
        q_m = jnp.where(in_head, q_all[:, p * LANES:(p + 1) * LANES], 0.0)
        k_m = jnp.where(in_head, k_all[:, p * LANES:(p + 1) * LANES], 0.0)
        v_h = v_ref[:, h * HEAD_DV:(h + 1) * HEAD_DV]
        b_col = csum[:, N_HEADS + h:N_HEADS + h + 1]
        b_row = csum_t[N_HEADS + h:N_HEADS + h + 1, :]
        li_col = gates[:, h:h + 1]
        li_row = gates_t[h:h + 1, :]
        m_st = m_ref[h:h + 1, 0:1]
        c_st = c_ref[h]
        n_st = n_ref[h:h + 1, :]

        dmat = jnp.where(tril, b_col - b_row + li_row, NEG)
        inter = b_col + m_st
        m_t = jnp.maximum(inter, jnp.max(dmat, axis=-1, keepdims=True))
        w_inter = jnp.exp(inter - m_t)
        qb = q_m.astype(BF16)
        kb = k_m.astype(BF16)
        vb = v_h.astype(BF16)
        s = _dot_nt(qb, kb) * jnp.exp(dmat - m_t)
        num = w_inter * _dot(qb, c_st.astype(BF16)) + _dot(s.astype(BF16), vb)
        den = (w_inter * jnp.sum(q_m * n_st, axis=-1, keepdims=True)
               + jnp.sum(s, axis=-1, keepdims=True))
        hh = num / jnp.maximum(jnp.abs(den), jnp.exp(-m_t))

        b_last = b_col[L - 1:L, :]
        dl_col = b_last - b_col + li_col
        dl_row = b_last - b_row + li_row
        m_new = jnp.maximum(b_last + m_st, jnp.max(dl_row, axis=-1, keepdims=True))
        wk = jnp.exp(dl_col - m_new)
        decay = jnp.exp(b_last + m_st - m_new)
        kw = k_m * wk
        c_ref[h] = decay * c_st + _dot(kw.T.astype(BF16), vb)
        n_ref[h:h + 1, :] = decay * n_st + jnp.sum(kw, axis=0, keepdims=True)
        m_ref[h:h + 1, :] = jnp.broadcast_to(m_new, (1, LANES))

        hn = hh * lax.rsqrt(jnp.mean(hh * hh, axis=-1, keepdims=True) + EPS)
        outs.append(hn)
    merged = jnp.concatenate(outs, axis=-1) * nw_ref[...]
    out_ref[...] = (merged * _sigmoid(o_ref[...])).astype(out_ref.dtype)


def mlstm_mix(proj, gates, conv_w, conv_b, gate_bias, norm_w, *, bsz, seq):
    t = bsz * seq
    nblk = seq // CHUNK
    dv = N_HEADS * HEAD_DV
    row = lambda b, s: b * nblk + s
    return pl.pallas_call(
        _mlstm_kernel,
        grid=(bsz, nblk),
        in_specs=[pl.BlockSpec((CHUNK, D), lambda b, s: (row(b, s), 0)),
                  pl.BlockSpec((CHUNK, dv), lambda b, s: (row(b, s), 1)),
                  pl.BlockSpec((CHUNK, dv), lambda b, s: (row(b, s), 2)),
                  pl.BlockSpec((CHUNK, LANES), lambda b, s: (row(b, s), 0)),
                  pl.BlockSpec((CONV_K, D), lambda b, s: (0, 0)),
                  pl.BlockSpec((1, D), lambda b, s: (0, 0)),
                  pl.BlockSpec((1, LANES), lambda b, s: (0, 0)),
                  pl.BlockSpec((1, dv), lambda b, s: (0, 0))],
        out_specs=pl.BlockSpec((CHUNK, dv), lambda b, s: (row(b, s), 0)),
        out_shape=jax.ShapeDtypeStruct((t, dv), BF16),
        scratch_shapes=[pltpu.VMEM((8, D), F32),
                        pltpu.VMEM((N_HEADS, LANES, HEAD_DV), F32),
                        pltpu.VMEM((N_HEADS, LANES), F32),
                        pltpu.VMEM((N_HEADS, LANES), F32)],
        compiler_params=_cparams("arbitrary", "arbitrary"),
        name="mlstm_mix",
    )(proj, proj, proj, gates, conv_w, conv_b, gate_bias, norm_w)


def _hgrn_kernel(q_ref, f_ref, v_ref, g_ref, lb_ref, nw_ref, out_ref, st_ref):
    L = CHUNK

    @pl.when(pl.program_id(1) == 0)
    def _():
        st_ref[...] = jnp.zeros_like(st_ref)

    lb = lb_ref[...]
    f = f_ref[...]
    sig = _sigmoid(f)
    log_f = jnp.log(jnp.maximum(lb + (1.0 - lb) * sig, F_FLOOR))
    k_all = (1.0 - lb) * _sigmoid(-f)
    q_all = _silu(q_ref[...])
    row = lax.broadcasted_iota(jnp.int32, (L, L), 0)
    col = lax.broadcasted_iota(jnp.int32, (L, L), 1)
    tri_f = (col <= row).astype(F32)
    bc_all = _dot_f32(tri_f, log_f)
    sub_row = lax.broadcasted_iota(jnp.int32, (SUB, LANES), 0)
    tok = lax.broadcasted_iota(jnp.int32, (L, LANES), 0)

    outs = []
    for h in range(N_HEADS):
        sl = slice(h * HEAD_DV, (h + 1) * HEAD_DV)
        q = q_all[:, sl]
        k = k_all[:, sl]
        bc = bc_all[:, sl]
        v = v_ref[:, sl]
        vb = v.astype(BF16)
        st = st_ref[h]
        b_last = bc[L - 1:L, :]

        o_parts = _dot_nt((q * jnp.exp(bc)).astype(BF16), st.astype(BF16))

        a_rows = []
        for i in range(L // SUB):
            r0 = i * SUB
            if i == 0:
                a_rows.append(jnp.zeros((SUB, L), F32))
                continue
            p_i = bc[r0 - 1:r0, :]
            q_i = q[r0:r0 + SUB, :] * jnp.exp(bc[r0:r0 + SUB, :] - p_i)
            k_i = k * jnp.exp(jnp.where(tok < r0, p_i - bc, NEG))
            a_rows.append(_dot_nt(q_i.astype(BF16), k_i.astype(BF16)))
        a_off = jnp.concatenate(a_rows, axis=0)
        o_parts = o_parts + _dot(a_off.astype(BF16), vb)

        o_diag = []
        for i in range(L // SUB):
            r0 = i * SUB
            q_i = q[r0:r0 + SUB, :]
            bc_i = bc[r0:r0 + SUB, :]
            acc = jnp.zeros((SUB, HEAD_DV), F32)
            for s in range(SUB):
                dec = jnp.exp(jnp.where(sub_row >= s, bc_i - bc[r0 + s:r0 + s + 1, :], NEG))
                a_s = jnp.sum(q_i * dec * k[r0 + s:r0 + s + 1, :], axis=-1, keepdims=True)
                acc = acc + a_s * v[r0 + s:r0 + s + 1, :]
            o_diag.append(acc)
        o_h = o_parts + jnp.concatenate(o_diag, axis=0)

        k_dec = k * jnp.exp(b_last - bc)
        st_ref[h] = st * jnp.exp(b_last) + _dot(v.T.astype(BF16), k_dec.astype(BF16))

        outs.append(o_h * lax.rsqrt(jnp.mean(o_h * o_h, axis=-1, keepdims=True) + EPS))
    merged = jnp.concatenate(outs, axis=-1) * nw_ref[...]
    out_ref[...] = (merged * _silu(g_ref[...])).astype(out_ref.dtype)


def hgrn_mix(proj, lb, norm_w, *, bsz, seq):
    t = bsz * seq
    nblk = seq // CHUNK
    row = lambda b, s: b * nblk + s
    return pl.pallas_call(
        _hgrn_kernel,
        grid=(bsz, nblk),
        in_specs=[pl.BlockSpec((CHUNK, D), lambda b, s: (row(b, s), 0)),
                  pl.BlockSpec((CHUNK, D), lambda b, s: (row(b, s), 1)),
                  pl.BlockSpec((CHUNK, D), lambda b, s: (row(b, s), 2)),
                  pl.BlockSpec((CHUNK, D), lambda b, s: (row(b, s), 3)),
                  pl.BlockSpec((1, D), lambda b, s: (0, 0)),
                  pl.BlockSpec((1, D), lambda b, s: (0, 0))],
        out_specs=pl.BlockSpec((CHUNK, D), lambda b, s: (row(b, s), 0)),
        out_shape=jax.ShapeDtypeStruct((t, D), BF16),
        scratch_shapes=[pltpu.VMEM((N_HEADS, HEAD_DV, HEAD_DV), F32)],
        compiler_params=_cparams("arbitrary", "arbitrary"),
        name="hgrn_mix",
    )(proj, proj, proj, proj, lb, norm_w)


def _out_proj_kernel(y_ref, w_ref, x_ref, mod_ref, o_ref, *, gate_row):
    o_ref[...] = x_ref[...] + mod_ref[0, gate_row:gate_row + 1, :] * _dot(y_ref[...], w_ref[...])


def out_proj(y, w, x, mod, *, seq, tm, gate_row):
    t = x.shape[0]
    per_b = seq // tm
    return pl.pallas_call(
        functools.partial(_out_proj_kernel, gate_row=gate_row),
        grid=(t // tm,),
        in_specs=[pl.BlockSpec((tm, D), lambda i: (i, 0)),
                  pl.BlockSpec((D, D), lambda i: (0, 0)),
                  pl.BlockSpec((tm, D), lambda i: (i, 0)),
                  pl.BlockSpec((1, 8, D), lambda i: (i // per_b, 0, 0))],
        out_specs=pl.BlockSpec((tm, D), lambda i: (i, 0)),
        out_shape=jax.ShapeDtypeStruct((t, D), F32),
        compiler_params=_cparams("arbitrary"),
        name="out_proj",
    )(y, w, x, mod)


def _ffn_kernel(x_ref, mod_ref, nw_ref, wg_ref, wu_ref, wd_ref, o_ref, hn_ref, acc_ref):
    f = pl.program_id(1)

    @pl.when(f == 0)
    def _():
        hn = _norm_mod(x_ref[...], nw_ref[...], mod_ref[0, 4:5, :], mod_ref[0, 3:4, :])
        hn_ref[...] = hn.astype(BF16)
        acc_ref[...] = jnp.zeros_like(acc_ref)

    hb = hn_ref[...]
    hid = _silu(_dot(hb, wg_ref[...])) * _dot(hb, wu_ref[...])
    acc_ref[...] += _dot(hid.astype(BF16), wd_ref[...])

    @pl.when(f == pl.num_programs(1) - 1)
    def _():
        o_ref[...] = x_ref[...] + mod_ref[0, 5:6, :] * acc_ref[...]


def ffn(x, mod, norm_w, wg, wu, wd, *, seq, tm, tf):
    t = x.shape[0]
    dff = wg.shape[1]
    per_b = seq // tm
    return pl.pallas_call(
        _ffn_kernel,
        grid=(t // tm, dff // tf),
        in_specs=[pl.BlockSpec((tm, D), lambda i, f: (i, 0)),
                  pl.BlockSpec((1, 8, D), lambda i, f: (i // per_b, 0, 0)),
                  pl.BlockSpec((1, D), lambda i, f: (0, 0)),
                  pl.BlockSpec((D, tf), lambda i, f: (0, f)),
                  pl.BlockSpec((D, tf), lambda i, f: (0, f)),
                  pl.BlockSpec((tf, D), lambda i, f: (f, 0))],
        out_specs=pl.BlockSpec((tm, D), lambda i, f: (i, 0)),
        out_shape=jax.ShapeDtypeStruct((t, D), F32),
        scratch_shapes=[pltpu.VMEM((tm, D), BF16), pltpu.VMEM((tm, D), F32)],
        compiler_params=_cparams("arbitrary", "arbitrary"),
        name="ffn",
    )(x, mod, norm_w, wg, wu, wd)


def _router_kernel(x_ref, mod_ref, nw_ref, wr_ref, hn_ref, rw_ref, ri_ref):
    hn = _norm_mod(x_ref[...], nw_ref[...], mod_ref[0, 4:5, :], mod_ref[0, 3:4, :])
    hn_ref[...] = hn
    logits = _dot_f32(hn, wr_ref[...])
    lane = lax.broadcasted_iota(jnp.int32, logits.shape, 1)
    lg = jnp.where(lane < N_EXPERTS, logits, NEG)
    m1 = jnp.max(lg, axis=-1, keepdims=True)
    i1 = jnp.min(jnp.where(lg == m1, lane, LANES), axis=-1, keepdims=True)
    lg2 = jnp.where(lane == i1, NEG, lg)
    m2 = jnp.max(lg2, axis=-1, keepdims=True)
    i2 = jnp.min(jnp.where(lg2 == m2, lane, LANES), axis=-1, keepdims=True)
    e2 = jnp.exp(m2 - m1)
    w1 = 1.0 / (1.0 + e2)
    rw_ref[...] = jnp.where(lane == 0, w1, jnp.where(lane == 1, e2 * w1, 0.0))
    ri_ref[...] = jnp.where(lane == 0, i1, jnp.where(lane == 1, i2, 0))


def router(x, mod, norm_w, wr, *, seq, tm):
    t = x.shape[0]
    per_b = seq // tm
    return pl.pallas_call(
        _router_kernel,
        grid=(t // tm,),
        in_specs=[pl.BlockSpec((tm, D), lambda i: (i, 0)),
                  pl.BlockSpec((1, 8, D), lambda i: (i // per_b, 0, 0)),
                  pl.BlockSpec((1, D), lambda i: (0, 0)),
                  pl.BlockSpec((D, LANES), lambda i: (0, 0))],
        out_specs=[pl.BlockSpec((tm, D), lambda i: (i, 0)),
                   pl.BlockSpec((tm, LANES), lambda i: (i, 0)),
                   pl.BlockSpec((tm, LANES), lambda i: (i, 0))],
        out_shape=[jax.ShapeDtypeStruct((t, D), F32),
                   jax.ShapeDtypeStruct((t, LANES), F32),
                   jax.ShapeDtypeStruct((t, LANES), jnp.int32)],
        compiler_params=_cparams("arbitrary"),
        name="router",
    )(x, mod, norm_w, wr)


def _row_copy(src_ref, src_row, dst_ref, dst_row, sem):
    return pltpu.make_async_copy(src_ref.at[pl.ds(src_row, 1)], dst_ref.at[pl.ds(dst_row, 1)], sem)


def _dispatch_kernel(dest_ref, hn_ref, xs_in_ref, xs_ref, sem):
    del xs_in_ref
    tm = hn_ref.shape[0]

    def body(r, carry):
        for k in range(2):
            _row_copy(hn_ref, r, xs_ref, dest_ref[0, 0, 2 * r + k], sem).start()
        return carry

    lax.fori_loop(0, tm, body, 0, unroll=4)
    for k in range(2):
        pltpu.make_async_copy(hn_ref, xs_ref.at[pl.ds(0, tm)], sem).wait()


def dispatch(dest, hn, xs_init, *, tm):
    t = hn.shape[0]
    return pl.pallas_call(
        _dispatch_kernel,
        grid=(t // tm,),
        in_specs=[pl.BlockSpec((1, 1, 2 * tm), lambda i: (i, 0, 0), memory_space=pltpu.SMEM),
                  pl.BlockSpec((tm, D), lambda i: (i, 0)),
                  pl.BlockSpec(memory_space=pl.ANY)],
        out_specs=pl.BlockSpec(memory_space=pl.ANY),
        out_shape=jax.ShapeDtypeStruct(xs_init.shape, F32),
        scratch_shapes=[pltpu.SemaphoreType.DMA],
        input_output_aliases={2: 0},
        compiler_params=_cparams("arbitrary"),
        name="moe_dispatch",
    )(dest, hn, xs_init)


def _expert_ffn_kernel(te_ref, nu_ref, x_ref, wg_ref, wu_ref, wd_ref, o_ref):
    del te_ref
    i = pl.program_id(0)

    @pl.when(i < nu_ref[0])
    def _():
        hb = x_ref[...].astype(BF16)
        hid = _silu(_dot(hb, wg_ref[0])) * _dot(hb, wu_ref[0])
        o_ref[...] = _dot(hid.astype(BF16), wd_ref[0])

    @pl.when(i >= nu_ref[0])
    def _():
        o_ref[...] = jnp.zeros_like(o_ref)


def expert_ffn(tile_expert, n_used, xs, wg, wu, wd, *, tg):
    r = xs.shape[0]
    dfe = wg.shape[2]
    grid_spec = pltpu.PrefetchScalarGridSpec(
        num_scalar_prefetch=2,
        grid=(r // tg,),
        in_specs=[pl.BlockSpec((tg, D), lambda i, te, nu: (i, 0)),
                  pl.BlockSpec((1, D, dfe), lambda i, te, nu: (te[i], 0, 0)),
                  pl.BlockSpec((1, D, dfe), lambda i, te, nu: (te[i], 0, 0)),
                  pl.BlockSpec((1, dfe, D), lambda i, te, nu: (te[i], 0, 0))],
        out_specs=pl.BlockSpec((tg, D), lambda i, te, nu: (i, 0)),
    )
    return pl.pallas_call(
        _expert_ffn_kernel,
        grid_spec=grid_spec,
        out_shape=jax.ShapeDtypeStruct((r, D), F32),
        compiler_params=_cparams("arbitrary"),
        name="moe_expert_ffn",
    )(tile_expert, n_used, xs, wg, wu, wd)


def _combine_kernel(dest_ref, x_ref, mod_ref, rw_ref, ys_ref, o_ref, buf_ref, sem):
    tm = x_ref.shape[0]

    def body(r, carry):
        for k in range(2):
            _row_copy(ys_ref, dest_ref[0, 0, 2 * r + k], buf_ref.at[k], r, sem).start()
        return carry

    lax.fori_loop(0, tm, body, 0, unroll=4)
    for k in range(2):
        pltpu.make_async_copy(ys_ref.at[pl.ds(0, tm)], buf_ref.at[k], sem).wait()
    rw = rw_ref[...]
    y = rw[:, 0:1] * buf_ref[0] + rw[:, 1:2] * buf_ref[1]
    o_ref[...] = x_ref[...] + mod_ref[0, 5:6, :] * y


def combine(dest, x, mod, rw, ys, *, seq, tm):
    t = x.shape[0]
    per_b = seq // tm
    return pl.pallas_call(
        _combine_kernel,
        grid=(t // tm,),
        in_specs=[pl.BlockSpec((1, 1, 2 * tm), lambda i: (i, 0, 0), memory_space=pltpu.SMEM),
                  pl.BlockSpec((tm, D), lambda i: (i, 0)),
                  pl.BlockSpec((1, 8, D), lambda i: (i // per_b, 0, 0)),
                  pl.BlockSpec((tm, LANES), lambda i: (i, 0)),
                  pl.BlockSpec(memory_space=pl.ANY)],
        out_specs=pl.BlockSpec((tm, D), lambda i: (i, 0)),
        out_shape=jax.ShapeDtypeStruct((t, D), F32),
        scratch_shapes=[pltpu.VMEM((2, tm, D), F32), pltpu.SemaphoreType.DMA],
        compiler_params=_cparams("arbitrary"),
        name="moe_combine",
    )(dest, x, mod, rw, ys)


def _routing_plan(route_i, *, tm, tg):
    t = route_i.shape[0]
    e_flat = route_i[:, :2].reshape(-1)
    onehot = (e_flat[:, None] == jnp.arange(N_EXPERTS, dtype=jnp.int32)[None, :]).astype(jnp.int32)
    csum = jnp.cumsum(onehot, axis=0)
    counts = csum[-1]
    padded = ((counts + tg - 1) // tg) * tg
    ends = jnp.cumsum(padded)
    starts = ends - padded
    dest = jnp.sum(onehot * (starts[None, :] + csum - onehot), axis=1)
    n_tiles = (2 * t) // tg + N_EXPERTS
    tile_start = jnp.arange(n_tiles, dtype=jnp.int32) * tg
    tile_expert = jnp.sum((tile_start[:, None] >= ends[None, :]).astype(jnp.int32), axis=1)
    n_used = (ends[-1] // tg).astype(jnp.int32)
    last_expert = jnp.take(tile_expert, jnp.maximum(n_used - 1, 0))
    tile_expert = jnp.where(tile_start < ends[-1], tile_expert, last_expert).astype(jnp.int32)
    return dest.reshape(t // tm, 1, 2 * tm).astype(jnp.int32), tile_expert, n_used.reshape(1), n_tiles


def moe(x, mod, norm_w, wr, wg, wu, wd, *, seq, tm, tg):
    hn, rw, ri = router(x, mod, norm_w, wr, seq=seq, tm=tm)
    dest, tile_expert, n_used, n_tiles = _routing_plan(ri, tm=tm, tg=tg)
    xs = dispatch(dest, hn, jnp.zeros((n_tiles * tg, D), F32), tm=tm)
    ys = expert_ffn(tile_expert, n_used, xs, wg, wu, wd, tg=tg)
    return combine(dest, x, mod, rw, ys, seq=seq, tm=tm)


def _final_norm_kernel(x_ref, w_ref, o_ref):
    x = x_ref[...]
    o_ref[...] = x * lax.rsqrt(jnp.mean(x * x, axis=-1, keepdims=True) + EPS) * w_ref[...]


def final_norm(x, w, *, tm):
    t = x.shape[0]
    return pl.pallas_call(
        _final_norm_kernel,
        grid=(t // tm,),
        in_specs=[pl.BlockSpec((tm, D), lambda i: (i, 0)), pl.BlockSpec((1, D), lambda i: (0, 0))],
        out_specs=pl.BlockSpec((tm, D), lambda i: (i, 0)),
        out_shape=jax.ShapeDtypeStruct((t, D), F32),
        compiler_params=_cparams("arbitrary"),
        name="final_norm",
    )(x, w)


def _pad_cols(w, n):
    return jnp.pad(w, ((0, 0), (0, n - w.shape[1])))


def kernel(x, c, ada_w, ada_b, norm1_w, norm2_w, final_norm_w, m_w_in, m_i_bias, m_f_bias, m_conv_w, m_conv_b, m_norm_w, m_w_out, h_w_in, h_lb_logits, h_norm_w, h_w_out, ffn_w_gate, ffn_w_up, ffn_w_down, moe_router, moe_w_gate, moe_w_up, moe_w_down):
    bsz, seq, _ = x.shape
    t = bsz * seq
    tm = min(512, seq)
    xt = x.reshape(t, D)

    mod_all = ada_modulation(c, ada_w, ada_b)
    mod_all = jnp.pad(mod_all.reshape(DEPTH, bsz, 6, D), ((0, 0), (0, 0), (0, 2), (0, 0)))

    gam = jax.nn.softmax(h_lb_logits.astype(F32), axis=0)
    lower_bounds = jnp.cumsum(gam, axis=0) - gam[:1]

    n_main = 2 * N_HEADS * M_DQK + 2 * N_HEADS * HEAD_DV
    zero_extra = jnp.zeros((D, LANES), F32)
    for layer in range(DEPTH):
        j = layer // 2
        mod = mod_all[layer]
        n1 = norm1_w[layer].reshape(1, D)
        n2 = norm2_w[layer].reshape(1, D)
        if layer % 2 == 0:
            w_in = m_w_in[j]
            proj, gates = in_proj(xt, mod, n1, w_in[:, :n_main].astype(BF16), _pad_cols(w_in[:, n_main:], LANES),
                                  seq=seq, tm=tm)
            gate_bias = _pad_cols(jnp.concatenate([m_i_bias[j], m_f_bias[j]]).reshape(1, -1), LANES)
            y = mlstm_mix(proj, gates, m_conv_w[j], m_conv_b[j].reshape(1, -1), gate_bias,
                          m_norm_w[j].reshape(1, -1), bsz=bsz, seq=seq)
            xt = out_proj(y, m_w_out[j].astype(BF16), xt, mod, seq=seq, tm=tm, gate_row=2)
            xt = ffn(xt, mod, n2, ffn_w_gate[j].astype(BF16), ffn_w_up[j].astype(BF16),
                     ffn_w_down[j].astype(BF16), seq=seq, tm=tm, tf=1408)
        else:
            proj, _ = in_proj(xt, mod, n1, h_w_in[j].astype(BF16), zero_extra, seq=seq, tm=tm)
            y = hgrn_mix(proj, lower_bounds[j].reshape(1, D), h_norm_w[j].reshape(1, D), bsz=bsz, seq=seq)
            xt = out_proj(y, h_w_out[j].astype(BF16), xt, mod, seq=seq, tm=tm, gate_row=2)
            xt = moe(xt, mod, n2, _pad_cols(moe_router[j], LANES), moe_w_gate[j].astype(BF16),
                     moe_w_up[j].astype(BF16), moe_w_down[j].astype(BF16), seq=seq, tm=tm, tg=tm)
    out = final_norm(xt, final_norm_w.reshape(1, D), tm=tm)
    return out.reshape(bsz, seq, D)
```

```python
import functools

import jax
import jax.numpy as jnp
from jax import lax
from jax.experimental import pallas as pl
from jax.experimental.pallas import tpu as pltpu

F32 = jnp.float32
BF16 = jnp.bfloat16

DEPTH = 4
D = 1024
N_HEADS = 8
HEAD_DV = 128
M_DQK = 64
CONV_K = 4
IGATE_CAP = 15.0
N_EXPERTS = 8
EPS = 1e-6
NEG = -1e30
F_FLOOR = 1e-30
LANES = 128
CHUNK = 128
SUB = 16
VMEM_LIMIT = 56 * 1024 * 1024


def _cparams(*sem):
    return pltpu.CompilerParams(dimension_semantics=sem, vmem_limit_bytes=VMEM_LIMIT)


def _sigmoid(x):
    return 1.0 / (1.0 + jnp.exp(-x))


def _silu(x):
    return x * _sigmoid(x)


def _dot(a, b):
    return jnp.dot(a, b, preferred_element_type=F32)


def _dot_nt(a, b):
    return lax.dot_general(a, b, (((1,), (1,)), ((), ())), preferred_element_type=F32)


def _dot_f32(a, b):
    return jnp.dot(a, b, preferred_element_type=F32, precision=lax.Precision.HIGHEST)


def _norm_mod(x, norm_w, scale, shift):
    y = x * lax.rsqrt(jnp.mean(x * x, axis=-1, keepdims=True) + EPS)
    return (y * norm_w) * (1.0 + scale) + shift


def _ada_kernel(c_ref, w_ref, b_ref, o_ref):
    c = c_ref[...]
    o_ref[0] = _dot_f32(_silu(c), w_ref[0]) + b_ref[0]


def ada_modulation(c, ada_w, ada_b):
    bsz = c.shape[0]
    tn = 1536
    return pl.pallas_call(
        _ada_kernel,
        grid=(DEPTH, 6 * D // tn),
        in_specs=[pl.BlockSpec((bsz, D), lambda l, j: (0, 0)),
                  pl.BlockSpec((1, D, tn), lambda l, j: (l, 0, j)),
                  pl.BlockSpec((1, 1, tn), lambda l, j: (l, 0, j))],
        out_specs=pl.BlockSpec((1, bsz, tn), lambda l, j: (l, 0, j)),
        out_shape=jax.ShapeDtypeStruct((DEPTH, bsz, 6 * D), F32),
        compiler_params=_cparams("arbitrary", "arbitrary"),
        name="ada_modulation",
    )(c, ada_w, ada_b.reshape(DEPTH, 1, 6 * D))


def _in_proj_kernel(x_ref, mod_ref, nw_ref, w_ref, wx_ref, o_ref, ox_ref, *, n_chunk):
    hn = _norm_mod(x_ref[...], nw_ref[...], mod_ref[0, 1:2, :], mod_ref[0, 0:1, :])
    ox_ref[...] = _dot_f32(hn, wx_ref[...])
    hb = hn.astype(BF16)
    for n0 in range(0, o_ref.shape[1], n_chunk):
        o_ref[:, n0:n0 + n_chunk] = _dot(hb, w_ref[:, n0:n0 + n_chunk])


def in_proj(x, mod, norm_w, w, w_extra, *, seq, tm):
    t = x.shape[0]
    n = w.shape[1]
    per_b = seq // tm
    return pl.pallas_call(
        functools.partial(_in_proj_kernel, n_chunk=512),
        grid=(t // tm,),
        in_specs=[pl.BlockSpec((tm, D), lambda i: (i, 0)),
                  pl.BlockSpec((1, 8, D), lambda i: (i // per_b, 0, 0)),
                  pl.BlockSpec((1, D), lambda i: (0, 0)),
                  pl.BlockSpec((D, n), lambda i: (0, 0)),
                  pl.BlockSpec((D, LANES), lambda i: (0, 0))],
        out_specs=[pl.BlockSpec((tm, n), lambda i: (i, 0)),
                   pl.BlockSpec((tm, LANES), lambda i: (i, 0))],
        out_shape=[jax.ShapeDtypeStruct((t, n), F32),
                   jax.ShapeDtypeStruct((t, LANES), F32)],
        compiler_params=_cparams("arbitrary"),
        name="in_proj",
    )(x, mod, norm_w, w, w_extra)


def _mlstm_kernel(qk_ref, v_ref, o_ref, g_ref, cw_ref, cb_ref, gb_ref, nw_ref, out_ref,
                  tail_ref, c_ref, m_ref):
    L = CHUNK

    @pl.when(pl.program_id(1) == 0)
    def _():
        tail_ref[...] = jnp.zeros_like(tail_ref)
        c_ref[...] = jnp.zeros_like(c_ref)
        m_ref[...] = jnp.zeros_like(m_ref)

    raw = qk_ref[...]
    xx = jnp.concatenate([tail_ref[...], raw], axis=0)
    tail_ref[...] = raw[L - 8:, :]
    y = cb_ref[...] + cw_ref[CONV_K - 1:CONV_K, :] * raw
    for j in range(CONV_K - 1):
        y = y + cw_ref[j:j + 1, :] * pltpu.roll(xx, CONV_K - 1 - j, axis=0)[8:, :]
    qk = _silu(y)
    half = N_HEADS * M_DQK
    q_all = qk[:, :half]
    k_all = qk[:, half:] * (M_DQK ** -0.5)

    g = g_ref[...] + gb_ref[...]
    lane_g = lax.broadcasted_iota(jnp.int32, g.shape, 1)
    li_all = IGATE_CAP * jnp.tanh(g * (1.0 / IGATE_CAP))
    lf_all = jnp.minimum(g, 0.0) - jnp.log1p(jnp.exp(-jnp.abs(g)))
    gates = jnp.where(lane_g < N_HEADS, li_all, lf_all)
    row = lax.broadcasted_iota(jnp.int32, (L, L), 0)
    col = lax.broadcasted_iota(jnp.int32, (L, L), 1)
    tril = (col <= row)
    csum = _dot_f32(tril.astype(F32), gates)

    li_r = gates.T[0:N_HEADS, :]
    b_r = csum.T[N_HEADS:2 * N_HEADS, :]
    a_r = li_r - b_r
    lane_r = lax.broadcasted_iota(jnp.int32, a_r.shape, 1)
    amax_r = a_r
    for sh in (1, 2, 4, 8, 16, 32, 64):
        amax_r = jnp.where(lane_r >= sh, jnp.maximum(amax_r, pltpu.roll(amax_r, sh, axis=1)), amax_r)
    m_prev = m_ref[...]
    mx_r = jnp.maximum(m_prev, amax_r)
    w_inter_r = jnp.exp(m_prev - mx_r)
    floor_r = jnp.exp(-(b_r + mx_r))
    mx_last = mx_r[:, L - 1:L]
    wk_r = jnp.exp(a_r - mx_last)
    m_ref[...] = jnp.broadcast_to(b_r[:, L - 1:L] + mx_last, m_prev.shape)
    decay_r = jnp.broadcast_to(w_inter_r[:, L - 1:L], m_prev.shape)
    cols = jnp.concatenate([mx_r, w_inter_r, floor_r, wk_r,
                            jnp.zeros((LANES - 4 * N_HEADS, L), F32)], axis=0).T

    lane = lax.broadcasted_iota(jnp.int32, (L, LANES), 1)
    ones_b = jnp.ones((L, LANES), BF16)
    qb_l, kb_l, ve_l, col_l = [], [], [], []
    for h in range(N_HEADS):
        p, hf = h // 2, h % 2
        in_head = (lane >= hf * M_DQK) & (lane < (hf + 1) * M_DQK)
        qb_l.append(jnp.where(in_head, q_all[:, p * LANES:(p + 1) * LANES], 0.0).astype(BF16))
        kb_l.append(k_all[:, p * LANES:(p + 1) * LANES])
        ve_l.append(jnp.concatenate([v_ref[:, h * HEAD_DV:(h + 1) * HEAD_DV].astype(BF16), ones_b], axis=-1))
        col_l.append([cols[:, j * N_HEADS + h:j * N_HEADS + h + 1] for j in range(4)])
    qb = jnp.stack(qb_l)
    k_p = jnp.stack(kb_l)
    v_ext = jnp.stack(ve_l)
    mx_col, w_inter, floor, wk = (jnp.stack([c[j] for c in col_l]) for j in range(4))
    c_st = c_ref[...]

    pmat = jnp.exp(jnp.where(tril[None], a_r[:, None, :] - mx_col, NEG))
    s = jnp.einsum('hqd,hkd->hqk', qb, k_p.astype(BF16), preferred_element_type=F32) * pmat
    nd = (w_inter * jnp.einsum('hqd,hde->hqe', qb, c_st.astype(BF16), preferred_element_type=F32)
          + jnp.einsum('hqk,hke->hqe', s.astype(BF16), v_ext, preferred_element_type=F32))
    hh = nd[:, :, :HEAD_DV] / jnp.maximum(jnp.abs(nd[:, :, HEAD_DV:]), floor)

    kw = (k_p * wk).astype(BF16)
    for h in range(N_HEADS):
        c_ref[h] = decay_r[h:h + 1, :1] * c_st[h] + lax.dot_general(
            kw[h], v_ext[h], (((0,), (0,)), ((), ())), preferred_element_type=F32)

    ms = _dot((hh * hh).astype(BF16).reshape(N_HEADS * L, HEAD_DV), ones_b).reshape(hh.shape) * (1.0 / HEAD_DV)
    hn = hh * lax.rsqrt(ms + EPS)
    merged = jnp.concatenate([hn[h] for h in range(N_HEADS)], axis=-1) * nw_ref[...]
    out_ref[...] = (merged * _sigmoid(o_ref[...])).astype(out_ref.dtype)


def mlstm_mix(proj, gates, conv_w, conv_b, gate_bias, norm_w, *, bsz, seq):
    t = bsz * seq
    nblk = seq // CHUNK
    dv = N_HEADS * HEAD_DV
    row = lambda b, s: b * nblk + s
    return pl.pallas_call(
        _mlstm_kernel,
        grid=(bsz, nblk),
        in_specs=[pl.BlockSpec((CHUNK, D), lambda b, s: (row(b, s), 0)),
                  pl.BlockSpec((CHUNK, dv), lambda b, s: (row(b, s), 1)),
                  pl.BlockSpec((CHUNK, dv), lambda b, s: (row(b, s), 2)),
                  pl.BlockSpec((CHUNK, LANES), lambda b, s: (row(b, s), 0)),
                  pl.BlockSpec((CONV_K, D), lambda b, s: (0, 0)),
                  pl.BlockSpec((1, D), lambda b, s: (0, 0)),
                  pl.BlockSpec((1, LANES), lambda b, s: (0, 0)),
                  pl.BlockSpec((1, dv), lambda b, s: (0, 0))],
        out_specs=pl.BlockSpec((CHUNK, dv), lambda b, s: (row(b, s), 0)),
        out_shape=jax.ShapeDtypeStruct((t, dv), BF16),
        scratch_shapes=[pltpu.VMEM((8, D), F32),
                        pltpu.VMEM((N_HEADS, LANES, 2 * HEAD_DV), F32),
                        pltpu.VMEM((N_HEADS, LANES), F32)],
        compiler_params=_cparams("arbitrary", "arbitrary"),
        name="mlstm_mix",
    )(proj, proj, proj, gates, conv_w, conv_b, gate_bias, norm_w)


def _hgrn_kernel(q_ref, f_ref, v_ref, g_ref, lb_ref, nw_ref, out_ref, st_ref):
    L = CHUNK

    @pl.when(pl.program_id(1) == 0)
    def _():
        st_ref[...] = jnp.zeros_like(st_ref)

    lb = lb_ref[...]
    f = f_ref[...]
    e = jnp.exp(-jnp.abs(f))
    r = 1.0 / (1.0 + e)
    er = e * r
    sig = jnp.where(f >= 0.0, r, er)
    nsig = jnp.where(f >= 0.0, er, r)
    g_all = jnp.log(jnp.maximum(lb + (1.0 - lb) * sig, F_FLOOR))
    k_all = (1.0 - lb) * nsig
    q_all = _silu(q_ref[...])
    row = lax.broadcasted_iota(jnp.int32, (L, L), 0)
    col = lax.broadcasted_iota(jnp.int32, (L, L), 1)
    tri_f = (col <= row).astype(F32)
    bc_all = _dot_f32(tri_f, g_all)
    xor = row ^ col
    eye = row == col
    ones_b = jnp.ones((LANES, LANES), BF16)
    sub8 = lax.broadcasted_iota(jnp.int32, (1, 8, LANES), 1)
    H = N_HEADS

    def heads(x):
        return jnp.stack([x[:, h * HEAD_DV:(h + 1) * HEAD_DV] for h in range(H)])

    def bmm_nt(x, y):
        return jnp.einsum('hqd,hkd->hqk', x.astype(BF16), y.astype(BF16), preferred_element_type=F32)

    def rot8(x3, d):
        return pltpu.roll(x3, d % 8, axis=1)

    q, k, g, bc = heads(q_all), heads(k_all), heads(g_all), heads(bc_all)
    vb = heads(v_ref[...]).astype(BF16)
    st = st_ref[...]
    b_last = bc[:, L - 1:L, :]

    o = bmm_nt(q * jnp.exp(bc), st)

    diag = _dot((q * k).astype(BF16).reshape(H * L, HEAD_DV), ones_b).reshape(H, L, L)
    a = jnp.where(eye[None], diag, 0.0)
    for b in (64, 32, 16, 8):
        qs, ks = [], []
        for blk in range(L // (2 * b)):
            lo, mid, hi = 2 * b * blk, 2 * b * blk + b, 2 * b * (blk + 1)
            ref = bc[:, mid - 1:mid, :]
            ks.append(k[:, lo:mid, :] * jnp.exp(ref - bc[:, lo:mid, :]))
            ks.append(jnp.zeros((H, b, LANES), F32))
            qs.append(jnp.zeros((H, b, LANES), F32))
            qs.append(q[:, mid:hi, :] * jnp.exp(bc[:, mid:hi, :] - ref))
        a_l = bmm_nt(jnp.concatenate(qs, axis=1), jnp.concatenate(ks, axis=1))
        a = jnp.where(((xor >= b) & (xor < 2 * b))[None], a_l, a)
    n8 = H * L // 8
    e8 = jnp.exp(g).reshape(n8, 8, LANES)
    q8 = q.reshape(n8, 8, LANES)
    k8 = k.reshape(n8, 8, LANES)
    ew = [e8]
    for j in range(1, 4):
        ew.append(ew[-1] * rot8(e8, j))
    eu = [rot8(e8, -1)]
    for j in range(1, 3):
        eu.append(eu[-1] * rot8(e8, -1 - j))
    for b in (4, 2, 1):
        pos = sub8 % (2 * b)
        fq = jnp.where(pos == b, ew[0], 0.0)
        fk = jnp.where(pos == b - 1, 1.0, 0.0)
        for j in range(b - 1):
            fq = jnp.where(pos == b + 1 + j, ew[1 + j], fq)
            fk = jnp.where(pos == j, eu[b - 2 - j], fk)
        a_l = bmm_nt((q8 * fq).reshape(H, L, LANES), (k8 * fk).reshape(H, L, LANES))
        a = jnp.where(((xor >= b) & (xor < 2 * b))[None], a_l, a)
    o = o + jnp.einsum('hqk,hkv->hqv', a.astype(BF16), vb, preferred_element_type=F32)

    k_dec = (k * jnp.exp(b_last - bc)).astype(BF16)
    st_decay = jnp.exp(b_last)
    for h in range(H):
        st_ref[h] = st[h] * st_decay[h] + lax.dot_general(
            vb[h], k_dec[h], (((0,), (0,)), ((), ())), preferred_element_type=F32)

    ms = _dot((o * o).astype(BF16).reshape(H * L, HEAD_DV), ones_b).reshape(o.shape) * (1.0 / HEAD_DV)
    on = o * lax.rsqrt(ms + EPS)
    merged = jnp.concatenate([on[h] for h in range(H)], axis=-1) * nw_ref[...]
    out_ref[...] = (merged * _silu(g_ref[...])).astype(out_ref.dtype)


def hgrn_mix(proj, lb, norm_w, *, bsz, seq):
    t = bsz * seq
    nblk = seq // CHUNK
    row = lambda b, s: b * nblk + s
    return pl.pallas_call(
        _hgrn_kernel,
        grid=(bsz, nblk),
        in_specs=[pl.BlockSpec((CHUNK, D), lambda b, s: (row(b, s), 0)),
                  pl.BlockSpec((CHUNK, D), lambda b, s: (row(b, s), 1)),
                  pl.BlockSpec((CHUNK, D), lambda b, s: (row(b, s), 2)),
                  pl.BlockSpec((CHUNK, D), lambda b, s: (row(b, s), 3)),
                  pl.BlockSpec((1, D), lambda b, s: (0, 0)),
                  pl.BlockSpec((1, D), lambda b, s: (0, 0))],
        out_specs=pl.BlockSpec((CHUNK, D), lambda b, s: (row(b, s), 0)),
        out_shape=jax.ShapeDtypeStruct((t, D), BF16),
        scratch_shapes=[pltpu.VMEM((N_HEADS, HEAD_DV, HEAD_DV), F32)],
        compiler_params=_cparams("arbitrary", "arbitrary"),
        name="hgrn_mix",
    )(proj, proj, proj, proj, lb, norm_w)


def _out_proj_kernel(y_ref, w_ref, x_ref, mod_ref, o_ref, *, gate_row):
    o_ref[...] = x_ref[...] + mod_ref[0, gate_row:gate_row + 1, :] * _dot(y_ref[...], w_ref[...])


def out_proj(y, w, x, mod, *, seq, tm, gate_row):
    t = x.shape[0]
    per_b = seq // tm
    return pl.pallas_call(
        functools.partial(_out_proj_kernel, gate_row=gate_row),
        grid=(t // tm,),
        in_specs=[pl.BlockSpec((tm, D), lambda i: (i, 0)),
                  pl.BlockSpec((D, D), lambda i: (0, 0)),
                  pl.BlockSpec((tm, D), lambda i: (i, 0)),
                  pl.BlockSpec((1, 8, D), lambda i: (i // per_b, 0, 0))],
        out_specs=pl.BlockSpec((tm, D), lambda i: (i, 0)),
        out_shape=jax.ShapeDtypeStruct((t, D), F32),
        compiler_params=_cparams("arbitrary"),
        name="out_proj",
    )(y, w, x, mod)


def _ffn_kernel(x_ref, mod_ref, nw_ref, wg_ref, wu_ref, wd_ref, o_ref, hn_ref, acc_ref):
    f = pl.program_id(1)

    @pl.when(f == 0)
    def _():
        hn = _norm_mod(x_ref[...], nw_ref[...], mod_ref[0, 4:5, :], mod_ref[0, 3:4, :])
        hn_ref[...] = hn.astype(BF16)
        acc_ref[...] = jnp.zeros_like(acc_ref)

    hb = hn_ref[...]
    hid = _silu(_dot(hb, wg_ref[...])) * _dot(hb, wu_ref[...])
    acc_ref[...] += _dot(hid.astype(BF16), wd_ref[...])

    @pl.when(f == pl.num_programs(1) - 1)
    def _():
        o_ref[...] = x_ref[...] + mod_ref[0, 5:6, :] * acc_ref[...]


def ffn(x, mod, norm_w, wg, wu, wd, *, seq, tm, tf):
    t = x.shape[0]
    dff = wg.shape[1]
    per_b = seq // tm
    return pl.pallas_call(
        _ffn_kernel,
        grid=(t // tm, dff // tf),
        in_specs=[pl.BlockSpec((tm, D), lambda i, f: (i, 0)),
                  pl.BlockSpec((1, 8, D), lambda i, f: (i // per_b, 0, 0)),
                  pl.BlockSpec((1, D), lambda i, f: (0, 0)),
                  pl.BlockSpec((D, tf), lambda i, f: (0, f)),
                  pl.BlockSpec((D, tf), lambda i, f: (0, f)),
                  pl.BlockSpec((tf, D), lambda i, f: (f, 0))],
        out_specs=pl.BlockSpec((tm, D), lambda i, f: (i, 0)),
        out_shape=jax.ShapeDtypeStruct((t, D), F32),
        scratch_shapes=[pltpu.VMEM((tm, D), BF16), pltpu.VMEM((tm, D), F32)],
        compiler_params=_cparams("arbitrary", "arbitrary"),
        name="ffn",
    )(x, mod, norm_w, wg, wu, wd)


def _router_kernel(x_ref, mod_ref, nw_ref, wr_ref, hn_ref, rw_ref, ri_ref):
    hn = _norm_mod(x_ref[...], nw_ref[...], mod_ref[0, 4:5, :], mod_ref[0, 3:4, :])
    hn_ref[...] = hn
    logits = _dot_f32(hn, wr_ref[...])
    lane = lax.broadcasted_iota(jnp.int32, logits.shape, 1)
    lg = jnp.where(lane < N_EXPERTS, logits, NEG)
    m1 = jnp.max(lg, axis=-1, keepdims=True)
    i1 = jnp.min(jnp.where(lg == m1, lane, LANES), axis=-1, keepdims=True)
    lg2 = jnp.where(lane == i1, NEG, lg)
    m2 = jnp.max(lg2, axis=-1, keepdims=True)
    i2 = jnp.min(jnp.where(lg2 == m2, lane, LANES), axis=-1, keepdims=True)
    e2 = jnp.exp(m2 - m1)
    w1 = 1.0 / (1.0 + e2)
    rw_ref[...] = jnp.where(lane == 0, w1, jnp.where(lane == 1, e2 * w1, 0.0))
    ri_ref[...] = jnp.where(lane == 0, i1, jnp.where(lane == 1, i2, 0))


def router(x, mod, norm_w, wr, *, seq, tm):
    t = x.shape[0]
    per_b = seq // tm
    return pl.pallas_call(
        _router_kernel,
        grid=(t // tm,),
        in_specs=[pl.BlockSpec((tm, D), lambda i: (i, 0)),
                  pl.BlockSpec((1, 8, D), lambda i: (i // per_b, 0, 0)),
                  pl.BlockSpec((1, D), lambda i: (0, 0)),
                  pl.BlockSpec((D, LANES), lambda i: (0, 0))],
        out_specs=[pl.BlockSpec((tm, D), lambda i: (i, 0)),
                   pl.BlockSpec((tm, LANES), lambda i: (i, 0)),
                   pl.BlockSpec((tm, LANES), lambda i: (i, 0))],
        out_shape=[jax.ShapeDtypeStruct((t, D), F32),
                   jax.ShapeDtypeStruct((t, LANES), F32),
                   jax.ShapeDtypeStruct((t, LANES), jnp.int32)],
        compiler_params=_cparams("arbitrary"),
        name="router",
    )(x, mod, norm_w, wr)


def _row_copy(src_ref, src_row, dst_ref, dst_row, sem):
    return pltpu.make_async_copy(src_ref.at[pl.ds(src_row, 1)], dst_ref.at[pl.ds(dst_row, 1)], sem)


def _dispatch_kernel(dest_ref, hn_ref, xs_in_ref, xs_ref, sem):
    del xs_in_ref
    tm = hn_ref.shape[0]

    def body(r, carry):
        for k in range(2):
            _row_copy(hn_ref, r, xs_ref, dest_ref[0, 0, 2 * r + k], sem).start()
        return carry

    lax.fori_loop(0, tm, body, 0, unroll=4)
    for k in range(2):
        pltpu.make_async_copy(hn_ref, xs_ref.at[pl.ds(0, tm)], sem).wait()


def dispatch(dest, hn, xs_init, *, tm):
    t = hn.shape[0]
    return pl.pallas_call(
        _dispatch_kernel,
        grid=(t // tm,),
        in_specs=[pl.BlockSpec((1, 1, 2 * tm), lambda i: (i, 0, 0), memory_space=pltpu.SMEM),
                  pl.BlockSpec((tm, D), lambda i: (i, 0)),
                  pl.BlockSpec(memory_space=pl.ANY)],
        out_specs=pl.BlockSpec(memory_space=pl.ANY),
        out_shape=jax.ShapeDtypeStruct(xs_init.shape, F32),
        scratch_shapes=[pltpu.SemaphoreType.DMA],
        input_output_aliases={2: 0},
        compiler_params=_cparams("arbitrary"),
        name="moe_dispatch",
    )(dest, hn, xs_init)


def _expert_ffn_kernel(te_ref, nu_ref, x_ref, wg_ref, wu_ref, wd_ref, o_ref):
    del te_ref
    i = pl.program_id(0)

    @pl.when(i < nu_ref[0])
    def _():
        hb = x_ref[...].astype(BF16)
        hid = _silu(_dot(hb, wg_ref[0])) * _dot(hb, wu_ref[0])
        o_ref[...] = _dot(hid.astype(BF16), wd_ref[0])

    @pl.when(i >= nu_ref[0])
    def _():
        o_ref[...] = jnp.zeros_like(o_ref)


def expert_ffn(tile_expert, n_used, xs, wg, wu, wd, *, tg):
    r = xs.shape[0]
    dfe = wg.shape[2]
    grid_spec = pltpu.PrefetchScalarGridSpec(
        num_scalar_prefetch=2,
        grid=(r // tg,),
        in_specs=[pl.BlockSpec((tg, D), lambda i, te, nu: (i, 0)),
                  pl.BlockSpec((1, D, dfe), lambda i, te, nu: (te[i], 0, 0)),
                  pl.BlockSpec((1, D, dfe), lambda i, te, nu: (te[i], 0, 0)),
                  pl.BlockSpec((1, dfe, D), lambda i, te, nu: (te[i], 0, 0))],
        out_specs=pl.BlockSpec((tg, D), lambda i, te, nu: (i, 0)),
    )
    return pl.pallas_call(
        _expert_ffn_kernel,
        grid_spec=grid_spec,
        out_shape=jax.ShapeDtypeStruct((r, D), F32),
        compiler_params=_cparams("arbitrary"),
        name="moe_expert_ffn",
    )(tile_expert, n_used, xs, wg, wu, wd)


def _combine_kernel(dest_ref, x_ref, mod_ref, rw_ref, ys_ref, o_ref, buf_ref, sem):
    tm = x_ref.shape[0]

    def body(r, carry):
        for k in range(2):
            _row_copy(ys_ref, dest_ref[0, 0, 2 * r + k], buf_ref.at[k], r, sem).start()
        return carry

    lax.fori_loop(0, tm, body, 0, unroll=4)
    for k in range(2):
        pltpu.make_async_copy(ys_ref.at[pl.ds(0, tm)], buf_ref.at[k], sem).wait()
    rw = rw_ref[...]
    y = rw[:, 0:1] * buf_ref[0] + rw[:, 1:2] * buf_ref[1]
    o_ref[...] = x_ref[...] + mod_ref[0, 5:6, :] * y


def combine(dest, x, mod, rw, ys, *, seq, tm):
    t = x.shape[0]
    per_b = seq // tm
    return pl.pallas_call(
        _combine_kernel,
        grid=(t // tm,),
        in_specs=[pl.BlockSpec((1, 1, 2 * tm), lambda i: (i, 0, 0), memory_space=pltpu.SMEM),
                  pl.BlockSpec((tm, D), lambda i: (i, 0)),
                  pl.BlockSpec((1, 8, D), lambda i: (i // per_b, 0, 0)),
                  pl.BlockSpec((tm, LANES), lambda i: (i, 0)),
                  pl.BlockSpec(memory_space=pl.ANY)],
        out_specs=pl.BlockSpec((tm, D), lambda i: (i, 0)),
        out_shape=jax.ShapeDtypeStruct((t, D), F32),
        scratch_shapes=[pltpu.VMEM((2, tm, D), F32), pltpu.SemaphoreType.DMA],
        compiler_params=_cparams("arbitrary"),
        name="moe_combine",
    )(dest, x, mod, rw, ys)


def _routing_plan(route_i, *, tm, tg):
    t = route_i.shape[0]
    e_flat = route_i[:, :2].reshape(-1)
    onehot = (e_flat[:, None] == jnp.arange(N_EXPERTS, dtype=jnp.int32)[None, :]).astype(jnp.int32)
    csum = jnp.cumsum(onehot, axis=0)
    counts = csum[-1]
    padded = ((counts + tg - 1) // tg) * tg
    ends = jnp.cumsum(padded)
    starts = ends - padded
    dest = jnp.sum(onehot * (starts[None, :] + csum - onehot), axis=1)
    n_tiles = (2 * t) // tg + N_EXPERTS
    tile_start = jnp.arange(n_tiles, dtype=jnp.int32) * tg
    tile_expert = jnp.sum((tile_start[:, None] >= ends[None, :]).astype(jnp.int32), axis=1)
    n_used = (ends[-1] // tg).astype(jnp.int32)
    last_expert = jnp.take(tile_expert, jnp.maximum(n_used - 1, 0))
    tile_expert = jnp.where(tile_start < ends[-1], tile_expert, last_expert).astype(jnp.int32)
    return dest.reshape(t // tm, 1, 2 * tm).astype(jnp.int32), tile_expert, n_used.reshape(1), n_tiles


def moe(x, mod, norm_w, wr, wg, wu, wd, *, seq, tm, tg):
    hn, rw, ri = router(x, mod, norm_w, wr, seq=seq, tm=tm)
    dest, tile_expert, n_used, n_tiles = _routing_plan(ri, tm=tm, tg=tg)
    xs = dispatch(dest, hn, jnp.zeros((n_tiles * tg, D), F32), tm=tm)
    ys = expert_ffn(tile_expert, n_used, xs, wg, wu, wd, tg=tg)
    return combine(dest, x, mod, rw, ys, seq=seq, tm=tm)


def _final_norm_kernel(x_ref, w_ref, o_ref):
    x = x_ref[...]
    o_ref[...] = x * lax.rsqrt(jnp.mean(x * x, axis=-1, keepdims=True) + EPS) * w_ref[...]


def final_norm(x, w, *, tm):
    t = x.shape[0]
    return pl.pallas_call(
        _final_norm_kernel,
        grid=(t // tm,),
        in_specs=[pl.BlockSpec((tm, D), lambda i: (i, 0)), pl.BlockSpec((1, D), lambda i: (0, 0))],
        out_specs=pl.BlockSpec((tm, D), lambda i: (i, 0)),
        out_shape=jax.ShapeDtypeStruct((t, D), F32),
        compiler_params=_cparams("arbitrary"),
        name="final_norm",
    )(x, w)


def _pad_cols(w, n):
    return jnp.pad(w, ((0, 0), (0, n - w.shape[1])))


def kernel(x, c, ada_w, ada_b, norm1_w, norm2_w, final_norm_w, m_w_in, m_i_bias, m_f_bias, m_conv_w, m_conv_b, m_norm_w, m_w_out, h_w_in, h_lb_logits, h_norm_w, h_w_out, ffn_w_gate, ffn_w_up, ffn_w_down, moe_router, moe_w_gate, moe_w_up, moe_w_down):
    bsz, seq, _ = x.shape
    t = bsz * seq
    tm = min(512, seq)
    xt = x.reshape(t, D)

    mod_all = ada_modulation(c, ada_w, ada_b)
    mod_all = jnp.pad(mod_all.reshape(DEPTH, bsz, 6, D), ((0, 0), (0, 0), (0, 2), (0, 0)))

    gam = jax.nn.softmax(h_lb_logits.astype(F32), axis=0)
    lower_bounds = jnp.cumsum(gam, axis=0) - gam[:1]

    n_main = 2 * N_HEADS * M_DQK + 2 * N_HEADS * HEAD_DV
    zero_extra = jnp.zeros((D, LANES), F32)
    for layer in range(DEPTH):
        j = layer // 2
        mod = mod_all[layer]
        n1 = norm1_w[layer].reshape(1, D)
        n2 = norm2_w[layer].reshape(1, D)
        if layer % 2 == 0:
            w_in = m_w_in[j]
            proj, gates = in_proj(xt, mod, n1, w_in[:, :n_main].astype(BF16), _pad_cols(w_in[:, n_main:], LANES),
                                  seq=seq, tm=tm)
            gate_bias = _pad_cols(jnp.concatenate([m_i_bias[j], m_f_bias[j]]).reshape(1, -1), LANES)
            y = mlstm_mix(proj, gates, m_conv_w[j], m_conv_b[j].reshape(1, -1), gate_bias,
                          m_norm_w[j].reshape(1, -1), bsz=bsz, seq=seq)
            xt = out_proj(y, m_w_out[j].astype(BF16), xt, mod, seq=seq, tm=tm, gate_row=2)
            xt = ffn(xt, mod, n2, ffn_w_gate[j].astype(BF16), ffn_w_up[j].astype(BF16),
                     ffn_w_down[j].astype(BF16), seq=seq, tm=tm, tf=1408)
        else:
            proj, _ = in_proj(xt, mod, n1, h_w_in[j].astype(BF16), zero_extra, seq=seq, tm=tm)
            y = hgrn_mix(proj, lower_bounds[j].reshape(1, D), h_norm_w[j].reshape(1, D), bsz=bsz, seq=seq)
            xt = out_proj(y, h_w_out[j].astype(BF16), xt, mod, seq=seq, tm=tm, gate_row=2)
            xt = moe(xt, mod, n2, _pad_cols(moe_router[j], LANES), moe_w_gate[j].astype(BF16),
                     moe_w_up[j].astype(BF16), moe_w_down[j].astype(BF16), seq=seq, tm=tm, tg=tm)
    out = final_norm(xt, final_norm_w.reshape(1, D), tm=tm)
    return out.reshape(bsz, seq, D)
```

```python
import functools

import jax
import jax.numpy as jnp
from jax import lax
from jax.experimental import pallas as pl
from jax.experimental.pallas import tpu as pltpu

F32 = jnp.float32
BF16 = jnp.bfloat16

DEPTH = 4
D = 1024
N_HEADS = 8
HEAD_DV = 128
M_DQK = 64
CONV_K = 4
IGATE_CAP = 15.0
N_EXPERTS = 8
EPS = 1e-6
NEG = -1e30
F_FLOOR = 1e-30
LANES = 128
CHUNK = 128
SUB = 16
VMEM_LIMIT = 56 * 1024 * 1024


def _cparams(*sem):
    return pltpu.CompilerParams(dimension_semantics=sem, vmem_limit_bytes=VMEM_LIMIT)


def _sigmoid(x):
    return 1.0 / (1.0 + jnp.exp(-x))


def _silu(x):
    return x * _sigmoid(x)


def _dot(a, b):
    return jnp.dot(a, b, preferred_element_type=F32)


def _dot_nt(a, b):
    return lax.dot_general(a, b, (((1,), (1,)), ((), ())), preferred_element_type=F32)


def _dot_f32(a, b):
    return jnp.dot(a, b, preferred_element_type=F32, precision=lax.Precision.HIGHEST)


def _cumsum_rows(x):
    n = x.shape[0]
    tri = (lax.broadcasted_iota(jnp.int32, (n, n), 1) <= lax.broadcasted_iota(jnp.int32, (n, n), 0)).astype(BF16)
    x1 = x.astype(BF16)
    r1 = x - x1.astype(F32)
    x2 = r1.astype(BF16)
    x3 = (r1 - x2.astype(F32)).astype(BF16)
    return _dot(tri, x1) + _dot(tri, x2) + _dot(tri, x3)


def _norm_mod(x, norm_w, scale, shift):
    y = x * lax.rsqrt(jnp.mean(x * x, axis=-1, keepdims=True) + EPS)
    return (y * norm_w) * (1.0 + scale) + shift


def _ada_kernel(c_ref, w_ref, b_ref, o_ref):
    c = c_ref[...]
    o_ref[0] = _dot_f32(_silu(c), w_ref[0]) + b_ref[0]


def ada_modulation(c, ada_w, ada_b):
    bsz = c.shape[0]
    tn = 1536
    return pl.pallas_call(
        _ada_kernel,
        grid=(DEPTH, 6 * D // tn),
        in_specs=[pl.BlockSpec((bsz, D), lambda l, j: (0, 0)),
                  pl.BlockSpec((1, D, tn), lambda l, j: (l, 0, j)),
                  pl.BlockSpec((1, 1, tn), lambda l, j: (l, 0, j))],
        out_specs=pl.BlockSpec((1, bsz, tn), lambda l, j: (l, 0, j)),
        out_shape=jax.ShapeDtypeStruct((DEPTH, bsz, 6 * D), F32),
        compiler_params=_cparams("arbitrary", "arbitrary"),
        name="ada_modulation",
    )(c, ada_w, ada_b.reshape(DEPTH, 1, 6 * D))


def _in_proj_kernel(x_ref, mod_ref, nw_ref, w_ref, *rest, has_extra):
    if has_extra:
        wx_ref, o_ref, ox_ref, hn_ref = rest
    else:
        o_ref, hn_ref = rest

    @pl.when(pl.program_id(1) == 0)
    def _():
        hn = _norm_mod(x_ref[...], nw_ref[...], mod_ref[0, 1:2, :], mod_ref[0, 0:1, :])
        hn_ref[...] = hn.astype(BF16)
        if has_extra:
            ox_ref[...] = _dot(hn_ref[...], wx_ref[...])

    o_ref[...] = _dot(hn_ref[...], w_ref[...])


def in_proj(x, mod, norm_w, w, w_extra, *, seq, tm, tn):
    t = x.shape[0]
    n = w.shape[1]
    per_b = seq // tm
    has_extra = w_extra is not None
    in_specs = [pl.BlockSpec((tm, D), lambda i, j: (i, 0)),
                pl.BlockSpec((1, 8, D), lambda i, j: (i // per_b, 0, 0)),
                pl.BlockSpec((1, D), lambda i, j: (0, 0)),
                pl.BlockSpec((D, tn), lambda i, j: (0, j))]
    out_specs = [pl.BlockSpec((tm, tn), lambda i, j: (i, j))]
    out_shape = [jax.ShapeDtypeStruct((t, n), F32)]
    args = [x, mod, norm_w, w]
    if has_extra:
        in_specs.append(pl.BlockSpec((D, LANES), lambda i, j: (0, 0)))
        out_specs.append(pl.BlockSpec((tm, LANES), lambda i, j: (i, 0)))
        out_shape.append(jax.ShapeDtypeStruct((t, LANES), F32))
        args.append(w_extra)
    return pl.pallas_call(
        functools.partial(_in_proj_kernel, has_extra=has_extra),
        grid=(t // tm, n // tn),
        in_specs=in_specs,
        out_specs=out_specs,
        out_shape=out_shape,
        scratch_shapes=[pltpu.VMEM((tm, D), BF16)],
        compiler_params=_cparams("arbitrary", "arbitrary"),
        name="in_proj",
    )(*args)


def _mlstm_kernel(qk_ref, v_ref, o_ref, g_ref, cw_ref, cb_ref, gb_ref, nw_ref, out_ref,
                  tail_ref, c_ref, m_ref):
    L = CHUNK

    @pl.when(pl.program_id(1) == 0)
    def _():
        tail_ref[...] = jnp.zeros_like(tail_ref)
        c_ref[...] = jnp.zeros_like(c_ref)
        m_ref[...] = jnp.zeros_like(m_ref)

    raw = qk_ref[...]
    xx = jnp.concatenate([tail_ref[...], raw], axis=0)
    tail_ref[...] = raw[L - 8:, :]
    y = cb_ref[...] + cw_ref[CONV_K - 1:CONV_K, :] * raw
    for j in range(CONV_K - 1):
        y = y + cw_ref[j:j + 1, :] * pltpu.roll(xx, CONV_K - 1 - j, axis=0)[8:, :]
    qk = _silu(y)
    half = N_HEADS * M_DQK
    q_all = qk[:, :half]
    k_all = qk[:, half:] * (M_DQK ** -0.5)

    g = g_ref[...] + gb_ref[...]
    lane_g = lax.broadcasted_iota(jnp.int32, g.shape, 1)
    li_all = IGATE_CAP * jnp.tanh(g * (1.0 / IGATE_CAP))
    lf_all = jnp.minimum(g, 0.0) - jnp.log1p(jnp.exp(-jnp.abs(g)))
    gates = jnp.where(lane_g < N_HEADS, li_all, lf_all)
    row = lax.broadcasted_iota(jnp.int32, (L, L), 0)
    col = lax.broadcasted_iota(jnp.int32, (L, L), 1)
    tril = (col <= row)
    csum = _cumsum_rows(gates)

    li_r = gates.T[0:N_HEADS, :]
    b_r = csum.T[N_HEADS:2 * N_HEADS, :]
    a_r = li_r - b_r
    lane_r = lax.broadcasted_iota(jnp.int32, a_r.shape, 1)
    amax_r = a_r
    for sh in (1, 2, 4, 8, 16, 32, 64):
        amax_r = jnp.where(lane_r >= sh, jnp.maximum(amax_r, pltpu.roll(amax_r, sh, axis=1)), amax_r)
    m_prev = m_ref[...]
    mx_r = jnp.maximum(m_prev, amax_r)
    w_inter_r = jnp.exp(m_prev - mx_r)
    floor_r = jnp.exp(-(b_r + mx_r))
    mx_last = mx_r[:, L - 1:L]
    wk_r = jnp.exp(a_r - mx_last)
    m_ref[...] = jnp.broadcast_to(b_r[:, L - 1:L] + mx_last, m_prev.shape)
    decay_r = jnp.broadcast_to(w_inter_r[:, L - 1:L], m_prev.shape)
    cols = jnp.concatenate([mx_r, w_inter_r, floor_r, wk_r,
                            jnp.zeros((LANES - 4 * N_HEADS, L), F32)], axis=0).T

    lane = lax.broadcasted_iota(jnp.int32, (L, LANES), 1)
    ones_b = jnp.ones((L, LANES), BF16)
    qb_l, kb_l, ve_l, col_l = [], [], [], []
    for h in range(N_HEADS):
        p, hf = h // 2, h % 2
        in_head = (lane >= hf * M_DQK) & (lane < (hf + 1) * M_DQK)
        qb_l.append(jnp.where(in_head, q_all[:, p * LANES:(p + 1) * LANES], 0.0).astype(BF16))
        kb_l.append(k_all[:, p * LANES:(p + 1) * LANES])
        ve_l.append(jnp.concatenate([v_ref[:, h * HEAD_DV:(h + 1) * HEAD_DV].astype(BF16), ones_b], axis=-1))
        col_l.append([cols[:, j * N_HEADS + h:j * N_HEADS + h + 1] for j in range(4)])
    qb = jnp.stack(qb_l)
    k_p = jnp.stack(kb_l)
    v_ext = jnp.stack(ve_l)
    mx_col, w_inter, floor, wk = (jnp.stack([c[j] for c in col_l]) for j in range(4))
    c_st = c_ref[...]

    pmat = jnp.exp(jnp.where(tril[None], a_r[:, None, :] - mx_col, NEG))
    s = jnp.einsum('hqd,hkd->hqk', qb, k_p.astype(BF16), preferred_element_type=F32) * pmat
    nd = (w_inter * jnp.einsum('hqd,hde->hqe', qb, c_st.astype(BF16), preferred_element_type=F32)
          + jnp.einsum('hqk,hke->hqe', s.astype(BF16), v_ext, preferred_element_type=F32))
    hh = nd[:, :, :HEAD_DV] / jnp.maximum(jnp.abs(nd[:, :, HEAD_DV:]), floor)

    kw = (k_p * wk).astype(BF16)
    for h in range(N_HEADS):
        c_ref[h] = decay_r[h:h + 1, :1] * c_st[h] + lax.dot_general(
            kw[h], v_ext[h], (((0,), (0,)), ((), ())), preferred_element_type=F32)

    ms = _dot((hh * hh).astype(BF16).reshape(N_HEADS * L, HEAD_DV), ones_b).reshape(hh.shape) * (1.0 / HEAD_DV)
    hn = hh * lax.rsqrt(ms + EPS)
    merged = jnp.concatenate([hn[h] for h in range(N_HEADS)], axis=-1) * nw_ref[...]
    out_ref[...] = (merged * _sigmoid(o_ref[...])).astype(out_ref.dtype)


def mlstm_mix(proj, gates, conv_w, conv_b, gate_bias, norm_w, *, bsz, seq):
    t = bsz * seq
    nblk = seq // CHUNK
    dv = N_HEADS * HEAD_DV
    row = lambda b, s: b * nblk + s
    return pl.pallas_call(
        _mlstm_kernel,
        grid=(bsz, nblk),
        in_specs=[pl.BlockSpec((CHUNK, D), lambda b, s: (row(b, s), 0)),
                  pl.BlockSpec((CHUNK, dv), lambda b, s: (row(b, s), 1)),
                  pl.BlockSpec((CHUNK, dv), lambda b, s: (row(b, s), 2)),
                  pl.BlockSpec((CHUNK, LANES), lambda b, s: (row(b, s), 0)),
                  pl.BlockSpec((CONV_K, D), lambda b, s: (0, 0)),
                  pl.BlockSpec((1, D), lambda b, s: (0, 0)),
                  pl.BlockSpec((1, LANES), lambda b, s: (0, 0)),
                  pl.BlockSpec((1, dv), lambda b, s: (0, 0))],
        out_specs=pl.BlockSpec((CHUNK, dv), lambda b, s: (row(b, s), 0)),
        out_shape=jax.ShapeDtypeStruct((t, dv), BF16),
        scratch_shapes=[pltpu.VMEM((8, D), F32),
                        pltpu.VMEM((N_HEADS, LANES, 2 * HEAD_DV), F32),
                        pltpu.VMEM((N_HEADS, LANES), F32)],
        compiler_params=_cparams("arbitrary", "arbitrary"),
        name="mlstm_mix",
    )(proj, proj, proj, gates, conv_w, conv_b, gate_bias, norm_w)


def _hgrn_kernel(q_ref, f_ref, v_ref, g_ref, lb_ref, nw_ref, out_ref, st_ref):
    L = CHUNK

    @pl.when(pl.program_id(1) == 0)
    def _():
        st_ref[...] = jnp.zeros_like(st_ref)

    lb = lb_ref[...]
    f = f_ref[...]
    e = jnp.exp(-jnp.abs(f))
    r = 1.0 / (1.0 + e)
    er = e * r
    sig = jnp.where(f >= 0.0, r, er)
    nsig = jnp.where(f >= 0.0, er, r)
    g_all = jnp.log(jnp.maximum(lb + (1.0 - lb) * sig, F_FLOOR))
    k_all = (1.0 - lb) * nsig
    q_all = _silu(q_ref[...])
    row = lax.broadcasted_iota(jnp.int32, (L, L), 0)
    col = lax.broadcasted_iota(jnp.int32, (L, L), 1)
    bc_all = _cumsum_rows(g_all)
    xor = row ^ col
    eye = row == col
    ones_b = jnp.ones((LANES, LANES), BF16)
    sub8 = lax.broadcasted_iota(jnp.int32, (1, 8, LANES), 1)
    H = N_HEADS

    def heads(x):
        return jnp.stack([x[:, h * HEAD_DV:(h + 1) * HEAD_DV] for h in range(H)])

    def bmm_nt(x, y):
        return jnp.einsum('hqd,hkd->hqk', x.astype(BF16), y.astype(BF16), preferred_element_type=F32)

    def rot8(x3, d):
        return pltpu.roll(x3, d % 8, axis=1)

    q, k, g, bc = heads(q_all), heads(k_all), heads(g_all), heads(bc_all)
    vb = heads(v_ref[...]).astype(BF16)
    st = st_ref[...]
    b_last = bc[:, L - 1:L, :]

    o = bmm_nt(q * jnp.exp(bc), st)

    diag = _dot((q * k).astype(BF16).reshape(H * L, HEAD_DV), ones_b).reshape(H, L, L)
    a = jnp.where(eye[None], diag, 0.0)
    for b in (64, 32, 16, 8):
        qs, ks = [], []
        for blk in range(L // (2 * b)):
            lo, mid, hi = 2 * b * blk, 2 * b * blk + b, 2 * b * (blk + 1)
            ref = bc[:, mid - 1:mid, :]
            ks.append(k[:, lo:mid, :] * jnp.exp(ref - bc[:, lo:mid, :]))
            ks.append(jnp.zeros((H, b, LANES), F32))
            qs.append(jnp.zeros((H, b, LANES), F32))
            qs.append(q[:, mid:hi, :] * jnp.exp(bc[:, mid:hi, :] - ref))
        a_l = bmm_nt(jnp.concatenate(qs, axis=1), jnp.concatenate(ks, axis=1))
        a = jnp.where(((xor >= b) & (xor < 2 * b))[None], a_l, a)
    n8 = H * L // 8
    e8 = jnp.exp(g).reshape(n8, 8, LANES)
    q8 = q.reshape(n8, 8, LANES)
    k8 = k.reshape(n8, 8, LANES)
    ew = [e8]
    for j in range(1, 4):
        ew.append(ew[-1] * rot8(e8, j))
    eu = [rot8(e8, -1)]
    for j in range(1, 3):
        eu.append(eu[-1] * rot8(e8, -1 - j))
    for b in (4, 2, 1):
        pos = sub8 % (2 * b)
        fq = jnp.where(pos == b, ew[0], 0.0)
        fk = jnp.where(pos == b - 1, 1.0, 0.0)
        for j in range(b - 1):
            fq = jnp.where(pos == b + 1 + j, ew[1 + j], fq)
            fk = jnp.where(pos == j, eu[b - 2 - j], fk)
        a_l = bmm_nt((q8 * fq).reshape(H, L, LANES), (k8 * fk).reshape(H, L, LANES))
        a = jnp.where(((xor >= b) & (xor < 2 * b))[None], a_l, a)
    o = o + jnp.einsum('hqk,hkv->hqv', a.astype(BF16), vb, preferred_element_type=F32)

    k_dec = (k * jnp.exp(b_last - bc)).astype(BF16)
    st_decay = jnp.exp(b_last)
    for h in range(H):
        st_ref[h] = st[h] * st_decay[h] + lax.dot_general(
            vb[h], k_dec[h], (((0,), (0,)), ((), ())), preferred_element_type=F32)

    ms = _dot((o * o).astype(BF16).reshape(H * L, HEAD_DV), ones_b).reshape(o.shape) * (1.0 / HEAD_DV)
    on = o * lax.rsqrt(ms + EPS)
    merged = jnp.concatenate([on[h] for h in range(H)], axis=-1) * nw_ref[...]
    out_ref[...] = (merged * _silu(g_ref[...])).astype(out_ref.dtype)


def hgrn_mix(proj, lb, norm_w, *, bsz, seq):
    t = bsz * seq
    nblk = seq // CHUNK
    row = lambda b, s: b * nblk + s
    return pl.pallas_call(
        _hgrn_kernel,
        grid=(bsz, nblk),
        in_specs=[pl.BlockSpec((CHUNK, D), lambda b, s: (row(b, s), 0)),
                  pl.BlockSpec((CHUNK, D), lambda b, s: (row(b, s), 1)),
                  pl.BlockSpec((CHUNK, D), lambda b, s: (row(b, s), 2)),
                  pl.BlockSpec((CHUNK, D), lambda b, s: (row(b, s), 3)),
                  pl.BlockSpec((1, D), lambda b, s: (0, 0)),
                  pl.BlockSpec((1, D), lambda b, s: (0, 0))],
        out_specs=pl.BlockSpec((CHUNK, D), lambda b, s: (row(b, s), 0)),
        out_shape=jax.ShapeDtypeStruct((t, D), BF16),
        scratch_shapes=[pltpu.VMEM((N_HEADS, HEAD_DV, HEAD_DV), F32)],
        compiler_params=_cparams("arbitrary", "arbitrary"),
        name="hgrn_mix",
    )(proj, proj, proj, proj, lb, norm_w)


def _mixer_residual(x_ref, y_ref, wo_ref, mod_ref):
    return x_ref[...] + mod_ref[0, 2:3, :] * _dot(y_ref[...], wo_ref[...])


def _ffn_kernel(x_ref, y_ref, wo_ref, mod_ref, nw_ref, wg_ref, wu_ref, wd_ref, o_ref, x1_ref, hn_ref, acc_ref):
    f = pl.program_id(1)

    @pl.when(f == 0)
    def _():
        x1 = _mixer_residual(x_ref, y_ref, wo_ref, mod_ref)
        x1_ref[...] = x1
        hn_ref[...] = _norm_mod(x1, nw_ref[...], mod_ref[0, 4:5, :], mod_ref[0, 3:4, :]).astype(BF16)
        acc_ref[...] = jnp.zeros_like(acc_ref)

    hb = hn_ref[...]
    hid = _silu(_dot(hb, wg_ref[...])) * _dot(hb, wu_ref[...])
    acc_ref[...] += _dot(hid.astype(BF16), wd_ref[...])

    @pl.when(f == pl.num_programs(1) - 1)
    def _():
        o_ref[...] = x1_ref[...] + mod_ref[0, 5:6, :] * acc_ref[...]


def ffn(x, y, w_out, mod, norm_w, wg, wu, wd, *, seq, tm, tf):
    t = x.shape[0]
    dff = wg.shape[1]
    per_b = seq // tm
    return pl.pallas_call(
        _ffn_kernel,
        grid=(t // tm, dff // tf),
        in_specs=[pl.BlockSpec((tm, D), lambda i, f: (i, 0)),
                  pl.BlockSpec((tm, D), lambda i, f: (i, 0)),
                  pl.BlockSpec((D, D), lambda i, f: (0, 0)),
                  pl.BlockSpec((1, 8, D), lambda i, f: (i // per_b, 0, 0)),
                  pl.BlockSpec((1, D), lambda i, f: (0, 0)),
                  pl.BlockSpec((D, tf), lambda i, f: (0, f)),
                  pl.BlockSpec((D, tf), lambda i, f: (0, f)),
                  pl.BlockSpec((tf, D), lambda i, f: (f, 0))],
        out_specs=pl.BlockSpec((tm, D), lambda i, f: (i, 0)),
        out_shape=jax.ShapeDtypeStruct((t, D), F32),
        scratch_shapes=[pltpu.VMEM((tm, D), F32), pltpu.VMEM((tm, D), BF16), pltpu.VMEM((tm, D), F32)],
        compiler_params=_cparams("arbitrary", "arbitrary"),
        name="ffn",
    )(x, y, w_out, mod, norm_w, wg, wu, wd)


def _router_kernel(x_ref, y_ref, wo_ref, mod_ref, nw_ref, wr_ref, x1_ref, hn_ref, rw_ref, ri_ref):
    x1 = _mixer_residual(x_ref, y_ref, wo_ref, mod_ref)
    x1_ref[...] = x1
    hn = _norm_mod(x1, nw_ref[...], mod_ref[0, 4:5, :], mod_ref[0, 3:4, :])
    _to_tile_rows(hn_ref, hn)
    h_hi = hn.astype(BF16)
    h_lo = (hn - h_hi.astype(F32)).astype(BF16)
    hw = _dot(h_hi, wr_ref[...])
    logits = hw[:, :LANES] + hw[:, LANES:] + _dot(h_lo, wr_ref[:, :LANES])
    lane = lax.broadcasted_iota(jnp.int32, logits.shape, 1)
    lg = jnp.where(lane < N_EXPERTS, logits, NEG)
    m1 = jnp.max(lg, axis=-1, keepdims=True)
    i1 = jnp.min(jnp.where(lg == m1, lane, LANES), axis=-1, keepdims=True)
    lg2 = jnp.where(lane == i1, NEG, lg)
    m2 = jnp.max(lg2, axis=-1, keepdims=True)
    i2 = jnp.min(jnp.where(lg2 == m2, lane, LANES), axis=-1, keepdims=True)
    e2 = jnp.exp(m2 - m1)
    w1 = 1.0 / (1.0 + e2)
    rw_ref[...] = jnp.where(lane == 0, w1, jnp.where(lane == 1, e2 * w1, 0.0))
    ri_ref[...] = jnp.where(lane == 0, i1, jnp.where(lane == 1, i2, 0))


def router(x, y, w_out, mod, norm_w, wr, *, seq, tm):
    t = x.shape[0]
    per_b = seq // tm
    return pl.pallas_call(
        _router_kernel,
        grid=(t // tm,),
        in_specs=[pl.BlockSpec((tm, D), lambda i: (i, 0)),
                  pl.BlockSpec((tm, D), lambda i: (i, 0)),
                  pl.BlockSpec((D, D), lambda i: (0, 0)),
                  pl.BlockSpec((1, 8, D), lambda i: (i // per_b, 0, 0)),
                  pl.BlockSpec((1, D), lambda i: (0, 0)),
                  pl.BlockSpec((D, 2 * LANES), lambda i: (0, 0))],
        out_specs=[pl.BlockSpec((tm, D), lambda i: (i, 0)),
                   pl.BlockSpec((tm * ROW_TILES, LANES), lambda i: (i, 0)),
                   pl.BlockSpec((tm, LANES), lambda i: (i, 0)),
                   pl.BlockSpec((tm, LANES), lambda i: (i, 0))],
        out_shape=[jax.ShapeDtypeStruct((t, D), F32),
                   jax.ShapeDtypeStruct((t * ROW_TILES, LANES), F32),
                   jax.ShapeDtypeStruct((t, LANES), F32),
                   jax.ShapeDtypeStruct((t, LANES), jnp.int32)],
        compiler_params=_cparams("arbitrary"),
        name="router",
    )(x, y, w_out, mod, norm_w, wr)


SUBLANES = 8
ROW_TILES = D // LANES


def _to_tile_rows(ref, val):
    rows = val.shape[0]
    for s in range(ROW_TILES):
        ref[pl.ds(s, rows, stride=ROW_TILES), :] = val[:, s * LANES:(s + 1) * LANES]


def _from_tile_rows(ref, rows):
    return jnp.concatenate([ref[pl.ds(s, rows, stride=ROW_TILES), :] for s in range(ROW_TILES)], axis=-1)


def _tile_row(ref, row):
    return ref.at[pl.ds(pl.multiple_of(row * ROW_TILES, ROW_TILES), ROW_TILES)]


def _for_each_row_pair(tm, fn):
    def body(row, carry):
        for k in range(2):
            fn(row, 2 * row + k, k)
        return carry

    lax.fori_loop(0, tm, body, 0, unroll=8)


def _dispatch_kernel(ends_ref, dest_ref, hn_ref, xs_ref, zero_ref, sem):
    tm = hn_ref.shape[0] // ROW_TILES
    tgr = zero_ref.shape[0]

    @pl.when(pl.program_id(0) == 0)
    def _():
        zero_ref[...] = jnp.zeros_like(zero_ref)

        def zero_tile(start):
            cp = pltpu.make_async_copy(zero_ref, xs_ref.at[pl.ds(pl.multiple_of(start, tgr), tgr)], sem)
            cp.start()
            cp.wait()

        n_tiles = xs_ref.shape[0] // tgr
        for e in range(N_EXPERTS):
            @pl.when(ends_ref[N_EXPERTS + e] > 0)
            def _():
                zero_tile(ends_ref[e] * ROW_TILES - tgr)

            @pl.when(n_tiles - 1 - e >= ends_ref[2 * N_EXPERTS])
            def _():
                zero_tile((n_tiles - 1 - e) * tgr)

    def scatter(row, slot, k):
        pltpu.make_async_copy(_tile_row(hn_ref, row), _tile_row(xs_ref, dest_ref[0, 0, slot]),
                              sem).start(priority=k)

    _for_each_row_pair(tm, scatter)
    for k in range(2):
        pltpu.make_async_copy(hn_ref, xs_ref.at[pl.ds(0, tm * ROW_TILES)], sem).wait()


def dispatch(ends_padded, dest, hn, *, n_rows, tm, tg):
    t = hn.shape[0] // ROW_TILES
    grid_spec = pltpu.PrefetchScalarGridSpec(
        num_scalar_prefetch=1,
        grid=(t // tm,),
        in_specs=[pl.BlockSpec((1, 1, 2 * tm), lambda i, ends: (i, 0, 0), memory_space=pltpu.SMEM),
                  pl.BlockSpec((tm * ROW_TILES, LANES), lambda i, ends: (i, 0))],
        out_specs=pl.BlockSpec(memory_space=pl.ANY),
        scratch_shapes=[pltpu.VMEM((tg * ROW_TILES, LANES), F32), pltpu.SemaphoreType.DMA],
    )
    return pl.pallas_call(
        _dispatch_kernel,
        grid_spec=grid_spec,
        out_shape=jax.ShapeDtypeStruct((n_rows * ROW_TILES, LANES), F32),
        compiler_params=_cparams("arbitrary"),
        name="moe_dispatch",
    )(ends_padded, dest, hn)


def _expert_ffn_kernel(te_ref, nu_ref, x_ref, wg_ref, wu_ref, wd_ref, o_ref):
    del te_ref
    i = pl.program_id(0)

    @pl.when(i < nu_ref[0])
    def _():
        hb = _from_tile_rows(x_ref, x_ref.shape[0] // ROW_TILES).astype(BF16)
        hid = _silu(_dot(hb, wg_ref[0])) * _dot(hb, wu_ref[0])
        _to_tile_rows(o_ref, _dot(hid.astype(BF16), wd_ref[0]))

    @pl.when(i >= nu_ref[0])
    def _():
        o_ref[...] = jnp.zeros_like(o_ref)


def expert_ffn(tile_expert, n_used, xs, wg, wu, wd, *, tg):
    r = xs.shape[0] // ROW_TILES
    dfe = wg.shape[2]
    grid_spec = pltpu.PrefetchScalarGridSpec(
        num_scalar_prefetch=2,
        grid=(r // tg,),
        in_specs=[pl.BlockSpec((tg * ROW_TILES, LANES), lambda i, te, nu: (jnp.minimum(i, nu[0] - 1), 0)),
                  pl.BlockSpec((1, D, dfe), lambda i, te, nu: (te[i], 0, 0)),
                  pl.BlockSpec((1, D, dfe), lambda i, te, nu: (te[i], 0, 0)),
                  pl.BlockSpec((1, dfe, D), lambda i, te, nu: (te[i], 0, 0))],
        out_specs=pl.BlockSpec((tg * ROW_TILES, LANES), lambda i, te, nu: (i, 0)),
    )
    return pl.pallas_call(
        _expert_ffn_kernel,
        grid_spec=grid_spec,
        out_shape=jax.ShapeDtypeStruct((r * ROW_TILES, LANES), F32),
        compiler_params=_cparams("arbitrary"),
        name="moe_expert_ffn",
    )(tile_expert, n_used, xs, wg, wu, wd)


def _combine_kernel(dest_ref, x_ref, mod_ref, rw_ref, fw_ref, ys_ref, o_ref, buf_ref, sem, *, final):
    tm = x_ref.shape[0]

    def gather(row, slot, k):
        pltpu.make_async_copy(_tile_row(ys_ref, dest_ref[0, 0, slot]), _tile_row(buf_ref.at[k], row),
                              sem).start(priority=k)

    _for_each_row_pair(tm, gather)
    for k in range(2):
        pltpu.make_async_copy(ys_ref.at[pl.ds(0, tm * ROW_TILES)], buf_ref.at[k], sem).wait()
    rw = rw_ref[...]
    y = rw[:, 0:1] * _from_tile_rows(buf_ref.at[0], tm) + rw[:, 1:2] * _from_tile_rows(buf_ref.at[1], tm)
    out = x_ref[...] + mod_ref[0, 5:6, :] * y
    if final:
        out = out * lax.rsqrt(jnp.mean(out * out, axis=-1, keepdims=True) + EPS) * fw_ref[...]
    o_ref[...] = out


def combine(dest, x, mod, rw, final_w, ys, *, seq, tm, final):
    t = x.shape[0]
    per_b = seq // tm
    return pl.pallas_call(
        functools.partial(_combine_kernel, final=final),
        grid=(t // tm,),
        in_specs=[pl.BlockSpec((1, 1, 2 * tm), lambda i: (i, 0, 0), memory_space=pltpu.SMEM),
                  pl.BlockSpec((tm, D), lambda i: (i, 0)),
                  pl.BlockSpec((1, 8, D), lambda i: (i // per_b, 0, 0)),
                  pl.BlockSpec((tm, LANES), lambda i: (i, 0)),
                  pl.BlockSpec((1, D), lambda i: (0, 0)),
                  pl.BlockSpec(memory_space=pl.ANY)],
        out_specs=pl.BlockSpec((tm, D), lambda i: (i, 0)),
        out_shape=jax.ShapeDtypeStruct((t, D), F32),
        scratch_shapes=[pltpu.VMEM((2, tm * ROW_TILES, LANES), F32), pltpu.SemaphoreType.DMA],
        compiler_params=_cparams("arbitrary"),
        name="moe_combine",
    )(dest, x, mod, rw, final_w, ys)


def _routing_plan(route_i, *, tm, tg):
    t = route_i.shape[0]
    e_flat = route_i[:, :2].reshape(-1)
    onehot = (e_flat[:, None] == jnp.arange(N_EXPERTS, dtype=jnp.int32)[None, :]).astype(jnp.int32)
    csum = jnp.cumsum(onehot, axis=0)
    counts = csum[-1]
    padded = ((counts + tg - 1) // tg) * tg
    ends = jnp.cumsum(padded)
    starts = ends - padded
    dest = jnp.sum(onehot * (starts[None, :] + csum - onehot), axis=1)
    n_tiles = (2 * t) // tg + N_EXPERTS
    tile_start = jnp.arange(n_tiles, dtype=jnp.int32) * tg
    tile_expert = jnp.sum((tile_start[:, None] >= ends[None, :]).astype(jnp.int32), axis=1)
    n_used = (ends[-1] // tg).astype(jnp.int32)
    last_expert = jnp.take(tile_expert, jnp.maximum(n_used - 1, 0))
    tile_expert = jnp.where(tile_start < ends[-1], tile_expert, last_expert).astype(jnp.int32)
    ends_padded = jnp.concatenate([ends, padded, n_used.reshape(1)]).astype(jnp.int32)
    return (dest.reshape(t // tm, 1, 2 * tm).astype(jnp.int32), tile_expert, n_used.reshape(1), ends_padded,
            n_tiles * tg)


def moe(x, y, w_out, mod, norm_w, wr, wg, wu, wd, final_w, *, seq, tm, tg, final):
    x1, hn, rw, ri = router(x, y, w_out, mod, norm_w, wr, seq=seq, tm=tm)
    dest, tile_expert, n_used, ends_padded, n_rows = _routing_plan(ri, tm=tm, tg=tg)
    xs = dispatch(ends_padded, dest, hn, n_rows=n_rows, tm=tm, tg=tg)
    ys = expert_ffn(tile_expert, n_used, xs, wg, wu, wd, tg=tg)
    return combine(dest, x1, mod, rw, final_w, ys, seq=seq, tm=tm, final=final)


def _pad_cols(w, n):
    return jnp.pad(w, ((0, 0), (0, n - w.shape[1])))


def kernel(x, c, ada_w, ada_b, norm1_w, norm2_w, final_norm_w, m_w_in, m_i_bias, m_f_bias, m_conv_w, m_conv_b, m_norm_w, m_w_out, h_w_in, h_lb_logits, h_norm_w, h_w_out, ffn_w_gate, ffn_w_up, ffn_w_down, moe_router, moe_w_gate, moe_w_up, moe_w_down):
    bsz, seq, _ = x.shape
    t = bsz * seq
    tm = min(512, seq)
    xt = x.reshape(t, D)

    mod_all = ada_modulation(c, ada_w, ada_b)
    mod_all = jnp.pad(mod_all.reshape(DEPTH, bsz, 6, D), ((0, 0), (0, 0), (0, 2), (0, 0)))

    gam = jax.nn.softmax(h_lb_logits.astype(F32), axis=0)
    lower_bounds = jnp.cumsum(gam, axis=0) - gam[:1]

    n_main = 2 * N_HEADS * M_DQK + 2 * N_HEADS * HEAD_DV
    tm_in = min(1024, seq)
    final_w = final_norm_w.reshape(1, D)
    for layer in range(DEPTH):
        j = layer // 2
        mod = mod_all[layer]
        n1 = norm1_w[layer].reshape(1, D)
        n2 = norm2_w[layer].reshape(1, D)
        if layer % 2 == 0:
            w_in = m_w_in[j]
            proj, gates = in_proj(xt, mod, n1, w_in[:, :n_main].astype(BF16),
                                  _pad_cols(w_in[:, n_main:], LANES).astype(BF16), seq=seq, tm=tm_in, tn=n_main // 2)
            gate_bias = _pad_cols(jnp.concatenate([m_i_bias[j], m_f_bias[j]]).reshape(1, -1), LANES)
            y = mlstm_mix(proj, gates, m_conv_w[j], m_conv_b[j].reshape(1, -1), gate_bias,
                          m_norm_w[j].reshape(1, -1), bsz=bsz, seq=seq)
            xt = ffn(xt, y, m_w_out[j].astype(BF16), mod, n2, ffn_w_gate[j].astype(BF16),
                     ffn_w_up[j].astype(BF16), ffn_w_down[j].astype(BF16), seq=seq, tm=tm, tf=1408)
        else:
            w_in = h_w_in[j].astype(BF16)
            (proj,) = in_proj(xt, mod, n1, w_in, None, seq=seq, tm=tm_in, tn=w_in.shape[1] // 2)
            y = hgrn_mix(proj, lower_bounds[j].reshape(1, D), h_norm_w[j].reshape(1, D), bsz=bsz, seq=seq)
            wr = _pad_cols(moe_router[j], LANES)
            wr_hi = wr.astype(BF16)
            wr_split = jnp.concatenate([wr_hi, (wr - wr_hi.astype(F32)).astype(BF16)], axis=1)
            xt = moe(xt, y, h_w_out[j].astype(BF16), mod, n2, wr_split,
                     moe_w_gate[j].astype(BF16), moe_w_up[j].astype(BF16), moe_w_down[j].astype(BF16), final_w,
                     seq=seq, tm=tm, tg=tm, final=layer == DEPTH - 1)
    return xt.reshape(bsz, seq, D)
```

```python
import functools

import jax
import jax.numpy as jnp
from jax import lax
from jax.experimental import pallas as pl
from jax.experimental.pallas import tpu as pltpu

F32 = jnp.float32
BF16 = jnp.bfloat16

DEPTH = 4
D = 1024
N_HEADS = 8
HEAD_DV = 128
M_DQK = 64
CONV_K = 4
IGATE_CAP = 15.0
N_EXPERTS = 8
EPS = 1e-6
NEG = -1e30
F_FLOOR = 1e-30
LANES = 128
CHUNK = 128
SUB = 16
VMEM_LIMIT = 56 * 1024 * 1024


def _cparams(*sem):
    return pltpu.CompilerParams(dimension_semantics=sem, vmem_limit_bytes=VMEM_LIMIT)


def _sigmoid(x):
    return 1.0 / (1.0 + jnp.exp(-x))


def _silu(x):
    return x * _sigmoid(x)


def _dot(a, b):
    return jnp.dot(a, b, preferred_element_type=F32)


def _dot_nt(a, b):
    return lax.dot_general(a, b, (((1,), (1,)), ((), ())), preferred_element_type=F32)


def _dot_f32(a, b):
    return jnp.dot(a, b, preferred_element_type=F32, precision=lax.Precision.HIGHEST)


def _cumsum_rows(x):
    n = x.shape[0]
    tri = (lax.broadcasted_iota(jnp.int32, (n, n), 1) <= lax.broadcasted_iota(jnp.int32, (n, n), 0)).astype(BF16)
    x1 = x.astype(BF16)
    r1 = x - x1.astype(F32)
    x2 = r1.astype(BF16)
    x3 = (r1 - x2.astype(F32)).astype(BF16)
    return _dot(tri, x1) + _dot(tri, x2) + _dot(tri, x3)


def _norm_mod(x, norm_w, scale, shift):
    y = x * lax.rsqrt(jnp.mean(x * x, axis=-1, keepdims=True) + EPS)
    return (y * norm_w) * (1.0 + scale) + shift


def _ada_kernel(c_ref, w_ref, b_ref, o_ref):
    c = c_ref[...]
    o_ref[0] = _dot_f32(_silu(c), w_ref[0]) + b_ref[0]


def ada_modulation(c, ada_w, ada_b):
    bsz = c.shape[0]
    tn = 1536
    return pl.pallas_call(
        _ada_kernel,
        grid=(DEPTH, 6 * D // tn),
        in_specs=[pl.BlockSpec((bsz, D), lambda l, j: (0, 0)),
                  pl.BlockSpec((1, D, tn), lambda l, j: (l, 0, j)),
                  pl.BlockSpec((1, 1, tn), lambda l, j: (l, 0, j))],
        out_specs=pl.BlockSpec((1, bsz, tn), lambda l, j: (l, 0, j)),
        out_shape=jax.ShapeDtypeStruct((DEPTH, bsz, 6 * D), F32),
        compiler_params=_cparams("arbitrary", "arbitrary"),
        name="ada_modulation",
    )(c, ada_w, ada_b.reshape(DEPTH, 1, 6 * D))


def _in_proj_kernel(x_ref, mod_ref, nw_ref, w_ref, *rest, has_extra):
    if has_extra:
        wx_ref, o_ref, ox_ref, hn_ref = rest
    else:
        o_ref, hn_ref = rest

    @pl.when(pl.program_id(1) == 0)
    def _():
        hn = _norm_mod(x_ref[...], nw_ref[...], mod_ref[0, 1:2, :], mod_ref[0, 0:1, :])
        hn_ref[...] = hn.astype(BF16)
        if has_extra:
            ox_ref[...] = _dot(hn_ref[...], wx_ref[...])

    o_ref[...] = _dot(hn_ref[...], w_ref[...])


def in_proj(x, mod, norm_w, w, w_extra, *, seq, tm, tn):
    t = x.shape[0]
    n = w.shape[1]
    per_b = seq // tm
    has_extra = w_extra is not None
    in_specs = [pl.BlockSpec((tm, D), lambda i, j: (i, 0)),
                pl.BlockSpec((1, 8, D), lambda i, j: (i // per_b, 0, 0)),
                pl.BlockSpec((1, D), lambda i, j: (0, 0)),
                pl.BlockSpec((D, tn), lambda i, j: (0, j))]
    out_specs = [pl.BlockSpec((tm, tn), lambda i, j: (i, j))]
    out_shape = [jax.ShapeDtypeStruct((t, n), F32)]
    args = [x, mod, norm_w, w]
    if has_extra:
        nx = w_extra.shape[1]
        in_specs.append(pl.BlockSpec((D, nx), lambda i, j: (0, 0)))
        out_specs.append(pl.BlockSpec((tm, nx), lambda i, j: (i, 0)))
        out_shape.append(jax.ShapeDtypeStruct((t, nx), F32))
        args.append(w_extra)
    return pl.pallas_call(
        functools.partial(_in_proj_kernel, has_extra=has_extra),
        grid=(t // tm, n // tn),
        in_specs=in_specs,
        out_specs=out_specs,
        out_shape=out_shape,
        scratch_shapes=[pltpu.VMEM((tm, D), BF16)],
        compiler_params=_cparams("arbitrary", "arbitrary"),
        name="in_proj",
    )(*args)


def _interleave(chunks):
    live = list(chunks)
    while live:
        for gen in list(live):
            if next(gen, StopIteration) is StopIteration:
                live.remove(gen)


def _mlstm_kernel(qk_ref, v_ref, o_ref, g_ref, cw_ref, cb_ref, gb_ref, nw_ref, out_ref,
                  tail_ref, c_ref, m_ref):
    @pl.when(pl.program_id(1) == 0)
    def _():
        tail_ref[...] = jnp.zeros_like(tail_ref)
        c_ref[...] = jnp.zeros_like(c_ref)
        m_ref[...] = jnp.zeros_like(m_ref)

    _interleave([_mlstm_chunk(qk_ref.at[i], v_ref.at[i], o_ref.at[i], g_ref.at[i], cw_ref, cb_ref, gb_ref, nw_ref,
                              out_ref.at[i], tail_ref.at[i], c_ref.at[i], m_ref.at[i])
                 for i in range(qk_ref.shape[0])])


def _mlstm_chunk(qk_ref, v_ref, o_ref, g_ref, cw_ref, cb_ref, gb_ref, nw_ref, out_ref, tail_ref, c_ref, m_ref):
    L = CHUNK

    raw = qk_ref[...]
    xx = jnp.concatenate([tail_ref[...], raw], axis=0)
    tail_ref[...] = raw[L - 8:, :]
    y = cb_ref[...] + cw_ref[CONV_K - 1:CONV_K, :] * raw
    for j in range(CONV_K - 1):
        y = y + cw_ref[j:j + 1, :] * pltpu.roll(xx, CONV_K - 1 - j, axis=0)[8:, :]
    qk = _silu(y)
    half = N_HEADS * M_DQK
    q_all = qk[:, :half]
    k_all = qk[:, half:] * (M_DQK ** -0.5)

    g = g_ref[...] + gb_ref[...]
    li = IGATE_CAP * jnp.tanh(g[:, :LANES] * (1.0 / IGATE_CAP))
    gf = g[:, LANES:]
    b = _cumsum_rows(jnp.minimum(gf, 0.0) - jnp.log1p(jnp.exp(-jnp.abs(gf))))
    tril = lax.broadcasted_iota(jnp.int32, (L, L), 1) <= lax.broadcasted_iota(jnp.int32, (L, L), 0)

    a = li - b
    amax8 = a.reshape(L // SUBLANES, SUBLANES, LANES)
    sub = lax.broadcasted_iota(jnp.int32, (1, SUBLANES, LANES), 1)
    for sh in (1, 2, 4):
        amax8 = jnp.where(sub >= sh, jnp.maximum(amax8, pltpu.roll(amax8, sh, axis=1)), amax8)
    groups = [amax8[0]]
    for v in range(1, L // SUBLANES):
        carry = jnp.broadcast_to(groups[-1][SUBLANES - 1:SUBLANES, :], (SUBLANES, LANES))
        groups.append(jnp.maximum(amax8[v], carry))
    amax = jnp.concatenate(groups, axis=0)
    m_prev = m_ref[...]
    mx = jnp.maximum(m_prev, amax)
    mx_last = mx[L - 1:L, :]
    m_ref[...] = b[L - 1:L, :] + mx_last
    decay = jnp.exp(m_prev - mx_last)
    a_r = a.T[0:N_HEADS, :]
    wk_r = jnp.exp(a - mx_last).T[0:N_HEADS, :]

    lane = lax.broadcasted_iota(jnp.int32, (L, LANES), 1)
    ones_b = jnp.ones((L, LANES), BF16)
    kt_pairs = [k_all[:, p * LANES:(p + 1) * LANES].T for p in range(N_HEADS // 2)]
    qb_l, kb_l, ve_l, kw_l = [], [], [], []
    for h in range(N_HEADS):
        p, hf = h // 2, h % 2
        in_head = (lane >= hf * M_DQK) & (lane < (hf + 1) * M_DQK)
        qb_l.append(jnp.where(in_head, q_all[:, p * LANES:(p + 1) * LANES], 0.0).astype(BF16))
        kb_l.append(k_all[:, p * LANES:(p + 1) * LANES].astype(BF16))
        ve_l.append(jnp.concatenate([v_ref[:, h * HEAD_DV:(h + 1) * HEAD_DV].astype(BF16), ones_b], axis=-1))
        kw_l.append((kt_pairs[p] * wk_r[h:h + 1, :]).astype(BF16))
    qb = jnp.stack(qb_l)
    kb = jnp.stack(kb_l)
    v_ext = jnp.stack(ve_l)
    c_st = c_ref[...]

    qk_raw = jnp.einsum('hqd,hkd->hqk', qb, kb, preferred_element_type=F32)
    inter = jnp.einsum('hqd,hde->hqe', qb, c_st.astype(BF16), preferred_element_type=F32)
    for h in range(N_HEADS):
        c_ref[h] = decay[:, h:h + 1] * c_st[h] + _dot(kw_l[h], v_ext[h])
    yield

    def lane_bcast(cols):
        return jnp.stack([jnp.broadcast_to(cols[:, h:h + 1], (L, LANES)) for h in range(N_HEADS)])

    mx_b = lane_bcast(mx)
    m_prev_b = jnp.stack([jnp.broadcast_to(m_prev[:, h:h + 1], (1, LANES)) for h in range(N_HEADS)])
    w_inter = jnp.exp(m_prev_b - mx_b)
    pmat = jnp.exp(jnp.where(tril[None], a_r[:, None, :] - mx_b, NEG))
    s = qk_raw * pmat
    nd = (jnp.concatenate([w_inter, w_inter], axis=-1) * inter
          + jnp.einsum('hqk,hke->hqe', s.astype(BF16), v_ext, preferred_element_type=F32))
    floor = jnp.exp(-lane_bcast(b + mx))
    yield

    hh = nd[:, :, :HEAD_DV] / jnp.maximum(jnp.abs(nd[:, :, HEAD_DV:]), floor)
    ms = _dot((hh * hh).astype(BF16).reshape(N_HEADS * L, HEAD_DV), ones_b).reshape(hh.shape) * (1.0 / HEAD_DV)
    yield

    hn = hh * lax.rsqrt(ms + EPS)
    merged = jnp.concatenate([hn[h] for h in range(N_HEADS)], axis=-1) * nw_ref[...]
    out_ref[...] = (merged * _sigmoid(o_ref[...])).astype(out_ref.dtype)


def mlstm_mix(proj, gates, conv_w, conv_b, gate_bias, norm_w, *, bsz, seq):
    nblk = seq // CHUNK
    dv = N_HEADS * HEAD_DV
    ns = _seqs_per_step(bsz)
    proj = proj.reshape(bsz, seq, proj.shape[-1])
    gates = gates.reshape(bsz, seq, 2 * LANES)
    out = pl.pallas_call(
        _mlstm_kernel,
        grid=(bsz // ns, nblk),
        in_specs=[pl.BlockSpec((ns, CHUNK, D), lambda b, s: (b, s, 0)),
                  pl.BlockSpec((ns, CHUNK, dv), lambda b, s: (b, s, 1)),
                  pl.BlockSpec((ns, CHUNK, dv), lambda b, s: (b, s, 2)),
                  pl.BlockSpec((ns, CHUNK, 2 * LANES), lambda b, s: (b, s, 0)),
                  pl.BlockSpec((CONV_K, D), lambda b, s: (0, 0)),
                  pl.BlockSpec((1, D), lambda b, s: (0, 0)),
                  pl.BlockSpec((1, 2 * LANES), lambda b, s: (0, 0)),
                  pl.BlockSpec((1, dv), lambda b, s: (0, 0))],
        out_specs=pl.BlockSpec((ns, CHUNK, dv), lambda b, s: (b, s, 0)),
        out_shape=jax.ShapeDtypeStruct((bsz, seq, dv), BF16),
        scratch_shapes=[pltpu.VMEM((ns, 8, D), F32),
                        pltpu.VMEM((ns, N_HEADS, LANES, 2 * HEAD_DV), F32),
                        pltpu.VMEM((ns, 1, LANES), F32)],
        compiler_params=_cparams("arbitrary", "arbitrary"),
        name="mlstm_mix",
    )(proj, proj, proj, gates, conv_w, conv_b, gate_bias, norm_w)
    return out.reshape(bsz * seq, dv)


def _hgrn_kernel(q_ref, f_ref, v_ref, g_ref, lb_ref, nw_ref, out_ref, st_ref):
    @pl.when(pl.program_id(1) == 0)
    def _():
        st_ref[...] = jnp.zeros_like(st_ref)

    _interleave([_hgrn_chunk(q_ref.at[i], f_ref.at[i], v_ref.at[i], g_ref.at[i], lb_ref, nw_ref, out_ref.at[i],
                             st_ref.at[i]) for i in range(q_ref.shape[0])])


def _hgrn_chunk(q_ref, f_ref, v_ref, g_ref, lb_ref, nw_ref, out_ref, st_ref):
    L = CHUNK

    lb = lb_ref[...]
    f = f_ref[...]
    e = jnp.exp(-jnp.abs(f))
    r = 1.0 / (1.0 + e)
    er = e * r
    sig = jnp.where(f >= 0.0, r, er)
    nsig = jnp.where(f >= 0.0, er, r)
    g_all = jnp.log(jnp.maximum(lb + (1.0 - lb) * sig, F_FLOOR))
    k_all = (1.0 - lb) * nsig
    q_all = _silu(q_ref[...])
    row = lax.broadcasted_iota(jnp.int32, (L, L), 0)
    col = lax.broadcasted_iota(jnp.int32, (L, L), 1)
    bc_all = _cumsum_rows(g_all)
    xor = row ^ col
    eye = row == col
    ones_b = jnp.ones((LANES, LANES), BF16)
    sub8 = lax.broadcasted_iota(jnp.int32, (1, 8, LANES), 1)
    H = N_HEADS

    def heads(x):
        return jnp.stack([x[:, h * HEAD_DV:(h + 1) * HEAD_DV] for h in range(H)])

    def bmm_nt(x, y):
        return jnp.einsum('hqd,hkd->hqk', x.astype(BF16), y.astype(BF16), preferred_element_type=F32)

    def rot8(x3, d):
        return pltpu.roll(x3, d % 8, axis=1)

    q, k, g, bc = heads(q_all), heads(k_all), heads(g_all), heads(bc_all)
    vb = heads(v_ref[...]).astype(BF16)
    st = st_ref[...]
    b_last = bc[:, L - 1:L, :]

    o_inter = bmm_nt(q * jnp.exp(bc), st)
    k_dec = (k * jnp.exp(b_last - bc)).astype(BF16)
    st_decay = jnp.exp(b_last)
    for h in range(H):
        st_ref[h] = st[h] * st_decay[h] + lax.dot_general(
            vb[h], k_dec[h], (((0,), (0,)), ((), ())), preferred_element_type=F32)

    def level_mask(b):
        return ((xor >= b) & (xor < 2 * b))[None]

    diag = _dot((q * k).astype(BF16).reshape(H * L, HEAD_DV), ones_b).reshape(H, L, L)
    big = []
    for b in (64, 32, 16, 8):
        qs, ks = [], []
        for blk in range(L // (2 * b)):
            lo, mid, hi = 2 * b * blk, 2 * b * blk + b, 2 * b * (blk + 1)
            ref = bc[:, mid - 1:mid, :]
            ks.append(k[:, lo:mid, :] * jnp.exp(ref - bc[:, lo:mid, :]))
            ks.append(jnp.zeros((H, b, LANES), F32))
            qs.append(jnp.zeros((H, b, LANES), F32))
            qs.append(q[:, mid:hi, :] * jnp.exp(bc[:, mid:hi, :] - ref))
        big.append((b, bmm_nt(jnp.concatenate(qs, axis=1), jnp.concatenate(ks, axis=1))))
    yield

    a = jnp.where(eye[None], diag, 0.0)
    for b, a_l in big:
        a = jnp.where(level_mask(b), a_l, a)
    n8 = H * L // 8
    e8 = jnp.exp(g).reshape(n8, 8, LANES)
    q8 = q.reshape(n8, 8, LANES)
    k8 = k.reshape(n8, 8, LANES)
    ew = [e8]
    for j in range(1, 4):
        ew.append(ew[-1] * rot8(e8, j))
    eu = [rot8(e8, -1)]
    for j in range(1, 3):
        eu.append(eu[-1] * rot8(e8, -1 - j))
    small = []
    for b in (4, 2, 1):
        pos = sub8 % (2 * b)
        fq = jnp.where(pos == b, ew[0], 0.0)
        fk = jnp.where(pos == b - 1, 1.0, 0.0)
        for j in range(b - 1):
            fq = jnp.where(pos == b + 1 + j, ew[1 + j], fq)
            fk = jnp.where(pos == j, eu[b - 2 - j], fk)
        small.append((b, bmm_nt((q8 * fq).reshape(H, L, LANES), (k8 * fk).reshape(H, L, LANES))))
    yield

    for b, a_l in small:
        a = jnp.where(level_mask(b), a_l, a)
    o = o_inter + jnp.einsum('hqk,hkv->hqv', a.astype(BF16), vb, preferred_element_type=F32)
    yield

    ms = _dot((o * o).astype(BF16).reshape(H * L, HEAD_DV), ones_b).reshape(o.shape) * (1.0 / HEAD_DV)
    yield

    on = o * lax.rsqrt(ms + EPS)
    merged = jnp.concatenate([on[h] for h in range(H)], axis=-1) * nw_ref[...]
    out_ref[...] = (merged * _silu(g_ref[...])).astype(out_ref.dtype)


def _seqs_per_step(bsz):
    return 2 if bsz % 2 == 0 else 1


def hgrn_mix(proj, lb, norm_w, *, bsz, seq):
    nblk = seq // CHUNK
    ns = _seqs_per_step(bsz)
    proj = proj.reshape(bsz, seq, proj.shape[-1])
    out = pl.pallas_call(
        _hgrn_kernel,
        grid=(bsz // ns, nblk),
        in_specs=[pl.BlockSpec((ns, CHUNK, D), lambda b, s: (b, s, 0)),
                  pl.BlockSpec((ns, CHUNK, D), lambda b, s: (b, s, 1)),
                  pl.BlockSpec((ns, CHUNK, D), lambda b, s: (b, s, 2)),
                  pl.BlockSpec((ns, CHUNK, D), lambda b, s: (b, s, 3)),
                  pl.BlockSpec((1, D), lambda b, s: (0, 0)),
                  pl.BlockSpec((1, D), lambda b, s: (0, 0))],
        out_specs=pl.BlockSpec((ns, CHUNK, D), lambda b, s: (b, s, 0)),
        out_shape=jax.ShapeDtypeStruct((bsz, seq, D), BF16),
        scratch_shapes=[pltpu.VMEM((ns, N_HEADS, HEAD_DV, HEAD_DV), F32)],
        compiler_params=_cparams("arbitrary", "arbitrary"),
        name="hgrn_mix",
    )(proj, proj, proj, proj, lb, norm_w)
    return out.reshape(bsz * seq, D)


def _mixer_residual(x_ref, y_ref, wo_ref, mod_ref):
    return x_ref[...] + mod_ref[0, 2:3, :] * _dot(y_ref[...], wo_ref[...])


def _ffn_kernel(x_ref, y_ref, wo_ref, mod_ref, nw_ref, wg_ref, wu_ref, wd_ref, o_ref, x1_ref, hn_ref, acc_ref):
    f = pl.program_id(1)

    @pl.when(f == 0)
    def _():
        x1 = _mixer_residual(x_ref, y_ref, wo_ref, mod_ref)
        x1_ref[...] = x1
        hn_ref[...] = _norm_mod(x1, nw_ref[...], mod_ref[0, 4:5, :], mod_ref[0, 3:4, :]).astype(BF16)
        acc_ref[...] = jnp.zeros_like(acc_ref)

    hb = hn_ref[...]
    hid = _silu(_dot(hb, wg_ref[...])) * _dot(hb, wu_ref[...])
    acc_ref[...] += _dot(hid.astype(BF16), wd_ref[...])

    @pl.when(f == pl.num_programs(1) - 1)
    def _():
        o_ref[...] = x1_ref[...] + mod_ref[0, 5:6, :] * acc_ref[...]


def ffn(x, y, w_out, mod, norm_w, wg, wu, wd, *, seq, tm, tf):
    t = x.shape[0]
    dff = wg.shape[1]
    per_b = seq // tm
    return pl.pallas_call(
        _ffn_kernel,
        grid=(t // tm, dff // tf),
        in_specs=[pl.BlockSpec((tm, D), lambda i, f: (i, 0)),
                  pl.BlockSpec((tm, D), lambda i, f: (i, 0)),
                  pl.BlockSpec((D, D), lambda i, f: (0, 0)),
                  pl.BlockSpec((1, 8, D), lambda i, f: (i // per_b, 0, 0)),
                  pl.BlockSpec((1, D), lambda i, f: (0, 0)),
                  pl.BlockSpec((D, tf), lambda i, f: (0, f)),
                  pl.BlockSpec((D, tf), lambda i, f: (0, f)),
                  pl.BlockSpec((tf, D), lambda i, f: (f, 0))],
        out_specs=pl.BlockSpec((tm, D), lambda i, f: (i, 0)),
        out_shape=jax.ShapeDtypeStruct((t, D), F32),
        scratch_shapes=[pltpu.VMEM((tm, D), F32), pltpu.VMEM((tm, D), BF16), pltpu.VMEM((tm, D), F32)],
        compiler_params=_cparams("arbitrary", "arbitrary"),
        name="ffn",
    )(x, y, w_out, mod, norm_w, wg, wu, wd)


def _router_kernel(x_ref, y_ref, wo_ref, mod_ref, nw_ref, wr_ref, x1_ref, hn_ref, rw_ref, ri_ref):
    x1 = _mixer_residual(x_ref, y_ref, wo_ref, mod_ref)
    x1_ref[...] = x1
    hn = _norm_mod(x1, nw_ref[...], mod_ref[0, 4:5, :], mod_ref[0, 3:4, :])
    _to_tile_rows(hn_ref, hn)
    h_hi = hn.astype(BF16)
    h_lo = (hn - h_hi.astype(F32)).astype(BF16)
    hw = _dot(h_hi, wr_ref[...])
    logits = hw[:, :LANES] + hw[:, LANES:] + _dot(h_lo, wr_ref[:, :LANES])
    lane = lax.broadcasted_iota(jnp.int32, logits.shape, 1)
    lg = jnp.where(lane < N_EXPERTS, logits, NEG)
    m1 = jnp.max(lg, axis=-1, keepdims=True)
    i1 = jnp.min(jnp.where(lg == m1, lane, LANES), axis=-1, keepdims=True)
    lg2 = jnp.where(lane == i1, NEG, lg)
    m2 = jnp.max(lg2, axis=-1, keepdims=True)
    i2 = jnp.min(jnp.where(lg2 == m2, lane, LANES), axis=-1, keepdims=True)
    e2 = jnp.exp(m2 - m1)
    w1 = 1.0 / (1.0 + e2)
    rw_ref[...] = jnp.where(lane == 0, w1, jnp.where(lane == 1, e2 * w1, 0.0))
    ri_ref[...] = jnp.where(lane == 0, i1, jnp.where(lane == 1, i2, 0))


def router(x, y, w_out, mod, norm_w, wr, *, seq, tm):
    t = x.shape[0]
    per_b = seq // tm
    return pl.pallas_call(
        _router_kernel,
        grid=(t // tm,),
        in_specs=[pl.BlockSpec((tm, D), lambda i: (i, 0)),
                  pl.BlockSpec((tm, D), lambda i: (i, 0)),
                  pl.BlockSpec((D, D), lambda i: (0, 0)),
                  pl.BlockSpec((1, 8, D), lambda i: (i // per_b, 0, 0)),
                  pl.BlockSpec((1, D), lambda i: (0, 0)),
                  pl.BlockSpec((D, 2 * LANES), lambda i: (0, 0))],
        out_specs=[pl.BlockSpec((tm, D), lambda i: (i, 0)),
                   pl.BlockSpec((tm * ROW_TILES, LANES), lambda i: (i, 0)),
                   pl.BlockSpec((tm, LANES), lambda i: (i, 0)),
                   pl.BlockSpec((tm, LANES), lambda i: (i, 0))],
        out_shape=[jax.ShapeDtypeStruct((t, D), F32),
                   jax.ShapeDtypeStruct((t * ROW_TILES, LANES), F32),
                   jax.ShapeDtypeStruct((t, LANES), F32),
                   jax.ShapeDtypeStruct((t, LANES), jnp.int32)],
        compiler_params=_cparams("arbitrary"),
        name="router",
    )(x, y, w_out, mod, norm_w, wr)


SUBLANES = 8
ROW_TILES = D // LANES


def _to_tile_rows(ref, val):
    rows = val.shape[0]
    for s in range(ROW_TILES):
        ref[pl.ds(s, rows, stride=ROW_TILES), :] = val[:, s * LANES:(s + 1) * LANES]


def _from_tile_rows(ref, rows):
    return jnp.concatenate([ref[pl.ds(s, rows, stride=ROW_TILES), :] for s in range(ROW_TILES)], axis=-1)


def _tile_row(ref, row):
    return ref.at[pl.ds(pl.multiple_of(row * ROW_TILES, ROW_TILES), ROW_TILES)]


def _for_each_row_pair(tm, fn):
    def body(row, carry):
        for k in range(2):
            fn(row, 2 * row + k, k)
        return carry

    lax.fori_loop(0, tm, body, 0, unroll=8)


def _dispatch_kernel(ends_ref, dest_ref, hn_ref, xs_ref, zero_ref, sem):
    tm = hn_ref.shape[0] // ROW_TILES
    tgr = zero_ref.shape[0]

    @pl.when(pl.program_id(0) == 0)
    def _():
        zero_ref[...] = jnp.zeros_like(zero_ref)

        def zero_tile(start):
            cp = pltpu.make_async_copy(zero_ref, xs_ref.at[pl.ds(pl.multiple_of(start, tgr), tgr)], sem)
            cp.start()
            cp.wait()

        n_tiles = xs_ref.shape[0] // tgr
        for e in range(N_EXPERTS):
            @pl.when(ends_ref[N_EXPERTS + e] > 0)
            def _():
                zero_tile(ends_ref[e] * ROW_TILES - tgr)

            @pl.when(n_tiles - 1 - e >= ends_ref[2 * N_EXPERTS])
            def _():
                zero_tile((n_tiles - 1 - e) * tgr)

    def scatter(row, slot, k):
        pltpu.make_async_copy(_tile_row(hn_ref, row), _tile_row(xs_ref, dest_ref[0, 0, slot]),
                              sem).start(priority=k)

    _for_each_row_pair(tm, scatter)
    for k in range(2):
        pltpu.make_async_copy(hn_ref, xs_ref.at[pl.ds(0, tm * ROW_TILES)], sem).wait()


def dispatch(ends_padded, dest, hn, *, n_rows, tm, tg):
    t = hn.shape[0] // ROW_TILES
    grid_spec = pltpu.PrefetchScalarGridSpec(
        num_scalar_prefetch=1,
        grid=(t // tm,),
        in_specs=[pl.BlockSpec((1, 1, 2 * tm), lambda i, ends: (i, 0, 0), memory_space=pltpu.SMEM),
                  pl.BlockSpec((tm * ROW_TILES, LANES), lambda i, ends: (i, 0))],
        out_specs=pl.BlockSpec(memory_space=pl.ANY),
        scratch_shapes=[pltpu.VMEM((tg * ROW_TILES, LANES), F32), pltpu.SemaphoreType.DMA],
    )
    return pl.pallas_call(
        _dispatch_kernel,
        grid_spec=grid_spec,
        out_shape=jax.ShapeDtypeStruct((n_rows * ROW_TILES, LANES), F32),
        compiler_params=_cparams("arbitrary"),
        name="moe_dispatch",
    )(ends_padded, dest, hn)


def _expert_ffn_kernel(te_ref, nu_ref, x_ref, wg_ref, wu_ref, wd_ref, o_ref):
    del te_ref
    i = pl.program_id(0)

    @pl.when(i < nu_ref[0])
    def _():
        hb = _from_tile_rows(x_ref, x_ref.shape[0] // ROW_TILES).astype(BF16)
        hid = _silu(_dot(hb, wg_ref[0])) * _dot(hb, wu_ref[0])
        _to_tile_rows(o_ref, _dot(hid.astype(BF16), wd_ref[0]))

    @pl.when(i >= nu_ref[0])
    def _():
        o_ref[...] = jnp.zeros_like(o_ref)


def expert_ffn(tile_expert, n_used, xs, wg, wu, wd, *, tg):
    r = xs.shape[0] // ROW_TILES
    dfe = wg.shape[2]
    grid_spec = pltpu.PrefetchScalarGridSpec(
        num_scalar_prefetch=2,
        grid=(r // tg,),
        in_specs=[pl.BlockSpec((tg * ROW_TILES, LANES), lambda i, te, nu: (jnp.minimum(i, nu[0] - 1), 0)),
                  pl.BlockSpec((1, D, dfe), lambda i, te, nu: (te[i], 0, 0)),
                  pl.BlockSpec((1, D, dfe), lambda i, te, nu: (te[i], 0, 0)),
                  pl.BlockSpec((1, dfe, D), lambda i, te, nu: (te[i], 0, 0))],
        out_specs=pl.BlockSpec((tg * ROW_TILES, LANES), lambda i, te, nu: (i, 0)),
    )
    return pl.pallas_call(
        _expert_ffn_kernel,
        grid_spec=grid_spec,
        out_shape=jax.ShapeDtypeStruct((r * ROW_TILES, LANES), F32),
        compiler_params=_cparams("arbitrary"),
        name="moe_expert_ffn",
    )(tile_expert, n_used, xs, wg, wu, wd)


def _combine_kernel(dest_ref, x_ref, mod_ref, rw_ref, fw_ref, ys_ref, o_ref, buf_ref, sem, *, final):
    tm = x_ref.shape[0]

    def gather(row, slot, k):
        pltpu.make_async_copy(_tile_row(ys_ref, dest_ref[0, 0, slot]), _tile_row(buf_ref.at[k], row),
                              sem).start(priority=k)

    _for_each_row_pair(tm, gather)
    for k in range(2):
        pltpu.make_async_copy(ys_ref.at[pl.ds(0, tm * ROW_TILES)], buf_ref.at[k], sem).wait()
    rw = rw_ref[...]
    y = rw[:, 0:1] * _from_tile_rows(buf_ref.at[0], tm) + rw[:, 1:2] * _from_tile_rows(buf_ref.at[1], tm)
    out = x_ref[...] + mod_ref[0, 5:6, :] * y
    if final:
        out = out * lax.rsqrt(jnp.mean(out * out, axis=-1, keepdims=True) + EPS) * fw_ref[...]
    o_ref[...] = out


def combine(dest, x, mod, rw, final_w, ys, *, seq, tm, final):
    t = x.shape[0]
    per_b = seq // tm
    return pl.pallas_call(
        functools.partial(_combine_kernel, final=final),
        grid=(t // tm,),
        in_specs=[pl.BlockSpec((1, 1, 2 * tm), lambda i: (i, 0, 0), memory_space=pltpu.SMEM),
                  pl.BlockSpec((tm, D), lambda i: (i, 0)),
                  pl.BlockSpec((1, 8, D), lambda i: (i // per_b, 0, 0)),
                  pl.BlockSpec((tm, LANES), lambda i: (i, 0)),
                  pl.BlockSpec((1, D), lambda i: (0, 0)),
                  pl.BlockSpec(memory_space=pl.ANY)],
        out_specs=pl.BlockSpec((tm, D), lambda i: (i, 0)),
        out_shape=jax.ShapeDtypeStruct((t, D), F32),
        scratch_shapes=[pltpu.VMEM((2, tm * ROW_TILES, LANES), F32), pltpu.SemaphoreType.DMA],
        compiler_params=_cparams("arbitrary"),
        name="moe_combine",
    )(dest, x, mod, rw, final_w, ys)


def _routing_plan(route_i, *, tm, tg):
    t = route_i.shape[0]
    e_flat = route_i[:, :2].reshape(-1)
    onehot = (e_flat[:, None] == jnp.arange(N_EXPERTS, dtype=jnp.int32)[None, :]).astype(jnp.int32)
    csum = jnp.cumsum(onehot, axis=0)
    counts = csum[-1]
    padded = ((counts + tg - 1) // tg) * tg
    ends = jnp.cumsum(padded)
    starts = ends - padded
    dest = jnp.sum(onehot * (starts[None, :] + csum - onehot), axis=1)
    n_tiles = (2 * t) // tg + N_EXPERTS
    tile_start = jnp.arange(n_tiles, dtype=jnp.int32) * tg
    tile_expert = jnp.sum((tile_start[:, None] >= ends[None, :]).astype(jnp.int32), axis=1)
    n_used = (ends[-1] // tg).astype(jnp.int32)
    last_expert = jnp.take(tile_expert, jnp.maximum(n_used - 1, 0))
    tile_expert = jnp.where(tile_start < ends[-1], tile_expert, last_expert).astype(jnp.int32)
    ends_padded = jnp.concatenate([ends, padded, n_used.reshape(1)]).astype(jnp.int32)
    return (dest.reshape(t // tm, 1, 2 * tm).astype(jnp.int32), tile_expert, n_used.reshape(1), ends_padded,
            n_tiles * tg)


def moe(x, y, w_out, mod, norm_w, wr, wg, wu, wd, final_w, *, seq, tm, tg, final):
    x1, hn, rw, ri = router(x, y, w_out, mod, norm_w, wr, seq=seq, tm=tm)
    dest, tile_expert, n_used, ends_padded, n_rows = _routing_plan(ri, tm=tm, tg=tg)
    xs = dispatch(ends_padded, dest, hn, n_rows=n_rows, tm=tm, tg=tg)
    ys = expert_ffn(tile_expert, n_used, xs, wg, wu, wd, tg=tg)
    return combine(dest, x1, mod, rw, final_w, ys, seq=seq, tm=tm, final=final)


def _pad_cols(w, n):
    return jnp.pad(w, ((0, 0), (0, n - w.shape[1])))


def kernel(x, c, ada_w, ada_b, norm1_w, norm2_w, final_norm_w, m_w_in, m_i_bias, m_f_bias, m_conv_w, m_conv_b, m_norm_w, m_w_out, h_w_in, h_lb_logits, h_norm_w, h_w_out, ffn_w_gate, ffn_w_up, ffn_w_down, moe_router, moe_w_gate, moe_w_up, moe_w_down):
    bsz, seq, _ = x.shape
    t = bsz * seq
    tm = min(512, seq)
    xt = x.reshape(t, D)

    mod_all = ada_modulation(c, ada_w, ada_b)
    mod_all = jnp.pad(mod_all.reshape(DEPTH, bsz, 6, D), ((0, 0), (0, 0), (0, 2), (0, 0)))

    gam = jax.nn.softmax(h_lb_logits.astype(F32), axis=0)
    lower_bounds = jnp.cumsum(gam, axis=0) - gam[:1]

    n_main = 2 * N_HEADS * M_DQK + 2 * N_HEADS * HEAD_DV
    tm_in = min(1024, seq)
    final_w = final_norm_w.reshape(1, D)
    for layer in range(DEPTH):
        j = layer // 2
        mod = mod_all[layer]
        n1 = norm1_w[layer].reshape(1, D)
        n2 = norm2_w[layer].reshape(1, D)
        if layer % 2 == 0:
            w_in = m_w_in[j]
            w_gates = jnp.concatenate([_pad_cols(w_in[:, n_main:n_main + N_HEADS], LANES),
                                       _pad_cols(w_in[:, n_main + N_HEADS:], LANES)], axis=1).astype(BF16)
            proj, gates = in_proj(xt, mod, n1, w_in[:, :n_main].astype(BF16), w_gates,
                                  seq=seq, tm=tm_in, tn=n_main // 2)
            gate_bias = jnp.concatenate([_pad_cols(m_i_bias[j].reshape(1, -1), LANES),
                                         _pad_cols(m_f_bias[j].reshape(1, -1), LANES)], axis=1)
            y = mlstm_mix(proj, gates, m_conv_w[j], m_conv_b[j].reshape(1, -1), gate_bias,
                          m_norm_w[j].reshape(1, -1), bsz=bsz, seq=seq)
            xt = ffn(xt, y, m_w_out[j].astype(BF16), mod, n2, ffn_w_gate[j].astype(BF16),
                     ffn_w_up[j].astype(BF16), ffn_w_down[j].astype(BF16), seq=seq, tm=tm, tf=1408)
        else:
            w_in = h_w_in[j].astype(BF16)
            (proj,) = in_proj(xt, mod, n1, w_in, None, seq=seq, tm=tm_in, tn=w_in.shape[1] // 2)
            y = hgrn_mix(proj, lower_bounds[j].reshape(1, D), h_norm_w[j].reshape(1, D), bsz=bsz, seq=seq)
            wr = _pad_cols(moe_router[j], LANES)
            wr_hi = wr.astype(BF16)
            wr_split = jnp.concatenate([wr_hi, (wr - wr_hi.astype(F32)).astype(BF16)], axis=1)
            xt = moe(xt, y, h_w_out[j].astype(BF16), mod, n2, wr_split,
                     moe_w_gate[j].astype(BF16), moe_w_up[j].astype(BF16), moe_w_down[j].astype(BF16), final_w,
                     seq=seq, tm=tm, tg=tm, final=layer == DEPTH - 1)
    return xt.reshape(bsz, seq, D)
```

```python
import functools

import jax
import jax.numpy as jnp
from jax import lax
from jax.experimental import pallas as pl
from jax.experimental.pallas import tpu as pltpu

F32 = jnp.float32
BF16 = jnp.bfloat16

DEPTH = 4
D = 1024
N_HEADS = 8
HEAD_DV = 128
M_DQK = 64
CONV_K = 4
IGATE_CAP = 15.0
N_EXPERTS = 8
EPS = 1e-6
NEG = -1e30
F_FLOOR = 1e-30
LANES = 128
CHUNK = 128
SUB = 16
VMEM_LIMIT = 56 * 1024 * 1024


def _cparams(*sem):
    return pltpu.CompilerParams(dimension_semantics=sem, vmem_limit_bytes=VMEM_LIMIT)


def _sigmoid(x):
    return 1.0 / (1.0 + jnp.exp(-x))


def _silu(x):
    return x * _sigmoid(x)


def _dot(a, b):
    return jnp.dot(a, b, preferred_element_type=F32)


def _dot_nt(a, b):
    return lax.dot_general(a, b, (((1,), (1,)), ((), ())), preferred_element_type=F32)


def _dot_f32(a, b):
    return jnp.dot(a, b, preferred_element_type=F32, precision=lax.Precision.HIGHEST)


def _cumsum_rows(x):
    n = x.shape[0]
    tri = (lax.broadcasted_iota(jnp.int32, (n, n), 1) <= lax.broadcasted_iota(jnp.int32, (n, n), 0)).astype(BF16)
    x1 = x.astype(BF16)
    r1 = x - x1.astype(F32)
    x2 = r1.astype(BF16)
    x3 = (r1 - x2.astype(F32)).astype(BF16)
    return _dot(tri, x1) + _dot(tri, x2) + _dot(tri, x3)


def _norm_mod(x, norm_w, scale, shift):
    y = x * lax.rsqrt(jnp.mean(x * x, axis=-1, keepdims=True) + EPS)
    return (y * norm_w) * (1.0 + scale) + shift


def _ada_kernel(c_ref, w_ref, b_ref, o_ref):
    c = c_ref[...]
    o_ref[0] = _dot_f32(_silu(c), w_ref[0]) + b_ref[0]


def ada_modulation(c, ada_w, ada_b):
    bsz = c.shape[0]
    tn = 1536
    return pl.pallas_call(
        _ada_kernel,
        grid=(DEPTH, 6 * D // tn),
        in_specs=[pl.BlockSpec((bsz, D), lambda l, j: (0, 0)),
                  pl.BlockSpec((1, D, tn), lambda l, j: (l, 0, j)),
                  pl.BlockSpec((1, 1, tn), lambda l, j: (l, 0, j))],
        out_specs=pl.BlockSpec((1, bsz, tn), lambda l, j: (l, 0, j)),
        out_shape=jax.ShapeDtypeStruct((DEPTH, bsz, 6 * D), F32),
        compiler_params=_cparams("arbitrary", "arbitrary"),
        name="ada_modulation",
    )(c, ada_w, ada_b.reshape(DEPTH, 1, 6 * D))


def _in_proj_kernel(x_ref, mod_ref, nw_ref, w_ref, *rest, has_extra):
    if has_extra:
        wx_ref, o_ref, ox_ref, hn_ref = rest
    else:
        o_ref, hn_ref = rest

    @pl.when(pl.program_id(1) == 0)
    def _():
        hn = _norm_mod(x_ref[...], nw_ref[...], mod_ref[0, 1:2, :], mod_ref[0, 0:1, :])
        hn_ref[...] = hn.astype(BF16)
        if has_extra:
            ox_ref[...] = _dot(hn_ref[...], wx_ref[...])

    o_ref[...] = _dot(hn_ref[...], w_ref[...])


def in_proj(x, mod, norm_w, w, w_extra, *, seq, tm, tn):
    t = x.shape[0]
    n = w.shape[1]
    per_b = seq // tm
    has_extra = w_extra is not None
    in_specs = [pl.BlockSpec((tm, D), lambda i, j: (i, 0)),
                pl.BlockSpec((1, 8, D), lambda i, j: (i // per_b, 0, 0)),
                pl.BlockSpec((1, D), lambda i, j: (0, 0)),
                pl.BlockSpec((D, tn), lambda i, j: (0, j))]
    out_specs = [pl.BlockSpec((tm, tn), lambda i, j: (i, j))]
    out_shape = [jax.ShapeDtypeStruct((t, n), F32)]
    args = [x, mod, norm_w, w]
    if has_extra:
        nx = w_extra.shape[1]
        in_specs.append(pl.BlockSpec((D, nx), lambda i, j: (0, 0)))
        out_specs.append(pl.BlockSpec((tm, nx), lambda i, j: (i, 0)))
        out_shape.append(jax.ShapeDtypeStruct((t, nx), F32))
        args.append(w_extra)
    return pl.pallas_call(
        functools.partial(_in_proj_kernel, has_extra=has_extra),
        grid=(t // tm, n // tn),
        in_specs=in_specs,
        out_specs=out_specs,
        out_shape=out_shape,
        scratch_shapes=[pltpu.VMEM((tm, D), BF16)],
        compiler_params=_cparams("arbitrary", "arbitrary"),
        name="in_proj",
    )(*args)


def _interleave(chunks):
    live = list(chunks)
    while live:
        for gen in list(live):
            if next(gen, StopIteration) is StopIteration:
                live.remove(gen)


def _mlstm_kernel(qk_ref, v_ref, o_ref, g_ref, cw_ref, cb_ref, gb_ref, nw_ref, out_ref,
                  tail_ref, c_ref, m_ref):
    @pl.when(pl.program_id(1) == 0)
    def _():
        tail_ref[...] = jnp.zeros_like(tail_ref)
        c_ref[...] = jnp.zeros_like(c_ref)
        m_ref[...] = jnp.zeros_like(m_ref)

    _interleave([_mlstm_chunk(qk_ref.at[i], v_ref.at[i], o_ref.at[i], g_ref.at[i], cw_ref, cb_ref, gb_ref, nw_ref,
                              out_ref.at[i], tail_ref.at[i], c_ref.at[i], m_ref.at[i])
                 for i in range(qk_ref.shape[0])])


def _mlstm_chunk(qk_ref, v_ref, o_ref, g_ref, cw_ref, cb_ref, gb_ref, nw_ref, out_ref, tail_ref, c_ref, m_ref):
    L = CHUNK

    raw = qk_ref[...]
    xx = jnp.concatenate([tail_ref[...], raw], axis=0)
    tail_ref[...] = raw[L - 8:, :]
    y = cb_ref[...] + cw_ref[CONV_K - 1:CONV_K, :] * raw
    for j in range(CONV_K - 1):
        y = y + cw_ref[j:j + 1, :] * pltpu.roll(xx, CONV_K - 1 - j, axis=0)[8:, :]
    qk = _silu(y)
    half = N_HEADS * M_DQK
    q_all = qk[:, :half]
    k_all = qk[:, half:] * (M_DQK ** -0.5)

    g = g_ref[...] + gb_ref[...]
    li = IGATE_CAP * jnp.tanh(g[:, :LANES] * (1.0 / IGATE_CAP))
    gf = g[:, LANES:]
    b = _cumsum_rows(jnp.minimum(gf, 0.0) - jnp.log1p(jnp.exp(-jnp.abs(gf))))
    tril = lax.broadcasted_iota(jnp.int32, (L, L), 1) <= lax.broadcasted_iota(jnp.int32, (L, L), 0)

    a = li - b
    amax8 = a.reshape(L // SUBLANES, SUBLANES, LANES)
    sub = lax.broadcasted_iota(jnp.int32, (1, SUBLANES, LANES), 1)
    for sh in (1, 2, 4):
        amax8 = jnp.where(sub >= sh, jnp.maximum(amax8, pltpu.roll(amax8, sh, axis=1)), amax8)
    groups = [amax8[0]]
    for v in range(1, L // SUBLANES):
        carry = jnp.broadcast_to(groups[-1][SUBLANES - 1:SUBLANES, :], (SUBLANES, LANES))
        groups.append(jnp.maximum(amax8[v], carry))
    amax = jnp.concatenate(groups, axis=0)
    m_prev = m_ref[...]
    mx = jnp.maximum(m_prev, amax)
    mx_last = mx[L - 1:L, :]
    m_ref[...] = b[L - 1:L, :] + mx_last
    decay = jnp.exp(m_prev - mx_last)
    a_r = a.T[0:N_HEADS, :]
    wk_r = jnp.exp(a - mx_last).T[0:N_HEADS, :]

    lane = lax.broadcasted_iota(jnp.int32, (L, LANES), 1)
    ones_b = jnp.ones((L, LANES), BF16)
    kt_pairs = [k_all[:, p * LANES:(p + 1) * LANES].T for p in range(N_HEADS // 2)]
    qb_l, kb_l, ve_l, kw_l = [], [], [], []
    for h in range(N_HEADS):
        p, hf = h // 2, h % 2
        in_head = (lane >= hf * M_DQK) & (lane < (hf + 1) * M_DQK)
        qb_l.append(jnp.where(in_head, q_all[:, p * LANES:(p + 1) * LANES], 0.0).astype(BF16))
        kb_l.append(k_all[:, p * LANES:(p + 1) * LANES].astype(BF16))
        ve_l.append(jnp.concatenate([v_ref[:, h * HEAD_DV:(h + 1) * HEAD_DV].astype(BF16), ones_b], axis=-1))
        kw_l.append((kt_pairs[p] * wk_r[h:h + 1, :]).astype(BF16))
    qb = jnp.stack(qb_l)
    kb = jnp.stack(kb_l)
    v_ext = jnp.stack(ve_l)
    c_st = c_ref[...]

    qk_raw = jnp.einsum('hqd,hkd->hqk', qb, kb, preferred_element_type=F32)
    inter = jnp.einsum('hqd,hde->hqe', qb, c_st.astype(BF16), preferred_element_type=F32)
    for h in range(N_HEADS):
        c_ref[h] = decay[:, h:h + 1] * c_st[h] + _dot(kw_l[h], v_ext[h])
    yield

    def lane_bcast(cols):
        return jnp.stack([jnp.broadcast_to(cols[:, h:h + 1], (L, LANES)) for h in range(N_HEADS)])

    mx_b = lane_bcast(mx)
    m_prev_b = jnp.stack([jnp.broadcast_to(m_prev[:, h:h + 1], (1, LANES)) for h in range(N_HEADS)])
    w_inter = jnp.exp(m_prev_b - mx_b)
    pmat = jnp.exp(jnp.where(tril[None], a_r[:, None, :] - mx_b, NEG))
    s = qk_raw * pmat
    nd = (jnp.concatenate([w_inter, w_inter], axis=-1) * inter
          + jnp.einsum('hqk,hke->hqe', s.astype(BF16), v_ext, preferred_element_type=F32))
    floor = jnp.exp(-lane_bcast(b + mx))
    yield

    hh = nd[:, :, :HEAD_DV] / jnp.maximum(jnp.abs(nd[:, :, HEAD_DV:]), floor)
    ms = _dot((hh * hh).astype(BF16).reshape(N_HEADS * L, HEAD_DV), ones_b).reshape(hh.shape) * (1.0 / HEAD_DV)
    yield

    hn = hh * lax.rsqrt(ms + EPS)
    merged = jnp.concatenate([hn[h] for h in range(N_HEADS)], axis=-1) * nw_ref[...]
    out_ref[...] = (merged * _sigmoid(o_ref[...])).astype(out_ref.dtype)


def mlstm_mix(proj, gates, conv_w, conv_b, gate_bias, norm_w, *, bsz, seq):
    nblk = seq // CHUNK
    dv = N_HEADS * HEAD_DV
    ns = _seqs_per_step(bsz)
    proj = proj.reshape(bsz, seq, proj.shape[-1])
    gates = gates.reshape(bsz, seq, 2 * LANES)
    out = pl.pallas_call(
        _mlstm_kernel,
        grid=(bsz // ns, nblk),
        in_specs=[pl.BlockSpec((ns, CHUNK, D), lambda b, s: (b, s, 0)),
                  pl.BlockSpec((ns, CHUNK, dv), lambda b, s: (b, s, 1)),
                  pl.BlockSpec((ns, CHUNK, dv), lambda b, s: (b, s, 2)),
                  pl.BlockSpec((ns, CHUNK, 2 * LANES), lambda b, s: (b, s, 0)),
                  pl.BlockSpec((CONV_K, D), lambda b, s: (0, 0)),
                  pl.BlockSpec((1, D), lambda b, s: (0, 0)),
                  pl.BlockSpec((1, 2 * LANES), lambda b, s: (0, 0)),
                  pl.BlockSpec((1, dv), lambda b, s: (0, 0))],
        out_specs=pl.BlockSpec((ns, CHUNK, dv), lambda b, s: (b, s, 0)),
        out_shape=jax.ShapeDtypeStruct((bsz, seq, dv), BF16),
        scratch_shapes=[pltpu.VMEM((ns, 8, D), F32),
                        pltpu.VMEM((ns, N_HEADS, LANES, 2 * HEAD_DV), F32),
                        pltpu.VMEM((ns, 1, LANES), F32)],
        compiler_params=_cparams("arbitrary", "arbitrary"),
        name="mlstm_mix",
    )(proj, proj, proj, gates, conv_w, conv_b, gate_bias, norm_w)
    return out.reshape(bsz * seq, dv)


def _hgrn_kernel(q_ref, f_ref, v_ref, g_ref, lb_ref, nw_ref, out_ref, st_ref):
    @pl.when(pl.program_id(1) == 0)
    def _():
        st_ref[...] = jnp.zeros_like(st_ref)

    _interleave([_hgrn_chunk(q_ref.at[i], f_ref.at[i], v_ref.at[i], g_ref.at[i], lb_ref, nw_ref, out_ref.at[i],
                             st_ref.at[i]) for i in range(q_ref.shape[0])])


def _hgrn_chunk(q_ref, f_ref, v_ref, g_ref, lb_ref, nw_ref, out_ref, st_ref):
    L = CHUNK

    lb = lb_ref[...]
    f = f_ref[...]
    e = jnp.exp(-jnp.abs(f))
    r = 1.0 / (1.0 + e)
    er = e * r
    sig = jnp.where(f >= 0.0, r, er)
    nsig = jnp.where(f >= 0.0, er, r)
    dec_all = jnp.maximum(lb + (1.0 - lb) * sig, F_FLOOR)
    k_all = (1.0 - lb) * nsig
    q_all = _silu(q_ref[...])
    row = lax.broadcasted_iota(jnp.int32, (L, L), 0)
    col = lax.broadcasted_iota(jnp.int32, (L, L), 1)
    xor = row ^ col
    eye = row == col
    ones_b = jnp.ones((LANES, LANES), BF16)
    sub8 = lax.broadcasted_iota(jnp.int32, (1, 8, LANES), 1)
    H = N_HEADS
    G = L // SUBLANES

    def heads(x):
        return jnp.stack([x[:, h * HEAD_DV:(h + 1) * HEAD_DV] for h in range(H)])

    def bmm_nt(x, y):
        return jnp.einsum('hqd,hkd->hqk', x.astype(BF16), y.astype(BF16), preferred_element_type=F32)

    def rot8(x3, d):
        return pltpu.roll(x3, d % 8, axis=1)

    def level_mask(b):
        return ((xor >= b) & (xor < 2 * b))[None]

    q, k, dec = heads(q_all), heads(k_all), heads(dec_all)
    vb = heads(v_ref[...]).astype(BF16)
    st = st_ref[...]

    diag = _dot((q * k).astype(BF16).reshape(H * L, HEAD_DV), ones_b).reshape(H, L, L)

    n8 = H * G
    e8 = dec.reshape(n8, 8, LANES)
    q8 = q.reshape(n8, 8, LANES)
    k8 = k.reshape(n8, 8, LANES)
    ew = [e8]
    for j in range(1, 4):
        ew.append(ew[-1] * rot8(e8, j))
    eu = [rot8(e8, -1)]
    for j in range(1, 3):
        eu.append(eu[-1] * rot8(e8, -1 - j))
    pos2, pos4 = sub8 % 2, sub8 % 4
    pre2 = jnp.where(pos2 == 0, ew[0], ew[1])
    suf2 = jnp.where(pos2 == 0, eu[0], 1.0)
    pre4 = jnp.where(pos4 == 0, ew[0], jnp.where(pos4 == 1, ew[1], jnp.where(pos4 == 2, ew[2], ew[3])))
    suf4 = jnp.where(pos4 == 0, eu[2], jnp.where(pos4 == 1, eu[1], jnp.where(pos4 == 2, eu[0], 1.0)))
    small = [
        (4, jnp.where(sub8 >= 4, pre4, 0.0), jnp.where(sub8 < 4, suf4, 0.0)),
        (2, jnp.where(pos4 >= 2, pre2, 0.0), jnp.where(pos4 < 2, suf2, 0.0)),
        (1, jnp.where(pos2 == 1, e8, 0.0), jnp.where(pos2 == 0, 1.0, 0.0)),
    ]
    levels = [(b, bmm_nt((q8 * fq).reshape(H, L, LANES), (k8 * fk).reshape(H, L, LANES))) for b, fq, fk in small]

    pre = pre4 * jnp.where(sub8 >= 4, pre4[:, 3:4, :], 1.0)
    suf = suf4 * jnp.where(sub8 < 4, pre4[:, 7:8, :], 1.0)
    tot = pre[:, 7:8, :]
    q6 = q.reshape(H, G, 1, 1, SUBLANES, LANES)
    k6 = k.reshape(H, G, 1, 1, SUBLANES, LANES)
    m = 1
    while m < G:
        nb = G // (2 * m)
        shape = (H, nb, 2, m, SUBLANES, LANES)
        pre, suf, q6, k6 = (x.reshape(shape) for x in (pre, suf, q6, k6))
        tot = tot.reshape(H, nb, 2, 1, 1, LANES)
        zeros = jnp.zeros((H, nb, 1, m, SUBLANES, LANES), F32)
        q_l = jnp.concatenate([zeros, q6[:, :, 1:] * pre[:, :, 1:]], axis=2)
        k_l = jnp.concatenate([k6[:, :, :1] * suf[:, :, :1], zeros], axis=2)
        levels.append((SUBLANES * m, bmm_nt(q_l.reshape(H, L, LANES), k_l.reshape(H, L, LANES))))
        pre = jnp.concatenate([pre[:, :, :1], pre[:, :, 1:] * tot[:, :, :1]], axis=2)
        suf = jnp.concatenate([suf[:, :, :1] * tot[:, :, 1:], suf[:, :, 1:]], axis=2)
        tot = tot[:, :, :1] * tot[:, :, 1:]
        m *= 2
    pre = pre.reshape(H, L, LANES)
    suf = suf.reshape(H, L, LANES)
    tot = tot.reshape(H, 1, LANES)

    o_inter = bmm_nt(q * pre, st)
    k_dec = (k * suf).astype(BF16)
    for h in range(H):
        st_ref[h] = st[h] * tot[h] + lax.dot_general(
            vb[h], k_dec[h], (((0,), (0,)), ((), ())), preferred_element_type=F32)
    yield

    a = jnp.where(eye[None], diag, 0.0)
    for b, a_l in levels:
        a = jnp.where(level_mask(b), a_l, a)
    o = o_inter + jnp.einsum('hqk,hkv->hqv', a.astype(BF16), vb, preferred_element_type=F32)
    yield

    ms = _dot((o * o).astype(BF16).reshape(H * L, HEAD_DV), ones_b).reshape(o.shape) * (1.0 / HEAD_DV)
    yield

    on = o * lax.rsqrt(ms + EPS)
    merged = jnp.concatenate([on[h] for h in range(H)], axis=-1) * nw_ref[...]
    out_ref[...] = (merged * _silu(g_ref[...])).astype(out_ref.dtype)


def _seqs_per_step(bsz):
    return 2 if bsz % 2 == 0 else 1


def hgrn_mix(proj, lb, norm_w, *, bsz, seq):
    nblk = seq // CHUNK
    ns = _seqs_per_step(bsz)
    proj = proj.reshape(bsz, seq, proj.shape[-1])
    out = pl.pallas_call(
        _hgrn_kernel,
        grid=(bsz // ns, nblk),
        in_specs=[pl.BlockSpec((ns, CHUNK, D), lambda b, s: (b, s, 0)),
                  pl.BlockSpec((ns, CHUNK, D), lambda b, s: (b, s, 1)),
                  pl.BlockSpec((ns, CHUNK, D), lambda b, s: (b, s, 2)),
                  pl.BlockSpec((ns, CHUNK, D), lambda b, s: (b, s, 3)),
                  pl.BlockSpec((1, D), lambda b, s: (0, 0)),
                  pl.BlockSpec((1, D), lambda b, s: (0, 0))],
        out_specs=pl.BlockSpec((ns, CHUNK, D), lambda b, s: (b, s, 0)),
        out_shape=jax.ShapeDtypeStruct((bsz, seq, D), BF16),
        scratch_shapes=[pltpu.VMEM((ns, N_HEADS, HEAD_DV, HEAD_DV), F32)],
        compiler_params=_cparams("arbitrary", "arbitrary"),
        name="hgrn_mix",
    )(proj, proj, proj, proj, lb, norm_w)
    return out.reshape(bsz * seq, D)


def _mixer_residual(x_ref, y_ref, wo_ref, mod_ref):
    return x_ref[...] + mod_ref[0, 2:3, :] * _dot(y_ref[...], wo_ref[...])


def _ffn_kernel(x_ref, y_ref, wo_ref, mod_ref, nw_ref, wg_ref, wu_ref, wd_ref, o_ref, x1_ref, hn_ref, acc_ref):
    f = pl.program_id(1)

    @pl.when(f == 0)
    def _():
        x1 = _mixer_residual(x_ref, y_ref, wo_ref, mod_ref)
        x1_ref[...] = x1
        hn_ref[...] = _norm_mod(x1, nw_ref[...], mod_ref[0, 4:5, :], mod_ref[0, 3:4, :]).astype(BF16)
        acc_ref[...] = jnp.zeros_like(acc_ref)

    hb = hn_ref[...]
    hid = _silu(_dot(hb, wg_ref[...])) * _dot(hb, wu_ref[...])
    acc_ref[...] += _dot(hid.astype(BF16), wd_ref[...])

    @pl.when(f == pl.num_programs(1) - 1)
    def _():
        o_ref[...] = x1_ref[...] + mod_ref[0, 5:6, :] * acc_ref[...]


def ffn(x, y, w_out, mod, norm_w, wg, wu, wd, *, seq, tm, tf):
    t = x.shape[0]
    dff = wg.shape[1]
    per_b = seq // tm
    return pl.pallas_call(
        _ffn_kernel,
        grid=(t // tm, dff // tf),
        in_specs=[pl.BlockSpec((tm, D), lambda i, f: (i, 0)),
                  pl.BlockSpec((tm, D), lambda i, f: (i, 0)),
                  pl.BlockSpec((D, D), lambda i, f: (0, 0)),
                  pl.BlockSpec((1, 8, D), lambda i, f: (i // per_b, 0, 0)),
                  pl.BlockSpec((1, D), lambda i, f: (0, 0)),
                  pl.BlockSpec((D, tf), lambda i, f: (0, f)),
                  pl.BlockSpec((D, tf), lambda i, f: (0, f)),
                  pl.BlockSpec((tf, D), lambda i, f: (f, 0))],
        out_specs=pl.BlockSpec((tm, D), lambda i, f: (i, 0)),
        out_shape=jax.ShapeDtypeStruct((t, D), F32),
        scratch_shapes=[pltpu.VMEM((tm, D), F32), pltpu.VMEM((tm, D), BF16), pltpu.VMEM((tm, D), F32)],
        compiler_params=_cparams("arbitrary", "arbitrary"),
        name="ffn",
    )(x, y, w_out, mod, norm_w, wg, wu, wd)


def _router_kernel(x_ref, y_ref, wo_ref, mod_ref, nw_ref, wr_ref, x1_ref, hn_ref, rw_ref, ri_ref):
    x1 = _mixer_residual(x_ref, y_ref, wo_ref, mod_ref)
    x1_ref[...] = x1
    hn = _norm_mod(x1, nw_ref[...], mod_ref[0, 4:5, :], mod_ref[0, 3:4, :])
    _to_tile_rows(hn_ref, hn)
    h_hi = hn.astype(BF16)
    h_lo = (hn - h_hi.astype(F32)).astype(BF16)
    hw = _dot(h_hi, wr_ref[...])
    logits = hw[:, :LANES] + hw[:, LANES:] + _dot(h_lo, wr_ref[:, :LANES])
    lane = lax.broadcasted_iota(jnp.int32, logits.shape, 1)
    lg = jnp.where(lane < N_EXPERTS, logits, NEG)
    m1 = jnp.max(lg, axis=-1, keepdims=True)
    i1 = jnp.min(jnp.where(lg == m1, lane, LANES), axis=-1, keepdims=True)
    lg2 = jnp.where(lane == i1, NEG, lg)
    m2 = jnp.max(lg2, axis=-1, keepdims=True)
    i2 = jnp.min(jnp.where(lg2 == m2, lane, LANES), axis=-1, keepdims=True)
    e2 = jnp.exp(m2 - m1)
    w1 = 1.0 / (1.0 + e2)
    rw_ref[...] = jnp.where(lane == 0, w1, jnp.where(lane == 1, e2 * w1, 0.0))
    ri_ref[...] = jnp.where(lane == 0, i1, jnp.where(lane == 1, i2, 0))


def router(x, y, w_out, mod, norm_w, wr, *, seq, tm):
    t = x.shape[0]
    per_b = seq // tm
    return pl.pallas_call(
        _router_kernel,
        grid=(t // tm,),
        in_specs=[pl.BlockSpec((tm, D), lambda i: (i, 0)),
                  pl.BlockSpec((tm, D), lambda i: (i, 0)),
                  pl.BlockSpec((D, D), lambda i: (0, 0)),
                  pl.BlockSpec((1, 8, D), lambda i: (i // per_b, 0, 0)),
                  pl.BlockSpec((1, D), lambda i: (0, 0)),
                  pl.BlockSpec((D, 2 * LANES), lambda i: (0, 0))],
        out_specs=[pl.BlockSpec((tm, D), lambda i: (i, 0)),
                   pl.BlockSpec((tm * ROW_TILES, LANES), lambda i: (i, 0)),
                   pl.BlockSpec((tm, LANES), lambda i: (i, 0)),
                   pl.BlockSpec((tm, LANES), lambda i: (i, 0))],
        out_shape=[jax.ShapeDtypeStruct((t, D), F32),
                   jax.ShapeDtypeStruct((t * ROW_TILES, LANES), F32),
                   jax.ShapeDtypeStruct((t, LANES), F32),
                   jax.ShapeDtypeStruct((t, LANES), jnp.int32)],
        compiler_params=_cparams("arbitrary"),
        name="router",
    )(x, y, w_out, mod, norm_w, wr)


SUBLANES = 8
ROW_TILES = D // LANES


def _to_tile_rows(ref, val):
    rows = val.shape[0]
    for s in range(ROW_TILES):
        ref[pl.ds(s, rows, stride=ROW_TILES), :] = val[:, s * LANES:(s + 1) * LANES]


def _from_tile_rows(ref, rows):
    return jnp.concatenate([ref[pl.ds(s, rows, stride=ROW_TILES), :] for s in range(ROW_TILES)], axis=-1)


def _tile_row(ref, row):
    return ref.at[pl.ds(pl.multiple_of(row * ROW_TILES, ROW_TILES), ROW_TILES)]


def _for_each_row_pair(tm, fn):
    def body(row, carry):
        for k in range(2):
            fn(row, 2 * row + k, k)
        return carry

    lax.fori_loop(0, tm, body, 0, unroll=8)


def _dispatch_kernel(ends_ref, dest_ref, hn_ref, xs_ref, zero_ref, sem):
    tm = hn_ref.shape[0] // ROW_TILES
    tgr = zero_ref.shape[0]

    @pl.when(pl.program_id(0) == 0)
    def _():
        zero_ref[...] = jnp.zeros_like(zero_ref)

        def zero_tile(start):
            cp = pltpu.make_async_copy(zero_ref, xs_ref.at[pl.ds(pl.multiple_of(start, tgr), tgr)], sem)
            cp.start()
            cp.wait()

        n_tiles = xs_ref.shape[0] // tgr
        for e in range(N_EXPERTS):
            @pl.when(ends_ref[N_EXPERTS + e] > 0)
            def _():
                zero_tile(ends_ref[e] * ROW_TILES - tgr)

            @pl.when(n_tiles - 1 - e >= ends_ref[2 * N_EXPERTS])
            def _():
                zero_tile((n_tiles - 1 - e) * tgr)

    def scatter(row, slot, k):
        pltpu.make_async_copy(_tile_row(hn_ref, row), _tile_row(xs_ref, dest_ref[0, 0, slot]),
                              sem).start(priority=k)

    _for_each_row_pair(tm, scatter)
    for k in range(2):
        pltpu.make_async_copy(hn_ref, xs_ref.at[pl.ds(0, tm * ROW_TILES)], sem).wait()


def dispatch(ends_padded, dest, hn, *, n_rows, tm, tg):
    t = hn.shape[0] // ROW_TILES
    grid_spec = pltpu.PrefetchScalarGridSpec(
        num_scalar_prefetch=1,
        grid=(t // tm,),
        in_specs=[pl.BlockSpec((1, 1, 2 * tm), lambda i, ends: (i, 0, 0), memory_space=pltpu.SMEM),
                  pl.BlockSpec((tm * ROW_TILES, LANES), lambda i, ends: (i, 0))],
        out_specs=pl.BlockSpec(memory_space=pl.ANY),
        scratch_shapes=[pltpu.VMEM((tg * ROW_TILES, LANES), F32), pltpu.SemaphoreType.DMA],
    )
    return pl.pallas_call(
        _dispatch_kernel,
        grid_spec=grid_spec,
        out_shape=jax.ShapeDtypeStruct((n_rows * ROW_TILES, LANES), F32),
        compiler_params=_cparams("arbitrary"),
        name="moe_dispatch",
    )(ends_padded, dest, hn)


def _expert_ffn_kernel(te_ref, nu_ref, x_ref, wg_ref, wu_ref, wd_ref, o_ref):
    del te_ref
    i = pl.program_id(0)

    @pl.when(i < nu_ref[0])
    def _():
        hb = _from_tile_rows(x_ref, x_ref.shape[0] // ROW_TILES).astype(BF16)
        hid = _silu(_dot(hb, wg_ref[0])) * _dot(hb, wu_ref[0])
        _to_tile_rows(o_ref, _dot(hid.astype(BF16), wd_ref[0]))

    @pl.when(i >= nu_ref[0])
    def _():
        o_ref[...] = jnp.zeros_like(o_ref)


def expert_ffn(tile_expert, n_used, xs, wg, wu, wd, *, tg):
    r = xs.shape[0] // ROW_TILES
    dfe = wg.shape[2]
    grid_spec = pltpu.PrefetchScalarGridSpec(
        num_scalar_prefetch=2,
        grid=(r // tg,),
        in_specs=[pl.BlockSpec((tg * ROW_TILES, LANES), lambda i, te, nu: (jnp.minimum(i, nu[0] - 1), 0)),
                  pl.BlockSpec((1, D, dfe), lambda i, te, nu: (te[i], 0, 0)),
                  pl.BlockSpec((1, D, dfe), lambda i, te, nu: (te[i], 0, 0)),
                  pl.BlockSpec((1, dfe, D), lambda i, te, nu: (te[i], 0, 0))],
        out_specs=pl.BlockSpec((tg * ROW_TILES, LANES), lambda i, te, nu: (i, 0)),
    )
    return pl.pallas_call(
        _expert_ffn_kernel,
        grid_spec=grid_spec,
        out_shape=jax.ShapeDtypeStruct((r * ROW_TILES, LANES), F32),
        compiler_params=_cparams("arbitrary"),
        name="moe_expert_ffn",
    )(tile_expert, n_used, xs, wg, wu, wd)


def _combine_kernel(dest_ref, dest_next_ref, x_ref, mod_ref, rw_ref, fw_ref, ys_ref, o_ref, buf_ref, sem, *, final):
    tm = x_ref.shape[0]
    i = pl.program_id(0)
    cur = lax.rem(i, 2)

    def issue(idx_ref, buf_slot):
        def gather(row, slot, k):
            pltpu.make_async_copy(_tile_row(ys_ref, idx_ref[0, 0, slot]), _tile_row(buf_ref.at[buf_slot, k], row),
                                  sem.at[buf_slot]).start(priority=k)

        _for_each_row_pair(tm, gather)

    @pl.when(i == 0)
    def _():
        issue(dest_ref, 0)

    @pl.when(i + 1 < pl.num_programs(0))
    def _():
        issue(dest_next_ref, 1 - cur)

    for k in range(2):
        pltpu.make_async_copy(ys_ref.at[pl.ds(0, tm * ROW_TILES)], buf_ref.at[cur, k], sem.at[cur]).wait()
    rw = rw_ref[...]
    y = (rw[:, 0:1] * _from_tile_rows(buf_ref.at[cur, 0], tm)
         + rw[:, 1:2] * _from_tile_rows(buf_ref.at[cur, 1], tm))
    out = x_ref[...] + mod_ref[0, 5:6, :] * y
    if final:
        out = out * lax.rsqrt(jnp.mean(out * out, axis=-1, keepdims=True) + EPS) * fw_ref[...]
    o_ref[...] = out


def combine(dest, x, mod, rw, final_w, ys, *, seq, tm, final):
    t = x.shape[0]
    per_b = seq // tm
    last = t // tm - 1
    return pl.pallas_call(
        functools.partial(_combine_kernel, final=final),
        grid=(t // tm,),
        in_specs=[pl.BlockSpec((1, 1, 2 * tm), lambda i: (i, 0, 0), memory_space=pltpu.SMEM),
                  pl.BlockSpec((1, 1, 2 * tm), lambda i: (jnp.minimum(i + 1, last), 0, 0), memory_space=pltpu.SMEM),
                  pl.BlockSpec((tm, D), lambda i: (i, 0)),
                  pl.BlockSpec((1, 8, D), lambda i: (i // per_b, 0, 0)),
                  pl.BlockSpec((tm, LANES), lambda i: (i, 0)),
                  pl.BlockSpec((1, D), lambda i: (0, 0)),
                  pl.BlockSpec(memory_space=pl.ANY)],
        out_specs=pl.BlockSpec((tm, D), lambda i: (i, 0)),
        out_shape=jax.ShapeDtypeStruct((t, D), F32),
        scratch_shapes=[pltpu.VMEM((2, 2, tm * ROW_TILES, LANES), F32), pltpu.SemaphoreType.DMA((2,))],
        compiler_params=_cparams("arbitrary"),
        name="moe_combine",
    )(dest, dest, x, mod, rw, final_w, ys)


def _routing_plan(route_i, *, tm, tg):
    t = route_i.shape[0]
    e_flat = route_i[:, :2].reshape(-1)
    onehot = (e_flat[:, None] == jnp.arange(N_EXPERTS, dtype=jnp.int32)[None, :]).astype(jnp.int32)
    csum = jnp.cumsum(onehot, axis=0)
    counts = csum[-1]
    padded = ((counts + tg - 1) // tg) * tg
    ends = jnp.cumsum(padded)
    starts = ends - padded
    dest = jnp.sum(onehot * (starts[None, :] + csum - onehot), axis=1)
    n_tiles = (2 * t) // tg + N_EXPERTS
    tile_start = jnp.arange(n_tiles, dtype=jnp.int32) * tg
    tile_expert = jnp.sum((tile_start[:, None] >= ends[None, :]).astype(jnp.int32), axis=1)
    n_used = (ends[-1] // tg).astype(jnp.int32)
    last_expert = jnp.take(tile_expert, jnp.maximum(n_used - 1, 0))
    tile_expert = jnp.where(tile_start < ends[-1], tile_expert, last_expert).astype(jnp.int32)
    ends_padded = jnp.concatenate([ends, padded, n_used.reshape(1)]).astype(jnp.int32)
    return (dest.reshape(t // tm, 1, 2 * tm).astype(jnp.int32), tile_expert, n_used.reshape(1), ends_padded,
            n_tiles * tg)


def moe(x, y, w_out, mod, norm_w, wr, wg, wu, wd, final_w, *, seq, tm, tg, final):
    x1, hn, rw, ri = router(x, y, w_out, mod, norm_w, wr, seq=seq, tm=tm)
    dest, tile_expert, n_used, ends_padded, n_rows = _routing_plan(ri, tm=tm, tg=tg)
    xs = dispatch(ends_padded, dest, hn, n_rows=n_rows, tm=tm, tg=tg)
    ys = expert_ffn(tile_expert, n_used, xs, wg, wu, wd, tg=tg)
    return combine(dest, x1, mod, rw, final_w, ys, seq=seq, tm=tm, final=final)


def _pad_cols(w, n):
    return jnp.pad(w, ((0, 0), (0, n - w.shape[1])))


def kernel(x, c, ada_w, ada_b, norm1_w, norm2_w, final_norm_w, m_w_in, m_i_bias, m_f_bias, m_conv_w, m_conv_b, m_norm_w, m_w_out, h_w_in, h_lb_logits, h_norm_w, h_w_out, ffn_w_gate, ffn_w_up, ffn_w_down, moe_router, moe_w_gate, moe_w_up, moe_w_down):
    bsz, seq, _ = x.shape
    t = bsz * seq
    tm = min(512, seq)
    xt = x.reshape(t, D)

    mod_all = ada_modulation(c, ada_w, ada_b)
    mod_all = jnp.pad(mod_all.reshape(DEPTH, bsz, 6, D), ((0, 0), (0, 0), (0, 2), (0, 0)))

    gam = jax.nn.softmax(h_lb_logits.astype(F32), axis=0)
    lower_bounds = jnp.cumsum(gam, axis=0) - gam[:1]

    n_main = 2 * N_HEADS * M_DQK + 2 * N_HEADS * HEAD_DV
    tm_in = min(1024, seq)
    final_w = final_norm_w.reshape(1, D)
    for layer in range(DEPTH):
        j = layer // 2
        mod = mod_all[layer]
        n1 = norm1_w[layer].reshape(1, D)
        n2 = norm2_w[layer].reshape(1, D)
        if layer % 2 == 0:
            w_in = m_w_in[j]
            w_gates = jnp.concatenate([_pad_cols(w_in[:, n_main:n_main + N_HEADS], LANES),
                                       _pad_cols(w_in[:, n_main + N_HEADS:], LANES)], axis=1).astype(BF16)
            proj, gates = in_proj(xt, mod, n1, w_in[:, :n_main].astype(BF16), w_gates,
                                  seq=seq, tm=tm_in, tn=n_main // 2)
            gate_bias = jnp.concatenate([_pad_cols(m_i_bias[j].reshape(1, -1), LANES),
                                         _pad_cols(m_f_bias[j].reshape(1, -1), LANES)], axis=1)
            y = mlstm_mix(proj, gates, m_conv_w[j], m_conv_b[j].reshape(1, -1), gate_bias,
                          m_norm_w[j].reshape(1, -1), bsz=bsz, seq=seq)
            xt = ffn(xt, y, m_w_out[j].astype(BF16), mod, n2, ffn_w_gate[j].astype(BF16),
                     ffn_w_up[j].astype(BF16), ffn_w_down[j].astype(BF16), seq=seq, tm=tm, tf=1408)
        else:
            w_in = h_w_in[j].astype(BF16)
            (proj,) = in_proj(xt, mod, n1, w_in, None, seq=seq, tm=tm_in, tn=w_in.shape[1] // 2)
            y = hgrn_mix(proj, lower_bounds[j].reshape(1, D), h_norm_w[j].reshape(1, D), bsz=bsz, seq=seq)
            wr = _pad_cols(moe_router[j], LANES)
            wr_hi = wr.astype(BF16)
            wr_split = jnp.concatenate([wr_hi, (wr - wr_hi.astype(F32)).astype(BF16)], axis=1)
            xt = moe(xt, y, h_w_out[j].astype(BF16), mod, n2, wr_split,
                     moe_w_gate[j].astype(BF16), moe_w_up[j].astype(BF16), moe_w_down[j].astype(BF16), final_w,
                     seq=seq, tm=tm, tg=tm, final=layer == DEPTH - 1)
    return xt.reshape(bsz, seq, D)
```

```python
import functools

import jax
import jax.numpy as jnp
from jax import lax
from jax.experimental import pallas as pl
from jax.experimental.pallas import tpu as pltpu

F32 = jnp.float32
BF16 = jnp.bfloat16

DEPTH = 4
D = 1024
N_HEADS = 8
HEAD_DV = 128
M_DQK = 64
CONV_K = 4
IGATE_CAP = 15.0
N_EXPERTS = 8
EPS = 1e-6
NEG = -1e30
F_FLOOR = 1e-30
LANES = 128
CHUNK = 128
SUB = 16
VMEM_LIMIT = 56 * 1024 * 1024


def _cparams(*sem):
    return pltpu.CompilerParams(dimension_semantics=sem, vmem_limit_bytes=VMEM_LIMIT)


def _sigmoid(x):
    return 1.0 / (1.0 + jnp.exp(-x))


def _silu(x):
    return x * _sigmoid(x)


def _dot(a, b):
    return jnp.dot(a, b, preferred_element_type=F32)


def _dot_nt(a, b):
    return lax.dot_general(a, b, (((1,), (1,)), ((), ())), preferred_element_type=F32)


def _dot_f32(a, b):
    return jnp.dot(a, b, preferred_element_type=F32, precision=lax.Precision.HIGHEST)


def _cumsum_rows(x):
    n = x.shape[0]
    tri = (lax.broadcasted_iota(jnp.int32, (n, n), 1) <= lax.broadcasted_iota(jnp.int32, (n, n), 0)).astype(BF16)
    x1 = x.astype(BF16)
    r1 = x - x1.astype(F32)
    x2 = r1.astype(BF16)
    x3 = (r1 - x2.astype(F32)).astype(BF16)
    return _dot(tri, x1) + _dot(tri, x2) + _dot(tri, x3)


def _norm_mod(x, norm_w, scale, shift):
    y = x * lax.rsqrt(jnp.mean(x * x, axis=-1, keepdims=True) + EPS)
    return (y * norm_w) * (1.0 + scale) + shift


def _ada_kernel(c_ref, w_ref, b_ref, o_ref):
    c = c_ref[...]
    o_ref[0] = _dot_f32(_silu(c), w_ref[0]) + b_ref[0]


def ada_modulation(c, ada_w, ada_b):
    bsz = c.shape[0]
    tn = 1536
    return pl.pallas_call(
        _ada_kernel,
        grid=(DEPTH, 6 * D // tn),
        in_specs=[pl.BlockSpec((bsz, D), lambda l, j: (0, 0)),
                  pl.BlockSpec((1, D, tn), lambda l, j: (l, 0, j)),
                  pl.BlockSpec((1, 1, tn), lambda l, j: (l, 0, j))],
        out_specs=pl.BlockSpec((1, bsz, tn), lambda l, j: (l, 0, j)),
        out_shape=jax.ShapeDtypeStruct((DEPTH, bsz, 6 * D), F32),
        compiler_params=_cparams("arbitrary", "arbitrary"),
        name="ada_modulation",
    )(c, ada_w, ada_b.reshape(DEPTH, 1, 6 * D))


def _in_proj_kernel(x_ref, mod_ref, nw_ref, w_ref, *rest, has_extra):
    if has_extra:
        wx_ref, o_ref, ox_ref, hn_ref = rest
    else:
        o_ref, hn_ref = rest

    @pl.when(pl.program_id(1) == 0)
    def _():
        hn = _norm_mod(x_ref[...], nw_ref[...], mod_ref[0, 1:2, :], mod_ref[0, 0:1, :])
        hn_ref[...] = hn.astype(BF16)
        if has_extra:
            ox_ref[...] = _dot(hn_ref[...], wx_ref[...])

    o_ref[...] = _dot(hn_ref[...], w_ref[0])


def in_proj(x, mod, norm_w, w, w_extra, *, layer, n, seq, tm, tn):
    t = x.shape[0]
    per_b = seq // tm
    has_extra = w_extra is not None
    in_specs = [pl.BlockSpec((tm, D), lambda i, j: (i, 0)),
                pl.BlockSpec((1, 8, D), lambda i, j: (i // per_b, 0, 0)),
                pl.BlockSpec((1, D), lambda i, j: (0, 0)),
                pl.BlockSpec((1, D, tn), lambda i, j: (layer, 0, j))]
    out_specs = [pl.BlockSpec((tm, tn), lambda i, j: (i, j))]
    out_shape = [jax.ShapeDtypeStruct((t, n), F32)]
    args = [x, mod, norm_w, w]
    if has_extra:
        nx = w_extra.shape[1]
        in_specs.append(pl.BlockSpec((D, nx), lambda i, j: (0, 0)))
        out_specs.append(pl.BlockSpec((tm, nx), lambda i, j: (i, 0)))
        out_shape.append(jax.ShapeDtypeStruct((t, nx), F32))
        args.append(w_extra)
    return pl.pallas_call(
        functools.partial(_in_proj_kernel, has_extra=has_extra),
        grid=(t // tm, n // tn),
        in_specs=in_specs,
        out_specs=out_specs,
        out_shape=out_shape,
        scratch_shapes=[pltpu.VMEM((tm, D), BF16)],
        compiler_params=_cparams("arbitrary", "arbitrary"),
        name="in_proj",
    )(*args)


def _interleave(chunks):
    live = list(chunks)
    while live:
        for gen in list(live):
            if next(gen, StopIteration) is StopIteration:
                live.remove(gen)


def _mlstm_kernel(qk_ref, v_ref, o_ref, g_ref, cw_ref, cb_ref, gb_ref, nw_ref, out_ref,
                  tail_ref, c_ref, m_ref):
    @pl.when(pl.program_id(1) == 0)
    def _():
        tail_ref[...] = jnp.zeros_like(tail_ref)
        c_ref[...] = jnp.zeros_like(c_ref)
        m_ref[...] = jnp.zeros_like(m_ref)

    _interleave([_mlstm_chunk(qk_ref.at[i], v_ref.at[i], o_ref.at[i], g_ref.at[i], cw_ref, cb_ref, gb_ref, nw_ref,
                              out_ref.at[i], tail_ref.at[i], c_ref.at[i], m_ref.at[i])
                 for i in range(qk_ref.shape[0])])


def _mlstm_chunk(qk_ref, v_ref, o_ref, g_ref, cw_ref, cb_ref, gb_ref, nw_ref, out_ref, tail_ref, c_ref, m_ref):
    L = CHUNK

    raw = qk_ref[...]
    xx = jnp.concatenate([tail_ref[...], raw], axis=0)
    tail_ref[...] = raw[L - 8:, :]
    y = cb_ref[...] + cw_ref[CONV_K - 1:CONV_K, :] * raw
    for j in range(CONV_K - 1):
        y = y + cw_ref[j:j + 1, :] * pltpu.roll(xx, CONV_K - 1 - j, axis=0)[8:, :]
    qk = _silu(y)
    half = N_HEADS * M_DQK
    q_all = qk[:, :half]
    k_all = qk[:, half:] * (M_DQK ** -0.5)

    g = g_ref[...] + gb_ref[...]
    li = IGATE_CAP * jnp.tanh(g[:, :LANES] * (1.0 / IGATE_CAP))
    gf = g[:, LANES:]
    b = _cumsum_rows(jnp.minimum(gf, 0.0) - jnp.log1p(jnp.exp(-jnp.abs(gf))))
    tril = lax.broadcasted_iota(jnp.int32, (L, L), 1) <= lax.broadcasted_iota(jnp.int32, (L, L), 0)

    a = li - b
    amax8 = a.reshape(L // SUBLANES, SUBLANES, LANES)
    sub = lax.broadcasted_iota(jnp.int32, (1, SUBLANES, LANES), 1)
    for sh in (1, 2, 4):
        amax8 = jnp.where(sub >= sh, jnp.maximum(amax8, pltpu.roll(amax8, sh, axis=1)), amax8)
    groups = [amax8[0]]
    for v in range(1, L // SUBLANES):
        carry = jnp.broadcast_to(groups[-1][SUBLANES - 1:SUBLANES, :], (SUBLANES, LANES))
        groups.append(jnp.maximum(amax8[v], carry))
    amax = jnp.concatenate(groups, axis=0)
    m_prev = m_ref[...]
    mx = jnp.maximum(m_prev, amax)
    mx_last = mx[L - 1:L, :]
    m_ref[...] = b[L - 1:L, :] + mx_last
    decay = jnp.exp(m_prev - mx_last)
    a_r = a.T[0:N_HEADS, :]
    wk_r = jnp.exp(a - mx_last).T[0:N_HEADS, :]

    lane = lax.broadcasted_iota(jnp.int32, (L, LANES), 1)
    ones_b = jnp.ones((L, LANES), BF16)
    kt_pairs = [k_all[:, p * LANES:(p + 1) * LANES].T for p in range(N_HEADS // 2)]
    qb_l, kb_l, ve_l, kw_l = [], [], [], []
    for h in range(N_HEADS):
        p, hf = h // 2, h % 2
        in_head = (lane >= hf * M_DQK) & (lane < (hf + 1) * M_DQK)
        qb_l.append(jnp.where(in_head, q_all[:, p * LANES:(p + 1) * LANES], 0.0).astype(BF16))
        kb_l.append(k_all[:, p * LANES:(p + 1) * LANES].astype(BF16))
        ve_l.append(jnp.concatenate([v_ref[:, h * HEAD_DV:(h + 1) * HEAD_DV].astype(BF16), ones_b], axis=-1))
        kw_l.append((kt_pairs[p] * wk_r[h:h + 1, :]).astype(BF16))
    qb = jnp.stack(qb_l)
    kb = jnp.stack(kb_l)
    v_ext = jnp.stack(ve_l)
    c_st = c_ref[...]

    qk_raw = jnp.einsum('hqd,hkd->hqk', qb, kb, preferred_element_type=F32)
    inter = jnp.einsum('hqd,hde->hqe', qb, c_st.astype(BF16), preferred_element_type=F32)
    for h in range(N_HEADS):
        c_ref[h] = decay[:, h:h + 1] * c_st[h] + _dot(kw_l[h], v_ext[h])
    yield

    def lane_bcast(cols):
        return jnp.stack([jnp.broadcast_to(cols[:, h:h + 1], (L, LANES)) for h in range(N_HEADS)])

    mx_b = lane_bcast(mx)
    m_prev_b = jnp.stack([jnp.broadcast_to(m_prev[:, h:h + 1], (1, LANES)) for h in range(N_HEADS)])
    w_inter = jnp.exp(m_prev_b - mx_b)
    pmat = jnp.exp(jnp.where(tril[None], a_r[:, None, :] - mx_b, NEG))
    s = qk_raw * pmat
    nd = (jnp.concatenate([w_inter, w_inter], axis=-1) * inter
          + jnp.einsum('hqk,hke->hqe', s.astype(BF16), v_ext, preferred_element_type=F32))
    floor = jnp.exp(-lane_bcast(b + mx))
    yield

    hh = nd[:, :, :HEAD_DV] / jnp.maximum(jnp.abs(nd[:, :, HEAD_DV:]), floor)
    ms = _dot((hh * hh).astype(BF16).reshape(N_HEADS * L, HEAD_DV), ones_b).reshape(hh.shape) * (1.0 / HEAD_DV)
    yield

    hn = hh * lax.rsqrt(ms + EPS)
    merged = jnp.concatenate([hn[h] for h in range(N_HEADS)], axis=-1) * nw_ref[...]
    out_ref[...] = (merged * _sigmoid(o_ref[...])).astype(out_ref.dtype)


def mlstm_mix(proj, gates, conv_w, conv_b, gate_bias, norm_w, *, bsz, seq):
    nblk = seq // CHUNK
    dv = N_HEADS * HEAD_DV
    ns = _seqs_per_step(bsz)
    proj = proj.reshape(bsz, seq, proj.shape[-1])
    gates = gates.reshape(bsz, seq, 2 * LANES)
    out = pl.pallas_call(
        _mlstm_kernel,
        grid=(bsz // ns, nblk),
        in_specs=[pl.BlockSpec((ns, CHUNK, D), lambda b, s: (b, s, 0)),
                  pl.BlockSpec((ns, CHUNK, dv), lambda b, s: (b, s, 1)),
                  pl.BlockSpec((ns, CHUNK, dv), lambda b, s: (b, s, 2)),
                  pl.BlockSpec((ns, CHUNK, 2 * LANES), lambda b, s: (b, s, 0)),
                  pl.BlockSpec((CONV_K, D), lambda b, s: (0, 0)),
                  pl.BlockSpec((1, D), lambda b, s: (0, 0)),
                  pl.BlockSpec((1, 2 * LANES), lambda b, s: (0, 0)),
                  pl.BlockSpec((1, dv), lambda b, s: (0, 0))],
        out_specs=pl.BlockSpec((ns, CHUNK, dv), lambda b, s: (b, s, 0)),
        out_shape=jax.ShapeDtypeStruct((bsz, seq, dv), BF16),
        scratch_shapes=[pltpu.VMEM((ns, 8, D), F32),
                        pltpu.VMEM((ns, N_HEADS, LANES, 2 * HEAD_DV), F32),
                        pltpu.VMEM((ns, 1, LANES), F32)],
        compiler_params=_cparams("arbitrary", "arbitrary"),
        name="mlstm_mix",
    )(proj, proj, proj, gates, conv_w, conv_b, gate_bias, norm_w)
    return out.reshape(bsz * seq, dv)


def _hgrn_kernel(q_ref, f_ref, v_ref, g_ref, lb_ref, nw_ref, out_ref, st_ref):
    @pl.when(pl.program_id(1) == 0)
    def _():
        st_ref[...] = jnp.zeros_like(st_ref)

    _interleave([_hgrn_chunk(q_ref.at[i], f_ref.at[i], v_ref.at[i], g_ref.at[i], lb_ref, nw_ref, out_ref.at[i],
                             st_ref.at[i]) for i in range(q_ref.shape[0])])


def _hgrn_chunk(q_ref, f_ref, v_ref, g_ref, lb_ref, nw_ref, out_ref, st_ref):
    L = CHUNK

    lb = lb_ref[...]
    f = f_ref[...]
    e = jnp.exp(-jnp.abs(f))
    r = 1.0 / (1.0 + e)
    er = e * r
    sig = jnp.where(f >= 0.0, r, er)
    nsig = jnp.where(f >= 0.0, er, r)
    dec_all = jnp.maximum(lb + (1.0 - lb) * sig, F_FLOOR)
    k_all = (1.0 - lb) * nsig
    q_all = _silu(q_ref[...])
    row = lax.broadcasted_iota(jnp.int32, (L, L), 0)
    col = lax.broadcasted_iota(jnp.int32, (L, L), 1)
    xor = row ^ col
    eye = row == col
    ones_b = jnp.ones((LANES, LANES), BF16)
    sub8 = lax.broadcasted_iota(jnp.int32, (1, 8, LANES), 1)
    H = N_HEADS
    G = L // SUBLANES

    def heads(x):
        return jnp.stack([x[:, h * HEAD_DV:(h + 1) * HEAD_DV] for h in range(H)])

    def bmm_nt(x, y):
        return jnp.einsum('hqd,hkd->hqk', x.astype(BF16), y.astype(BF16), preferred_element_type=F32)

    def rot8(x3, d):
        return pltpu.roll(x3, d % 8, axis=1)

    def level_mask(b):
        return ((xor >= b) & (xor < 2 * b))[None]

    q, k, dec = heads(q_all), heads(k_all), heads(dec_all)
    vb = heads(v_ref[...]).astype(BF16)
    st = st_ref[...]

    diag = _dot((q * k).astype(BF16).reshape(H * L, HEAD_DV), ones_b).reshape(H, L, L)

    n8 = H * G
    e8 = dec.reshape(n8, 8, LANES)
    q8 = q.reshape(n8, 8, LANES)
    k8 = k.reshape(n8, 8, LANES)
    ew = [e8]
    for j in range(1, 4):
        ew.append(ew[-1] * rot8(e8, j))
    eu = [rot8(e8, -1)]
    for j in range(1, 3):
        eu.append(eu[-1] * rot8(e8, -1 - j))
    pos2, pos4 = sub8 % 2, sub8 % 4
    pre2 = jnp.where(pos2 == 0, ew[0], ew[1])
    suf2 = jnp.where(pos2 == 0, eu[0], 1.0)
    pre4 = jnp.where(pos4 == 0, ew[0], jnp.where(pos4 == 1, ew[1], jnp.where(pos4 == 2, ew[2], ew[3])))
    suf4 = jnp.where(pos4 == 0, eu[2], jnp.where(pos4 == 1, eu[1], jnp.where(pos4 == 2, eu[0], 1.0)))
    small = [
        (4, jnp.where(sub8 >= 4, pre4, 0.0), jnp.where(sub8 < 4, suf4, 0.0)),
        (2, jnp.where(pos4 >= 2, pre2, 0.0), jnp.where(pos4 < 2, suf2, 0.0)),
        (1, jnp.where(pos2 == 1, e8, 0.0), jnp.where(pos2 == 0, 1.0, 0.0)),
    ]
    levels = [(b, bmm_nt((q8 * fq).reshape(H, L, LANES), (k8 * fk).reshape(H, L, LANES))) for b, fq, fk in small]

    pre = pre4 * jnp.where(sub8 >= 4, pre4[:, 3:4, :], 1.0)
    suf = suf4 * jnp.where(sub8 < 4, pre4[:, 7:8, :], 1.0)
    tot = pre[:, 7:8, :]
    q6 = q.reshape(H, G, 1, 1, SUBLANES, LANES)
    k6 = k.reshape(H, G, 1, 1, SUBLANES, LANES)
    m = 1
    while m < G:
        nb = G // (2 * m)
        shape = (H, nb, 2, m, SUBLANES, LANES)
        pre, suf, q6, k6 = (x.reshape(shape) for x in (pre, suf, q6, k6))
        tot = tot.reshape(H, nb, 2, 1, 1, LANES)
        zeros = jnp.zeros((H, nb, 1, m, SUBLANES, LANES), F32)
        q_l = jnp.concatenate([zeros, q6[:, :, 1:] * pre[:, :, 1:]], axis=2)
        k_l = jnp.concatenate([k6[:, :, :1] * suf[:, :, :1], zeros], axis=2)
        levels.append((SUBLANES * m, bmm_nt(q_l.reshape(H, L, LANES), k_l.reshape(H, L, LANES))))
        pre = jnp.concatenate([pre[:, :, :1], pre[:, :, 1:] * tot[:, :, :1]], axis=2)
        suf = jnp.concatenate([suf[:, :, :1] * tot[:, :, 1:], suf[:, :, 1:]], axis=2)
        tot = tot[:, :, :1] * tot[:, :, 1:]
        m *= 2
    pre = pre.reshape(H, L, LANES)
    suf = suf.reshape(H, L, LANES)
    tot = tot.reshape(H, 1, LANES)

    o_inter = bmm_nt(q * pre, st)
    k_dec = (k * suf).astype(BF16)
    for h in range(H):
        st_ref[h] = st[h] * tot[h] + lax.dot_general(
            vb[h], k_dec[h], (((0,), (0,)), ((), ())), preferred_element_type=F32)
    yield

    a = jnp.where(eye[None], diag, 0.0)
    for b, a_l in levels:
        a = jnp.where(level_mask(b), a_l, a)
    o = o_inter + jnp.einsum('hqk,hkv->hqv', a.astype(BF16), vb, preferred_element_type=F32)
    yield

    ms = _dot((o * o).astype(BF16).reshape(H * L, HEAD_DV), ones_b).reshape(o.shape) * (1.0 / HEAD_DV)
    yield

    on = o * lax.rsqrt(ms + EPS)
    merged = jnp.concatenate([on[h] for h in range(H)], axis=-1) * nw_ref[...]
    out_ref[...] = (merged * _silu(g_ref[...])).astype(out_ref.dtype)


def _seqs_per_step(bsz):
    return 2 if bsz % 2 == 0 else 1


def hgrn_mix(proj, lb, norm_w, *, bsz, seq):
    nblk = seq // CHUNK
    ns = _seqs_per_step(bsz)
    proj = proj.reshape(bsz, seq, proj.shape[-1])
    out = pl.pallas_call(
        _hgrn_kernel,
        grid=(bsz // ns, nblk),
        in_specs=[pl.BlockSpec((ns, CHUNK, D), lambda b, s: (b, s, 0)),
                  pl.BlockSpec((ns, CHUNK, D), lambda b, s: (b, s, 1)),
                  pl.BlockSpec((ns, CHUNK, D), lambda b, s: (b, s, 2)),
                  pl.BlockSpec((ns, CHUNK, D), lambda b, s: (b, s, 3)),
                  pl.BlockSpec((1, D), lambda b, s: (0, 0)),
                  pl.BlockSpec((1, D), lambda b, s: (0, 0))],
        out_specs=pl.BlockSpec((ns, CHUNK, D), lambda b, s: (b, s, 0)),
        out_shape=jax.ShapeDtypeStruct((bsz, seq, D), BF16),
        scratch_shapes=[pltpu.VMEM((ns, N_HEADS, HEAD_DV, HEAD_DV), F32)],
        compiler_params=_cparams("arbitrary", "arbitrary"),
        name="hgrn_mix",
    )(proj, proj, proj, proj, lb, norm_w)
    return out.reshape(bsz * seq, D)


def _mixer_residual(x_ref, y_ref, wo_ref, mod_ref):
    return x_ref[...] + mod_ref[0, 2:3, :] * _dot(y_ref[...], wo_ref[0])


def _ffn_kernel(x_ref, y_ref, wo_ref, mod_ref, nw_ref, wg_ref, wu_ref, wd_ref, o_ref, *, tf):
    x1 = _mixer_residual(x_ref, y_ref, wo_ref, mod_ref)
    hb = _norm_mod(x1, nw_ref[...], mod_ref[0, 4:5, :], mod_ref[0, 3:4, :]).astype(BF16)
    acc = None
    for f0 in range(0, wg_ref.shape[2], tf):
        hid = _silu(_dot(hb, wg_ref[0, :, f0:f0 + tf])) * _dot(hb, wu_ref[0, :, f0:f0 + tf])
        part = _dot(hid.astype(BF16), wd_ref[0, f0:f0 + tf, :])
        acc = part if acc is None else acc + part
    o_ref[...] = x1 + mod_ref[0, 5:6, :] * acc


def ffn(x, y, w_out, mod, norm_w, wg, wu, wd, *, layer, seq, tm, tf):
    t = x.shape[0]
    dff = wg.shape[2]
    per_b = seq // tm
    resident = dict(pipeline_mode=pl.Buffered(1))
    return pl.pallas_call(
        functools.partial(_ffn_kernel, tf=tf),
        grid=(t // tm,),
        in_specs=[pl.BlockSpec((tm, D), lambda i: (i, 0)),
                  pl.BlockSpec((tm, D), lambda i: (i, 0)),
                  pl.BlockSpec((1, D, D), lambda i: (layer, 0, 0), **resident),
                  pl.BlockSpec((1, 8, D), lambda i: (i // per_b, 0, 0)),
                  pl.BlockSpec((1, D), lambda i: (0, 0)),
                  pl.BlockSpec((1, D, dff), lambda i: (layer, 0, 0), **resident),
                  pl.BlockSpec((1, D, dff), lambda i: (layer, 0, 0), **resident),
                  pl.BlockSpec((1, dff, D), lambda i: (layer, 0, 0), **resident)],
        out_specs=pl.BlockSpec((tm, D), lambda i: (i, 0)),
        out_shape=jax.ShapeDtypeStruct((t, D), F32),
        compiler_params=_cparams("arbitrary"),
        name="ffn",
    )(x, y, w_out, mod, norm_w, wg, wu, wd)


def _router_kernel(x_ref, y_ref, wo_ref, mod_ref, nw_ref, wr_ref, x1_ref, hn_ref, rw_ref, ri_ref):
    x1 = _mixer_residual(x_ref, y_ref, wo_ref, mod_ref)
    x1_ref[...] = x1
    hn = _norm_mod(x1, nw_ref[...], mod_ref[0, 4:5, :], mod_ref[0, 3:4, :])
    _to_tile_rows(hn_ref, hn)
    h_hi = hn.astype(BF16)
    h_lo = (hn - h_hi.astype(F32)).astype(BF16)
    hw = _dot(h_hi, wr_ref[...])
    logits = hw[:, :LANES] + hw[:, LANES:] + _dot(h_lo, wr_ref[:, :LANES])
    lane = lax.broadcasted_iota(jnp.int32, logits.shape, 1)
    lg = jnp.where(lane < N_EXPERTS, logits, NEG)
    m1 = jnp.max(lg, axis=-1, keepdims=True)
    i1 = jnp.min(jnp.where(lg == m1, lane, LANES), axis=-1, keepdims=True)
    lg2 = jnp.where(lane == i1, NEG, lg)
    m2 = jnp.max(lg2, axis=-1, keepdims=True)
    i2 = jnp.min(jnp.where(lg2 == m2, lane, LANES), axis=-1, keepdims=True)
    e2 = jnp.exp(m2 - m1)
    w1 = 1.0 / (1.0 + e2)
    rw_ref[...] = jnp.where(lane == 0, w1, jnp.where(lane == 1, e2 * w1, 0.0))
    ri_ref[...] = jnp.where(lane == 0, i1, jnp.where(lane == 1, i2, 0))


def router(x, y, w_out, mod, norm_w, wr, *, layer, seq, tm):
    t = x.shape[0]
    per_b = seq // tm
    return pl.pallas_call(
        _router_kernel,
        grid=(t // tm,),
        in_specs=[pl.BlockSpec((tm, D), lambda i: (i, 0)),
                  pl.BlockSpec((tm, D), lambda i: (i, 0)),
                  pl.BlockSpec((1, D, D), lambda i: (layer, 0, 0), pipeline_mode=pl.Buffered(1)),
                  pl.BlockSpec((1, 8, D), lambda i: (i // per_b, 0, 0)),
                  pl.BlockSpec((1, D), lambda i: (0, 0)),
                  pl.BlockSpec((D, 2 * LANES), lambda i: (0, 0))],
        out_specs=[pl.BlockSpec((tm, D), lambda i: (i, 0)),
                   pl.BlockSpec((tm * ROW_TILES, LANES), lambda i: (i, 0)),
                   pl.BlockSpec((tm, LANES), lambda i: (i, 0)),
                   pl.BlockSpec((tm, LANES), lambda i: (i, 0))],
        out_shape=[jax.ShapeDtypeStruct((t, D), F32),
                   jax.ShapeDtypeStruct((t * ROW_TILES, LANES), F32),
                   jax.ShapeDtypeStruct((t, LANES), F32),
                   jax.ShapeDtypeStruct((t, LANES), jnp.int32)],
        compiler_params=_cparams("arbitrary"),
        name="router",
    )(x, y, w_out, mod, norm_w, wr)


SUBLANES = 8
ROW_TILES = D // LANES


def _to_tile_rows(ref, val):
    rows = val.shape[0]
    for s in range(ROW_TILES):
        ref[pl.ds(s, rows, stride=ROW_TILES), :] = val[:, s * LANES:(s + 1) * LANES]


def _from_tile_rows(ref, rows):
    return jnp.concatenate([ref[pl.ds(s, rows, stride=ROW_TILES), :] for s in range(ROW_TILES)], axis=-1)


def _tile_row(ref, row):
    return ref.at[pl.ds(pl.multiple_of(row * ROW_TILES, ROW_TILES), ROW_TILES)]


def _for_each_row_pair(tm, fn):
    def body(row, carry):
        for k in range(2):
            fn(row, 2 * row + k, k)
        return carry

    lax.fori_loop(0, tm, body, 0, unroll=8)


def _dispatch_kernel(ends_ref, dest_ref, hn_ref, xs_ref, zero_ref, sem):
    tm = hn_ref.shape[0] // ROW_TILES
    tgr = zero_ref.shape[0]

    @pl.when(pl.program_id(0) == 0)
    def _():
        zero_ref[...] = jnp.zeros_like(zero_ref)

        def zero_tile(start):
            cp = pltpu.make_async_copy(zero_ref, xs_ref.at[pl.ds(pl.multiple_of(start, tgr), tgr)], sem)
            cp.start()
            cp.wait()

        n_tiles = xs_ref.shape[0] // tgr
        for e in range(N_EXPERTS):
            @pl.when(ends_ref[N_EXPERTS + e] > 0)
            def _():
                zero_tile(ends_ref[e] * ROW_TILES - tgr)

            @pl.when(n_tiles - 1 - e >= ends_ref[2 * N_EXPERTS])
            def _():
                zero_tile((n_tiles - 1 - e) * tgr)

    def scatter(row, slot, k):
        pltpu.make_async_copy(_tile_row(hn_ref, row), _tile_row(xs_ref, dest_ref[0, 0, slot]),
                              sem).start(priority=k)

    _for_each_row_pair(tm, scatter)
    for k in range(2):
        pltpu.make_async_copy(hn_ref, xs_ref.at[pl.ds(0, tm * ROW_TILES)], sem).wait()


def dispatch(ends_padded, dest, hn, *, n_rows, tm, tg):
    t = hn.shape[0] // ROW_TILES
    grid_spec = pltpu.PrefetchScalarGridSpec(
        num_scalar_prefetch=1,
        grid=(t // tm,),
        in_specs=[pl.BlockSpec((1, 1, 2 * tm), lambda i, ends: (i, 0, 0), memory_space=pltpu.SMEM),
                  pl.BlockSpec((tm * ROW_TILES, LANES), lambda i, ends: (i, 0))],
        out_specs=pl.BlockSpec(memory_space=pl.ANY),
        scratch_shapes=[pltpu.VMEM((tg * ROW_TILES, LANES), F32), pltpu.SemaphoreType.DMA],
    )
    return pl.pallas_call(
        _dispatch_kernel,
        grid_spec=grid_spec,
        out_shape=jax.ShapeDtypeStruct((n_rows * ROW_TILES, LANES), F32),
        compiler_params=_cparams("arbitrary"),
        name="moe_dispatch",
    )(ends_padded, dest, hn)


def _expert_ffn_kernel(te_ref, nu_ref, x_ref, wg_ref, wu_ref, wd_ref, o_ref):
    del te_ref
    i = pl.program_id(0)

    @pl.when(i < nu_ref[0])
    def _():
        hb = _from_tile_rows(x_ref, x_ref.shape[0] // ROW_TILES).astype(BF16)
        hid = _silu(_dot(hb, wg_ref[0, 0])) * _dot(hb, wu_ref[0, 0])
        _to_tile_rows(o_ref, _dot(hid.astype(BF16), wd_ref[0, 0]))

    @pl.when(i >= nu_ref[0])
    def _():
        o_ref[...] = jnp.zeros_like(o_ref)


def expert_ffn(tile_expert, n_used, xs, wg, wu, wd, *, layer, tg):
    r = xs.shape[0] // ROW_TILES
    dfe = wg.shape[3]
    grid_spec = pltpu.PrefetchScalarGridSpec(
        num_scalar_prefetch=2,
        grid=(r // tg,),
        in_specs=[pl.BlockSpec((tg * ROW_TILES, LANES), lambda i, te, nu: (jnp.minimum(i, nu[0] - 1), 0)),
                  pl.BlockSpec((1, 1, D, dfe), lambda i, te, nu: (layer, te[i], 0, 0)),
                  pl.BlockSpec((1, 1, D, dfe), lambda i, te, nu: (layer, te[i], 0, 0)),
                  pl.BlockSpec((1, 1, dfe, D), lambda i, te, nu: (layer, te[i], 0, 0))],
        out_specs=pl.BlockSpec((tg * ROW_TILES, LANES), lambda i, te, nu: (i, 0)),
    )
    return pl.pallas_call(
        _expert_ffn_kernel,
        grid_spec=grid_spec,
        out_shape=jax.ShapeDtypeStruct((r * ROW_TILES, LANES), F32),
        compiler_params=_cparams("arbitrary"),
        name="moe_expert_ffn",
    )(tile_expert, n_used, xs, wg, wu, wd)


def _combine_kernel(dest_ref, dest_next_ref, x_ref, mod_ref, rw_ref, fw_ref, ys_ref, o_ref, buf_ref, sem, *, final):
    tm = x_ref.shape[0]
    i = pl.program_id(0)
    cur = lax.rem(i, 2)

    def issue(idx_ref, buf_slot):
        def gather(row, slot, k):
            pltpu.make_async_copy(_tile_row(ys_ref, idx_ref[0, 0, slot]), _tile_row(buf_ref.at[buf_slot, k], row),
                                  sem.at[buf_slot]).start(priority=k)

        _for_each_row_pair(tm, gather)

    @pl.when(i == 0)
    def _():
        issue(dest_ref, 0)

    @pl.when(i + 1 < pl.num_programs(0))
    def _():
        issue(dest_next_ref, 1 - cur)

    for k in range(2):
        pltpu.make_async_copy(ys_ref.at[pl.ds(0, tm * ROW_TILES)], buf_ref.at[cur, k], sem.at[cur]).wait()
    rw = rw_ref[...]
    y = (rw[:, 0:1] * _from_tile_rows(buf_ref.at[cur, 0], tm)
         + rw[:, 1:2] * _from_tile_rows(buf_ref.at[cur, 1], tm))
    out = x_ref[...] + mod_ref[0, 5:6, :] * y
    if final:
        out = out * lax.rsqrt(jnp.mean(out * out, axis=-1, keepdims=True) + EPS) * fw_ref[...]
    o_ref[...] = out


def combine(dest, x, mod, rw, final_w, ys, *, seq, tm, final):
    t = x.shape[0]
    per_b = seq // tm
    last = t // tm - 1
    return pl.pallas_call(
        functools.partial(_combine_kernel, final=final),
        grid=(t // tm,),
        in_specs=[pl.BlockSpec((1, 1, 2 * tm), lambda i: (i, 0, 0), memory_space=pltpu.SMEM),
                  pl.BlockSpec((1, 1, 2 * tm), lambda i: (jnp.minimum(i + 1, last), 0, 0), memory_space=pltpu.SMEM),
                  pl.BlockSpec((tm, D), lambda i: (i, 0)),
                  pl.BlockSpec((1, 8, D), lambda i: (i // per_b, 0, 0)),
                  pl.BlockSpec((tm, LANES), lambda i: (i, 0)),
                  pl.BlockSpec((1, D), lambda i: (0, 0)),
                  pl.BlockSpec(memory_space=pl.ANY)],
        out_specs=pl.BlockSpec((tm, D), lambda i: (i, 0)),
        out_shape=jax.ShapeDtypeStruct((t, D), F32),
        scratch_shapes=[pltpu.VMEM((2, 2, tm * ROW_TILES, LANES), F32), pltpu.SemaphoreType.DMA((2,))],
        compiler_params=_cparams("arbitrary"),
        name="moe_combine",
    )(dest, dest, x, mod, rw, final_w, ys)


def _routing_plan(route_i, *, tm, tg):
    t = route_i.shape[0]
    e_flat = route_i[:, :2].reshape(-1)
    onehot = (e_flat[:, None] == jnp.arange(N_EXPERTS, dtype=jnp.int32)[None, :]).astype(jnp.int32)
    csum = jnp.cumsum(onehot, axis=0)
    counts = csum[-1]
    padded = ((counts + tg - 1) // tg) * tg
    ends = jnp.cumsum(padded)
    starts = ends - padded
    dest = jnp.sum(onehot * (starts[None, :] + csum - onehot), axis=1)
    n_tiles = (2 * t) // tg + N_EXPERTS
    tile_start = jnp.arange(n_tiles, dtype=jnp.int32) * tg
    tile_expert = jnp.sum((tile_start[:, None] >= ends[None, :]).astype(jnp.int32), axis=1)
    n_used = (ends[-1] // tg).astype(jnp.int32)
    last_expert = jnp.take(tile_expert, jnp.maximum(n_used - 1, 0))
    tile_expert = jnp.where(tile_start < ends[-1], tile_expert, last_expert).astype(jnp.int32)
    ends_padded = jnp.concatenate([ends, padded, n_used.reshape(1)]).astype(jnp.int32)
    return (dest.reshape(t // tm, 1, 2 * tm).astype(jnp.int32), tile_expert, n_used.reshape(1), ends_padded,
            n_tiles * tg)


def moe(x, y, w_out, mod, norm_w, wr, wg, wu, wd, final_w, *, layer, seq, tm, tg, final):
    x1, hn, rw, ri = router(x, y, w_out, mod, norm_w, wr, layer=layer, seq=seq, tm=tm)
    dest, tile_expert, n_used, ends_padded, n_rows = _routing_plan(ri, tm=tm, tg=tg)
    xs = dispatch(ends_padded, dest, hn, n_rows=n_rows, tm=tm, tg=tg)
    ys = expert_ffn(tile_expert, n_used, xs, wg, wu, wd, layer=layer, tg=tg)
    return combine(dest, x1, mod, rw, final_w, ys, seq=seq, tm=tm, final=final)


def _pad_cols(w, n):
    return jnp.pad(w, ((0, 0), (0, n - w.shape[1])))


def kernel(x, c, ada_w, ada_b, norm1_w, norm2_w, final_norm_w, m_w_in, m_i_bias, m_f_bias, m_conv_w, m_conv_b, m_norm_w, m_w_out, h_w_in, h_lb_logits, h_norm_w, h_w_out, ffn_w_gate, ffn_w_up, ffn_w_down, moe_router, moe_w_gate, moe_w_up, moe_w_down):
    bsz, seq, _ = x.shape
    t = bsz * seq
    tm = min(512, seq)
    xt = x.reshape(t, D)

    mod_all = ada_modulation(c, ada_w, ada_b)
    mod_all = jnp.pad(mod_all.reshape(DEPTH, bsz, 6, D), ((0, 0), (0, 0), (0, 2), (0, 0)))

    gam = jax.nn.softmax(h_lb_logits.astype(F32), axis=0)
    lower_bounds = jnp.cumsum(gam, axis=0) - gam[:1]

    n_main = 2 * N_HEADS * M_DQK + 2 * N_HEADS * HEAD_DV
    tm_in = min(1024, seq)
    final_w = final_norm_w.reshape(1, D)
    m_w_in_b, h_w_in_b = m_w_in.astype(BF16), h_w_in.astype(BF16)
    m_w_out_b, h_w_out_b = m_w_out.astype(BF16), h_w_out.astype(BF16)
    ffn_g, ffn_u, ffn_d = ffn_w_gate.astype(BF16), ffn_w_up.astype(BF16), ffn_w_down.astype(BF16)
    moe_g, moe_u, moe_d = moe_w_gate.astype(BF16), moe_w_up.astype(BF16), moe_w_down.astype(BF16)
    for layer in range(DEPTH):
        j = layer // 2
        mod = mod_all[layer]
        n1 = norm1_w[layer].reshape(1, D)
        n2 = norm2_w[layer].reshape(1, D)
        if layer % 2 == 0:
            w_gates = jnp.concatenate([_pad_cols(m_w_in[j, :, n_main:n_main + N_HEADS], LANES),
                                       _pad_cols(m_w_in[j, :, n_main + N_HEADS:], LANES)], axis=1).astype(BF16)
            proj, gates = in_proj(xt, mod, n1, m_w_in_b, w_gates, layer=j, n=n_main,
                                  seq=seq, tm=tm_in, tn=n_main // 2)
            gate_bias = jnp.concatenate([_pad_cols(m_i_bias[j].reshape(1, -1), LANES),
                                         _pad_cols(m_f_bias[j].reshape(1, -1), LANES)], axis=1)
            y = mlstm_mix(proj, gates, m_conv_w[j], m_conv_b[j].reshape(1, -1), gate_bias,
                          m_norm_w[j].reshape(1, -1), bsz=bsz, seq=seq)
            xt = ffn(xt, y, m_w_out_b, mod, n2, ffn_g, ffn_u, ffn_d, layer=j, seq=seq, tm=tm, tf=1408)
        else:
            n_h = h_w_in.shape[2]
            (proj,) = in_proj(xt, mod, n1, h_w_in_b, None, layer=j, n=n_h, seq=seq, tm=tm_in, tn=n_h // 2)
            y = hgrn_mix(proj, lower_bounds[j].reshape(1, D), h_norm_w[j].reshape(1, D), bsz=bsz, seq=seq)
            wr = _pad_cols(moe_router[j], LANES)
            wr_hi = wr.astype(BF16)
            wr_split = jnp.concatenate([wr_hi, (wr - wr_hi.astype(F32)).astype(BF16)], axis=1)
            xt = moe(xt, y, h_w_out_b, mod, n2, wr_split, moe_g, moe_u, moe_d, final_w,
                     layer=j, seq=seq, tm=tm, tg=tm, final=layer == DEPTH - 1)
    return xt.reshape(bsz, seq, D)
```

```python
import functools

import jax
import jax.numpy as jnp
from jax import lax
from jax.experimental import pallas as pl
from jax.experimental.pallas import tpu as pltpu

F32 = jnp.float32
BF16 = jnp.bfloat16

DEPTH = 4
D = 1024
N_HEADS = 8
HEAD_DV = 128
M_DQK = 64
CONV_K = 4
IGATE_CAP = 15.0
N_EXPERTS = 8
EPS = 1e-6
NEG = -1e30
F_FLOOR = 1e-30
LANES = 128
CHUNK = 128
SUB = 16
VMEM_LIMIT = 56 * 1024 * 1024


def _cparams(*sem):
    return pltpu.CompilerParams(dimension_semantics=sem, vmem_limit_bytes=VMEM_LIMIT)


def _sigmoid(x):
    return 1.0 / (1.0 + jnp.exp(-x))


def _silu(x):
    return x * _sigmoid(x)


def _dot(a, b):
    return jnp.dot(a, b, preferred_element_type=F32)


def _dot_nt(a, b):
    return lax.dot_general(a, b, (((1,), (1,)), ((), ())), preferred_element_type=F32)


def _dot_f32(a, b):
    return jnp.dot(a, b, preferred_element_type=F32, precision=lax.Precision.HIGHEST)


def _cumsum_rows(x):
    n = x.shape[0]
    tri = (lax.broadcasted_iota(jnp.int32, (n, n), 1) <= lax.broadcasted_iota(jnp.int32, (n, n), 0)).astype(BF16)
    x1 = x.astype(BF16)
    r1 = x - x1.astype(F32)
    x2 = r1.astype(BF16)
    x3 = (r1 - x2.astype(F32)).astype(BF16)
    return _dot(tri, x1) + _dot(tri, x2) + _dot(tri, x3)


def _norm_mod(x, norm_w, scale, shift):
    y = x * lax.rsqrt(jnp.mean(x * x, axis=-1, keepdims=True) + EPS)
    return (y * norm_w) * (1.0 + scale) + shift


def _ada_kernel(c_ref, w_ref, b_ref, o_ref):
    c = c_ref[...]
    o_ref[0] = _dot_f32(_silu(c), w_ref[0]) + b_ref[0]


def ada_modulation(c, ada_w, ada_b):
    bsz = c.shape[0]
    tn = 1536
    return pl.pallas_call(
        _ada_kernel,
        grid=(DEPTH, 6 * D // tn),
        in_specs=[pl.BlockSpec((bsz, D), lambda l, j: (0, 0)),
                  pl.BlockSpec((1, D, tn), lambda l, j: (l, 0, j)),
                  pl.BlockSpec((1, 1, tn), lambda l, j: (l, 0, j))],
        out_specs=pl.BlockSpec((1, bsz, tn), lambda l, j: (l, 0, j)),
        out_shape=jax.ShapeDtypeStruct((DEPTH, bsz, 6 * D), F32),
        compiler_params=_cparams("arbitrary", "arbitrary"),
        name="ada_modulation",
    )(c, ada_w, ada_b.reshape(DEPTH, 1, 6 * D))


def _in_proj_kernel(x_ref, mod_ref, nw_ref, w_ref, *rest, has_extra):
    if has_extra:
        wx_ref, o_ref, ox_ref, hn_ref = rest
    else:
        o_ref, hn_ref = rest

    @pl.when(pl.program_id(1) == 0)
    def _():
        hn = _norm_mod(x_ref[...], nw_ref[...], mod_ref[0, 1:2, :], mod_ref[0, 0:1, :])
        hn_ref[...] = hn.astype(BF16)
        if has_extra:
            ox_ref[...] = _dot(hn_ref[...], wx_ref[...])

    o_ref[...] = _dot(hn_ref[...], w_ref[0]).astype(o_ref.dtype)


def in_proj(x, mod, norm_w, w, w_extra, *, layer, n, seq, tm, tn):
    t = x.shape[0]
    per_b = seq // tm
    has_extra = w_extra is not None
    in_specs = [pl.BlockSpec((tm, D), lambda i, j: (i, 0)),
                pl.BlockSpec((1, 8, D), lambda i, j: (i // per_b, 0, 0)),
                pl.BlockSpec((1, D), lambda i, j: (0, 0)),
                pl.BlockSpec((1, D, tn), lambda i, j: (layer, 0, j))]
    out_specs = [pl.BlockSpec((tm, tn), lambda i, j: (i, j))]
    out_shape = [jax.ShapeDtypeStruct((t, n), BF16)]
    args = [x, mod, norm_w, w]
    if has_extra:
        nx = w_extra.shape[1]
        in_specs.append(pl.BlockSpec((D, nx), lambda i, j: (0, 0)))
        out_specs.append(pl.BlockSpec((tm, nx), lambda i, j: (i, 0)))
        out_shape.append(jax.ShapeDtypeStruct((t, nx), F32))
        args.append(w_extra)
    return pl.pallas_call(
        functools.partial(_in_proj_kernel, has_extra=has_extra),
        grid=(t // tm, n // tn),
        in_specs=in_specs,
        out_specs=out_specs,
        out_shape=out_shape,
        scratch_shapes=[pltpu.VMEM((tm, D), BF16)],
        compiler_params=_cparams("arbitrary", "arbitrary"),
        name="in_proj",
    )(*args)


def _interleave(chunks):
    live = list(chunks)
    while live:
        for gen in list(live):
            if next(gen, StopIteration) is StopIteration:
                live.remove(gen)


def _mlstm_kernel(qk_ref, v_ref, o_ref, g_ref, cw_ref, cb_ref, gb_ref, nw_ref, out_ref,
                  tail_ref, c_ref, m_ref):
    @pl.when(pl.program_id(1) == 0)
    def _():
        tail_ref[...] = jnp.zeros_like(tail_ref)
        c_ref[...] = jnp.zeros_like(c_ref)
        m_ref[...] = jnp.zeros_like(m_ref)

    _interleave([_mlstm_chunk(qk_ref.at[i], v_ref.at[i], o_ref.at[i], g_ref.at[i], cw_ref, cb_ref, gb_ref, nw_ref,
                              out_ref.at[i], tail_ref.at[i], c_ref.at[i], m_ref.at[i])
                 for i in range(qk_ref.shape[0])])


def _mlstm_chunk(qk_ref, v_ref, o_ref, g_ref, cw_ref, cb_ref, gb_ref, nw_ref, out_ref, tail_ref, c_ref, m_ref):
    L = CHUNK

    raw = qk_ref[...].astype(F32)
    xx = jnp.concatenate([tail_ref[...], raw], axis=0)
    tail_ref[...] = raw[L - 8:, :]
    y = cb_ref[...] + cw_ref[CONV_K - 1:CONV_K, :] * raw
    for j in range(CONV_K - 1):
        y = y + cw_ref[j:j + 1, :] * pltpu.roll(xx, CONV_K - 1 - j, axis=0)[8:, :]
    qk = _silu(y)
    half = N_HEADS * M_DQK
    q_all = qk[:, :half]
    k_all = qk[:, half:] * (M_DQK ** -0.5)

    g = g_ref[...] + gb_ref[...]
    li = IGATE_CAP * jnp.tanh(g[:, :LANES] * (1.0 / IGATE_CAP))
    gf = g[:, LANES:]
    b = _cumsum_rows(jnp.minimum(gf, 0.0) - jnp.log1p(jnp.exp(-jnp.abs(gf))))
    tril = lax.broadcasted_iota(jnp.int32, (L, L), 1) <= lax.broadcasted_iota(jnp.int32, (L, L), 0)

    a = li - b
    amax8 = a.reshape(L // SUBLANES, SUBLANES, LANES)
    sub = lax.broadcasted_iota(jnp.int32, (1, SUBLANES, LANES), 1)
    for sh in (1, 2, 4):
        amax8 = jnp.where(sub >= sh, jnp.maximum(amax8, pltpu.roll(amax8, sh, axis=1)), amax8)
    groups = [amax8[0]]
    for v in range(1, L // SUBLANES):
        carry = jnp.broadcast_to(groups[-1][SUBLANES - 1:SUBLANES, :], (SUBLANES, LANES))
        groups.append(jnp.maximum(amax8[v], carry))
    amax = jnp.concatenate(groups, axis=0)
    m_prev = m_ref[...]
    mx = jnp.maximum(m_prev, amax)
    mx_last = mx[L - 1:L, :]
    m_ref[...] = b[L - 1:L, :] + mx_last
    decay = jnp.exp(m_prev - mx_last)
    a_r = a.T[0:N_HEADS, :]
    wk_r = jnp.exp(a - mx_last).T[0:N_HEADS, :]

    lane = lax.broadcasted_iota(jnp.int32, (L, LANES), 1)
    ones_b = jnp.ones((L, LANES), BF16)
    kt_pairs = [k_all[:, p * LANES:(p + 1) * LANES].T for p in range(N_HEADS // 2)]
    qb_l, kb_l, ve_l, kw_l = [], [], [], []
    for h in range(N_HEADS):
        p, hf = h // 2, h % 2
        in_head = (lane >= hf * M_DQK) & (lane < (hf + 1) * M_DQK)
        qb_l.append(jnp.where(in_head, q_all[:, p * LANES:(p + 1) * LANES], 0.0).astype(BF16))
        kb_l.append(k_all[:, p * LANES:(p + 1) * LANES].astype(BF16))
        ve_l.append(jnp.concatenate([v_ref[:, h * HEAD_DV:(h + 1) * HEAD_DV].astype(BF16), ones_b], axis=-1))
        kw_l.append((kt_pairs[p] * wk_r[h:h + 1, :]).astype(BF16))
    qb = jnp.stack(qb_l)
    kb = jnp.stack(kb_l)
    v_ext = jnp.stack(ve_l)
    c_st = c_ref[...]

    qk_raw = jnp.einsum('hqd,hkd->hqk', qb, kb, preferred_element_type=F32)
    inter = jnp.einsum('hqd,hde->hqe', qb, c_st.astype(BF16), preferred_element_type=F32)
    for h in range(N_HEADS):
        c_ref[h] = decay[:, h:h + 1] * c_st[h] + _dot(kw_l[h], v_ext[h])
    yield

    def lane_bcast(cols):
        return jnp.stack([jnp.broadcast_to(cols[:, h:h + 1], (L, LANES)) for h in range(N_HEADS)])

    mx_b = lane_bcast(mx)
    m_prev_b = jnp.stack([jnp.broadcast_to(m_prev[:, h:h + 1], (1, LANES)) for h in range(N_HEADS)])
    w_inter = jnp.exp(m_prev_b - mx_b)
    pmat = jnp.exp(jnp.where(tril[None], a_r[:, None, :] - mx_b, NEG))
    s = qk_raw * pmat
    nd = (jnp.concatenate([w_inter, w_inter], axis=-1) * inter
          + jnp.einsum('hqk,hke->hqe', s.astype(BF16), v_ext, preferred_element_type=F32))
    floor = jnp.exp(-lane_bcast(b + mx))
    yield

    hh = nd[:, :, :HEAD_DV] / jnp.maximum(jnp.abs(nd[:, :, HEAD_DV:]), floor)
    ms = _dot((hh * hh).astype(BF16).reshape(N_HEADS * L, HEAD_DV), ones_b).reshape(hh.shape) * (1.0 / HEAD_DV)
    yield

    hn = hh * lax.rsqrt(ms + EPS)
    merged = jnp.concatenate([hn[h] for h in range(N_HEADS)], axis=-1) * nw_ref[...]
    out_ref[...] = (merged * _sigmoid(o_ref[...].astype(F32))).astype(out_ref.dtype)


def mlstm_mix(proj, gates, conv_w, conv_b, gate_bias, norm_w, *, bsz, seq):
    nblk = seq // CHUNK
    dv = N_HEADS * HEAD_DV
    ns = _seqs_per_step(bsz)
    proj = proj.reshape(bsz, seq, proj.shape[-1])
    gates = gates.reshape(bsz, seq, 2 * LANES)
    out = pl.pallas_call(
        _mlstm_kernel,
        grid=(bsz // ns, nblk),
        in_specs=[pl.BlockSpec((ns, CHUNK, D), lambda b, s: (b, s, 0)),
                  pl.BlockSpec((ns, CHUNK, dv), lambda b, s: (b, s, 1)),
                  pl.BlockSpec((ns, CHUNK, dv), lambda b, s: (b, s, 2)),
                  pl.BlockSpec((ns, CHUNK, 2 * LANES), lambda b, s: (b, s, 0)),
                  pl.BlockSpec((CONV_K, D), lambda b, s: (0, 0)),
                  pl.BlockSpec((1, D), lambda b, s: (0, 0)),
                  pl.BlockSpec((1, 2 * LANES), lambda b, s: (0, 0)),
                  pl.BlockSpec((1, dv), lambda b, s: (0, 0))],
        out_specs=pl.BlockSpec((ns, CHUNK, dv), lambda b, s: (b, s, 0)),
        out_shape=jax.ShapeDtypeStruct((bsz, seq, dv), BF16),
        scratch_shapes=[pltpu.VMEM((ns, 8, D), F32),
                        pltpu.VMEM((ns, N_HEADS, LANES, 2 * HEAD_DV), F32),
                        pltpu.VMEM((ns, 1, LANES), F32)],
        compiler_params=_cparams("arbitrary", "arbitrary"),
        name="mlstm_mix",
    )(proj, proj, proj, gates, conv_w, conv_b, gate_bias, norm_w)
    return out.reshape(bsz * seq, dv)


def _hgrn_kernel(q_ref, f_ref, v_ref, g_ref, lb_ref, nw_ref, out_ref, st_ref):
    @pl.when(pl.program_id(1) == 0)
    def _():
        st_ref[...] = jnp.zeros_like(st_ref)

    _interleave([_hgrn_chunk(q_ref.at[i], f_ref.at[i], v_ref.at[i], g_ref.at[i], lb_ref, nw_ref, out_ref.at[i],
                             st_ref.at[i]) for i in range(q_ref.shape[0])])


def _hgrn_chunk(q_ref, f_ref, v_ref, g_ref, lb_ref, nw_ref, out_ref, st_ref):
    L = CHUNK

    lb = lb_ref[...]
    f = f_ref[...].astype(F32)
    e = jnp.exp(-jnp.abs(f))
    r = 1.0 / (1.0 + e)
    er = e * r
    sig = jnp.where(f >= 0.0, r, er)
    nsig = jnp.where(f >= 0.0, er, r)
    dec_all = jnp.maximum(lb + (1.0 - lb) * sig, F_FLOOR)
    k_all = (1.0 - lb) * nsig
    q_all = _silu(q_ref[...].astype(F32))
    row = lax.broadcasted_iota(jnp.int32, (L, L), 0)
    col = lax.broadcasted_iota(jnp.int32, (L, L), 1)
    xor = row ^ col
    eye = row == col
    ones_b = jnp.ones((LANES, LANES), BF16)
    sub8 = lax.broadcasted_iota(jnp.int32, (1, 8, LANES), 1)
    H = N_HEADS
    G = L // SUBLANES

    def heads(x):
        return jnp.stack([x[:, h * HEAD_DV:(h + 1) * HEAD_DV] for h in range(H)])

    def bmm_nt(x, y):
        return jnp.einsum('hqd,hkd->hqk', x.astype(BF16), y.astype(BF16), preferred_element_type=F32)

    def rot8(x3, d):
        return pltpu.roll(x3, d % 8, axis=1)

    def level_mask(b):
        return ((xor >= b) & (xor < 2 * b))[None]

    q, k, dec = heads(q_all), heads(k_all), heads(dec_all)
    vb = heads(v_ref[...]).astype(BF16)
    st = st_ref[...]

    diag = _dot((q * k).astype(BF16).reshape(H * L, HEAD_DV), ones_b).reshape(H, L, L)

    n8 = H * G
    e8 = dec.reshape(n8, 8, LANES)
    q8 = q.reshape(n8, 8, LANES)
    k8 = k.reshape(n8, 8, LANES)
    ew = [e8]
    for j in range(1, 4):
        ew.append(ew[-1] * rot8(e8, j))
    eu = [rot8(e8, -1)]
    for j in range(1, 3):
        eu.append(eu[-1] * rot8(e8, -1 - j))
    pos2, pos4 = sub8 % 2, sub8 % 4
    pre2 = jnp.where(pos2 == 0, ew[0], ew[1])
    suf2 = jnp.where(pos2 == 0, eu[0], 1.0)
    pre4 = jnp.where(pos4 == 0, ew[0], jnp.where(pos4 == 1, ew[1], jnp.where(pos4 == 2, ew[2], ew[3])))
    suf4 = jnp.where(pos4 == 0, eu[2], jnp.where(pos4 == 1, eu[1], jnp.where(pos4 == 2, eu[0], 1.0)))
    small = [
        (4, jnp.where(sub8 >= 4, pre4, 0.0), jnp.where(sub8 < 4, suf4, 0.0)),
        (2, jnp.where(pos4 >= 2, pre2, 0.0), jnp.where(pos4 < 2, suf2, 0.0)),
        (1, jnp.where(pos2 == 1, e8, 0.0), jnp.where(pos2 == 0, 1.0, 0.0)),
    ]
    levels = [(b, bmm_nt((q8 * fq).reshape(H, L, LANES), (k8 * fk).reshape(H, L, LANES))) for b, fq, fk in small]

    pre = pre4 * jnp.where(sub8 >= 4, pre4[:, 3:4, :], 1.0)
    suf = suf4 * jnp.where(sub8 < 4, pre4[:, 7:8, :], 1.0)
    tot = pre[:, 7:8, :]
    q6 = q.reshape(H, G, 1, 1, SUBLANES, LANES)
    k6 = k.reshape(H, G, 1, 1, SUBLANES, LANES)
    m = 1
    while m < G:
        nb = G // (2 * m)
        shape = (H, nb, 2, m, SUBLANES, LANES)
        pre, suf, q6, k6 = (x.reshape(shape) for x in (pre, suf, q6, k6))
        tot = tot.reshape(H, nb, 2, 1, 1, LANES)
        zeros = jnp.zeros((H, nb, 1, m, SUBLANES, LANES), F32)
        q_l = jnp.concatenate([zeros, q6[:, :, 1:] * pre[:, :, 1:]], axis=2)
        k_l = jnp.concatenate([k6[:, :, :1] * suf[:, :, :1], zeros], axis=2)
        levels.append((SUBLANES * m, bmm_nt(q_l.reshape(H, L, LANES), k_l.reshape(H, L, LANES))))
        pre = jnp.concatenate([pre[:, :, :1], pre[:, :, 1:] * tot[:, :, :1]], axis=2)
        suf = jnp.concatenate([suf[:, :, :1] * tot[:, :, 1:], suf[:, :, 1:]], axis=2)
        tot = tot[:, :, :1] * tot[:, :, 1:]
        m *= 2
    pre = pre.reshape(H, L, LANES)
    suf = suf.reshape(H, L, LANES)
    tot = tot.reshape(H, 1, LANES)

    o_inter = bmm_nt(q * pre, st)
    k_dec = (k * suf).astype(BF16)
    for h in range(H):
        st_ref[h] = st[h] * tot[h] + lax.dot_general(
            vb[h], k_dec[h], (((0,), (0,)), ((), ())), preferred_element_type=F32)
    yield

    a = jnp.where(eye[None], diag, 0.0)
    for b, a_l in levels:
        a = jnp.where(level_mask(b), a_l, a)
    o = o_inter + jnp.einsum('hqk,hkv->hqv', a.astype(BF16), vb, preferred_element_type=F32)
    yield

    ms = _dot((o * o).astype(BF16).reshape(H * L, HEAD_DV), ones_b).reshape(o.shape) * (1.0 / HEAD_DV)
    yield

    on = o * lax.rsqrt(ms + EPS)
    merged = jnp.concatenate([on[h] for h in range(H)], axis=-1) * nw_ref[...]
    out_ref[...] = (merged * _silu(g_ref[...].astype(F32))).astype(out_ref.dtype)


def _seqs_per_step(bsz):
    return 2 if bsz % 2 == 0 else 1


def hgrn_mix(proj, lb, norm_w, *, bsz, seq):
    nblk = seq // CHUNK
    ns = _seqs_per_step(bsz)
    proj = proj.reshape(bsz, seq, proj.shape[-1])
    out = pl.pallas_call(
        _hgrn_kernel,
        grid=(bsz // ns, nblk),
        in_specs=[pl.BlockSpec((ns, CHUNK, D), lambda b, s: (b, s, 0)),
                  pl.BlockSpec((ns, CHUNK, D), lambda b, s: (b, s, 1)),
                  pl.BlockSpec((ns, CHUNK, D), lambda b, s: (b, s, 2)),
                  pl.BlockSpec((ns, CHUNK, D), lambda b, s: (b, s, 3)),
                  pl.BlockSpec((1, D), lambda b, s: (0, 0)),
                  pl.BlockSpec((1, D), lambda b, s: (0, 0))],
        out_specs=pl.BlockSpec((ns, CHUNK, D), lambda b, s: (b, s, 0)),
        out_shape=jax.ShapeDtypeStruct((bsz, seq, D), BF16),
        scratch_shapes=[pltpu.VMEM((ns, N_HEADS, HEAD_DV, HEAD_DV), F32)],
        compiler_params=_cparams("arbitrary", "arbitrary"),
        name="hgrn_mix",
    )(proj, proj, proj, proj, lb, norm_w)
    return out.reshape(bsz * seq, D)


def _mixer_residual(x_ref, y_ref, wo_ref, mod_ref):
    return x_ref[...] + mod_ref[0, 2:3, :] * _dot(y_ref[...], wo_ref[0])


def _ffn_kernel(x_ref, y_ref, wo_ref, mod_ref, nw_ref, wg_ref, wu_ref, wd_ref, o_ref, *, tf):
    x1 = _mixer_residual(x_ref, y_ref, wo_ref, mod_ref)
    hb = _norm_mod(x1, nw_ref[...], mod_ref[0, 4:5, :], mod_ref[0, 3:4, :]).astype(BF16)
    acc = None
    for f0 in range(0, wg_ref.shape[2], tf):
        hid = _silu(_dot(hb, wg_ref[0, :, f0:f0 + tf])) * _dot(hb, wu_ref[0, :, f0:f0 + tf])
        part = _dot(hid.astype(BF16), wd_ref[0, f0:f0 + tf, :])
        acc = part if acc is None else acc + part
    o_ref[...] = x1 + mod_ref[0, 5:6, :] * acc


def ffn(x, y, w_out, mod, norm_w, wg, wu, wd, *, layer, seq, tm, tf):
    t = x.shape[0]
    dff = wg.shape[2]
    per_b = seq // tm
    resident = dict(pipeline_mode=pl.Buffered(1))
    return pl.pallas_call(
        functools.partial(_ffn_kernel, tf=tf),
        grid=(t // tm,),
        in_specs=[pl.BlockSpec((tm, D), lambda i: (i, 0)),
                  pl.BlockSpec((tm, D), lambda i: (i, 0)),
                  pl.BlockSpec((1, D, D), lambda i: (layer, 0, 0), **resident),
                  pl.BlockSpec((1, 8, D), lambda i: (i // per_b, 0, 0)),
                  pl.BlockSpec((1, D), lambda i: (0, 0)),
                  pl.BlockSpec((1, D, dff), lambda i: (layer, 0, 0), **resident),
                  pl.BlockSpec((1, D, dff), lambda i: (layer, 0, 0), **resident),
                  pl.BlockSpec((1, dff, D), lambda i: (layer, 0, 0), **resident)],
        out_specs=pl.BlockSpec((tm, D), lambda i: (i, 0)),
        out_shape=jax.ShapeDtypeStruct((t, D), F32),
        compiler_params=_cparams("arbitrary"),
        name="ffn",
    )(x, y, w_out, mod, norm_w, wg, wu, wd)


def _router_kernel(x_ref, y_ref, wo_ref, mod_ref, nw_ref, wr_ref, x1_ref, hn_ref, rw_ref, ri_ref):
    x1 = _mixer_residual(x_ref, y_ref, wo_ref, mod_ref)
    x1_ref[...] = x1
    hn = _norm_mod(x1, nw_ref[...], mod_ref[0, 4:5, :], mod_ref[0, 3:4, :])
    _to_tile_rows(hn_ref, hn)
    h_hi = hn.astype(BF16)
    h_lo = (hn - h_hi.astype(F32)).astype(BF16)
    hw = _dot(h_hi, wr_ref[...])
    logits = hw[:, :LANES] + hw[:, LANES:] + _dot(h_lo, wr_ref[:, :LANES])
    lane = lax.broadcasted_iota(jnp.int32, logits.shape, 1)
    lg = jnp.where(lane < N_EXPERTS, logits, NEG)
    m1 = jnp.max(lg, axis=-1, keepdims=True)
    i1 = jnp.min(jnp.where(lg == m1, lane, LANES), axis=-1, keepdims=True)
    lg2 = jnp.where(lane == i1, NEG, lg)
    m2 = jnp.max(lg2, axis=-1, keepdims=True)
    i2 = jnp.min(jnp.where(lg2 == m2, lane, LANES), axis=-1, keepdims=True)
    e2 = jnp.exp(m2 - m1)
    w1 = 1.0 / (1.0 + e2)
    rw_ref[...] = jnp.where(lane == 0, w1, jnp.where(lane == 1, e2 * w1, 0.0))
    ri_ref[...] = jnp.where(lane == 0, i1, jnp.where(lane == 1, i2, 0))


def router(x, y, w_out, mod, norm_w, wr, *, layer, seq, tm):
    t = x.shape[0]
    per_b = seq // tm
    return pl.pallas_call(
        _router_kernel,
        grid=(t // tm,),
        in_specs=[pl.BlockSpec((tm, D), lambda i: (i, 0)),
                  pl.BlockSpec((tm, D), lambda i: (i, 0)),
                  pl.BlockSpec((1, D, D), lambda i: (layer, 0, 0), pipeline_mode=pl.Buffered(1)),
                  pl.BlockSpec((1, 8, D), lambda i: (i // per_b, 0, 0)),
                  pl.BlockSpec((1, D), lambda i: (0, 0)),
                  pl.BlockSpec((D, 2 * LANES), lambda i: (0, 0))],
        out_specs=[pl.BlockSpec((tm, D), lambda i: (i, 0)),
                   pl.BlockSpec((tm * ROW_TILES, LANES), lambda i: (i, 0)),
                   pl.BlockSpec((tm, LANES), lambda i: (i, 0)),
                   pl.BlockSpec((tm, LANES), lambda i: (i, 0))],
        out_shape=[jax.ShapeDtypeStruct((t, D), F32),
                   jax.ShapeDtypeStruct((t * ROW_TILES, LANES), F32),
                   jax.ShapeDtypeStruct((t, LANES), F32),
                   jax.ShapeDtypeStruct((t, LANES), jnp.int32)],
        compiler_params=_cparams("arbitrary"),
        name="router",
    )(x, y, w_out, mod, norm_w, wr)


SUBLANES = 8
ROW_TILES = D // LANES


def _to_tile_rows(ref, val):
    rows = val.shape[0]
    for s in range(ROW_TILES):
        ref[pl.ds(s, rows, stride=ROW_TILES), :] = val[:, s * LANES:(s + 1) * LANES]


def _from_tile_rows(ref, rows):
    return jnp.concatenate([ref[pl.ds(s, rows, stride=ROW_TILES), :] for s in range(ROW_TILES)], axis=-1)


def _tile_row(ref, row):
    return ref.at[pl.ds(pl.multiple_of(row * ROW_TILES, ROW_TILES), ROW_TILES)]


def _for_each_row_pair(tm, fn):
    def body(row, carry):
        for k in range(2):
            fn(row, 2 * row + k, k)
        return carry

    lax.fori_loop(0, tm, body, 0, unroll=8)


def _dispatch_kernel(ends_ref, dest_ref, hn_ref, xs_ref, zero_ref, sem):
    tm = hn_ref.shape[0] // ROW_TILES
    tgr = zero_ref.shape[0]

    @pl.when(pl.program_id(0) == 0)
    def _():
        zero_ref[...] = jnp.zeros_like(zero_ref)

        def zero_tile(start):
            cp = pltpu.make_async_copy(zero_ref, xs_ref.at[pl.ds(pl.multiple_of(start, tgr), tgr)], sem)
            cp.start()
            cp.wait()

        n_tiles = xs_ref.shape[0] // tgr
        for e in range(N_EXPERTS):
            @pl.when(ends_ref[N_EXPERTS + e] > 0)
            def _():
                zero_tile(ends_ref[e] * ROW_TILES - tgr)

            @pl.when(n_tiles - 1 - e >= ends_ref[2 * N_EXPERTS])
            def _():
                zero_tile((n_tiles - 1 - e) * tgr)

    def scatter(row, slot, k):
        pltpu.make_async_copy(_tile_row(hn_ref, row), _tile_row(xs_ref, dest_ref[0, 0, slot]),
                              sem).start(priority=k)

    _for_each_row_pair(tm, scatter)
    for k in range(2):
        pltpu.make_async_copy(hn_ref, xs_ref.at[pl.ds(0, tm * ROW_TILES)], sem).wait()


def dispatch(ends_padded, dest, hn, *, n_rows, tm, tg):
    t = hn.shape[0] // ROW_TILES
    grid_spec = pltpu.PrefetchScalarGridSpec(
        num_scalar_prefetch=1,
        grid=(t // tm,),
        in_specs=[pl.BlockSpec((1, 1, 2 * tm), lambda i, ends: (i, 0, 0), memory_space=pltpu.SMEM),
                  pl.BlockSpec((tm * ROW_TILES, LANES), lambda i, ends: (i, 0))],
        out_specs=pl.BlockSpec(memory_space=pl.ANY),
        scratch_shapes=[pltpu.VMEM((tg * ROW_TILES, LANES), F32), pltpu.SemaphoreType.DMA],
    )
    return pl.pallas_call(
        _dispatch_kernel,
        grid_spec=grid_spec,
        out_shape=jax.ShapeDtypeStruct((n_rows * ROW_TILES, LANES), F32),
        compiler_params=_cparams("arbitrary"),
        name="moe_dispatch",
    )(ends_padded, dest, hn)


def _expert_ffn_kernel(te_ref, nu_ref, x_ref, wg_ref, wu_ref, wd_ref, o_ref):
    del te_ref
    i = pl.program_id(0)

    @pl.when(i < nu_ref[0])
    def _():
        hb = _from_tile_rows(x_ref, x_ref.shape[0] // ROW_TILES).astype(BF16)
        hid = _silu(_dot(hb, wg_ref[0, 0])) * _dot(hb, wu_ref[0, 0])
        _to_tile_rows(o_ref, _dot(hid.astype(BF16), wd_ref[0, 0]))

    @pl.when(i >= nu_ref[0])
    def _():
        o_ref[...] = jnp.zeros_like(o_ref)


def expert_ffn(tile_expert, n_used, xs, wg, wu, wd, *, layer, tg):
    r = xs.shape[0] // ROW_TILES
    dfe = wg.shape[3]
    grid_spec = pltpu.PrefetchScalarGridSpec(
        num_scalar_prefetch=2,
        grid=(r // tg,),
        in_specs=[pl.BlockSpec((tg * ROW_TILES, LANES), lambda i, te, nu: (jnp.minimum(i, nu[0] - 1), 0)),
                  pl.BlockSpec((1, 1, D, dfe), lambda i, te, nu: (layer, te[i], 0, 0)),
                  pl.BlockSpec((1, 1, D, dfe), lambda i, te, nu: (layer, te[i], 0, 0)),
                  pl.BlockSpec((1, 1, dfe, D), lambda i, te, nu: (layer, te[i], 0, 0))],
        out_specs=pl.BlockSpec((tg * ROW_TILES, LANES), lambda i, te, nu: (i, 0)),
    )
    return pl.pallas_call(
        _expert_ffn_kernel,
        grid_spec=grid_spec,
        out_shape=jax.ShapeDtypeStruct((r * ROW_TILES, LANES), F32),
        compiler_params=_cparams("arbitrary"),
        name="moe_expert_ffn",
    )(tile_expert, n_used, xs, wg, wu, wd)


def _combine_kernel(dest_ref, dest_next_ref, x_ref, mod_ref, rw_ref, fw_ref, ys_ref, o_ref, buf_ref, sem, *, final):
    tm = x_ref.shape[0]
    i = pl.program_id(0)
    cur = lax.rem(i, 2)

    def issue(idx_ref, buf_slot):
        def gather(row, slot, k):
            pltpu.make_async_copy(_tile_row(ys_ref, idx_ref[0, 0, slot]), _tile_row(buf_ref.at[buf_slot, k], row),
                                  sem.at[buf_slot]).start(priority=k)

        _for_each_row_pair(tm, gather)

    @pl.when(i == 0)
    def _():
        issue(dest_ref, 0)

    @pl.when(i + 1 < pl.num_programs(0))
    def _():
        issue(dest_next_ref, 1 - cur)

    for k in range(2):
        pltpu.make_async_copy(ys_ref.at[pl.ds(0, tm * ROW_TILES)], buf_ref.at[cur, k], sem.at[cur]).wait()
    rw = rw_ref[...]
    y = (rw[:, 0:1] * _from_tile_rows(buf_ref.at[cur, 0], tm)
         + rw[:, 1:2] * _from_tile_rows(buf_ref.at[cur, 1], tm))
    out = x_ref[...] + mod_ref[0, 5:6, :] * y
    if final:
        out = out * lax.rsqrt(jnp.mean(out * out, axis=-1, keepdims=True) + EPS) * fw_ref[...]
    o_ref[...] = out


def combine(dest, x, mod, rw, final_w, ys, *, seq, tm, final):
    t = x.shape[0]
    per_b = seq // tm
    last = t // tm - 1
    return pl.pallas_call(
        functools.partial(_combine_kernel, final=final),
        grid=(t // tm,),
        in_specs=[pl.BlockSpec((1, 1, 2 * tm), lambda i: (i, 0, 0), memory_space=pltpu.SMEM),
                  pl.BlockSpec((1, 1, 2 * tm), lambda i: (jnp.minimum(i + 1, last), 0, 0), memory_space=pltpu.SMEM),
                  pl.BlockSpec((tm, D), lambda i: (i, 0)),
                  pl.BlockSpec((1, 8, D), lambda i: (i // per_b, 0, 0)),
                  pl.BlockSpec((tm, LANES), lambda i: (i, 0)),
                  pl.BlockSpec((1, D), lambda i: (0, 0)),
                  pl.BlockSpec(memory_space=pl.ANY)],
        out_specs=pl.BlockSpec((tm, D), lambda i: (i, 0)),
        out_shape=jax.ShapeDtypeStruct((t, D), F32),
        scratch_shapes=[pltpu.VMEM((2, 2, tm * ROW_TILES, LANES), F32), pltpu.SemaphoreType.DMA((2,))],
        compiler_params=_cparams("arbitrary"),
        name="moe_combine",
    )(dest, dest, x, mod, rw, final_w, ys)


def _routing_plan(route_i, *, tm, tg):
    t = route_i.shape[0]
    e_flat = route_i[:, :2].reshape(-1)
    onehot = (e_flat[:, None] == jnp.arange(N_EXPERTS, dtype=jnp.int32)[None, :]).astype(jnp.int32)
    csum = jnp.cumsum(onehot, axis=0)
    counts = csum[-1]
    padded = ((counts + tg - 1) // tg) * tg
    ends = jnp.cumsum(padded)
    starts = ends - padded
    dest = jnp.sum(onehot * (starts[None, :] + csum - onehot), axis=1)
    n_tiles = (2 * t) // tg + N_EXPERTS
    tile_start = jnp.arange(n_tiles, dtype=jnp.int32) * tg
    tile_expert = jnp.sum((tile_start[:, None] >= ends[None, :]).astype(jnp.int32), axis=1)
    n_used = (ends[-1] // tg).astype(jnp.int32)
    last_expert = jnp.take(tile_expert, jnp.maximum(n_used - 1, 0))
    tile_expert = jnp.where(tile_start < ends[-1], tile_expert, last_expert).astype(jnp.int32)
    ends_padded = jnp.concatenate([ends, padded, n_used.reshape(1)]).astype(jnp.int32)
    return (dest.reshape(t // tm, 1, 2 * tm).astype(jnp.int32), tile_expert, n_used.reshape(1), ends_padded,
            n_tiles * tg)


def moe(x, y, w_out, mod, norm_w, wr, wg, wu, wd, final_w, *, layer, seq, tm, tg, final):
    x1, hn, rw, ri = router(x, y, w_out, mod, norm_w, wr, layer=layer, seq=seq, tm=tm)
    dest, tile_expert, n_used, ends_padded, n_rows = _routing_plan(ri, tm=tm, tg=tg)
    xs = dispatch(ends_padded, dest, hn, n_rows=n_rows, tm=tm, tg=tg)
    ys = expert_ffn(tile_expert, n_used, xs, wg, wu, wd, layer=layer, tg=tg)
    return combine(dest, x1, mod, rw, final_w, ys, seq=seq, tm=tm, final=final)


def _pad_cols(w, n):
    return jnp.pad(w, ((0, 0), (0, n - w.shape[1])))


def kernel(x, c, ada_w, ada_b, norm1_w, norm2_w, final_norm_w, m_w_in, m_i_bias, m_f_bias, m_conv_w, m_conv_b, m_norm_w, m_w_out, h_w_in, h_lb_logits, h_norm_w, h_w_out, ffn_w_gate, ffn_w_up, ffn_w_down, moe_router, moe_w_gate, moe_w_up, moe_w_down):
    bsz, seq, _ = x.shape
    t = bsz * seq
    tm = min(512, seq)
    xt = x.reshape(t, D)

    mod_all = ada_modulation(c, ada_w, ada_b)
    mod_all = jnp.pad(mod_all.reshape(DEPTH, bsz, 6, D), ((0, 0), (0, 0), (0, 2), (0, 0)))

    gam = jax.nn.softmax(h_lb_logits.astype(F32), axis=0)
    lower_bounds = jnp.cumsum(gam, axis=0) - gam[:1]

    n_main = 2 * N_HEADS * M_DQK + 2 * N_HEADS * HEAD_DV
    tm_in = min(1024, seq)
    final_w = final_norm_w.reshape(1, D)
    m_w_in_b, h_w_in_b = m_w_in.astype(BF16), h_w_in.astype(BF16)
    m_w_out_b, h_w_out_b = m_w_out.astype(BF16), h_w_out.astype(BF16)
    ffn_g, ffn_u, ffn_d = ffn_w_gate.astype(BF16), ffn_w_up.astype(BF16), ffn_w_down.astype(BF16)
    moe_g, moe_u, moe_d = moe_w_gate.astype(BF16), moe_w_up.astype(BF16), moe_w_down.astype(BF16)
    for layer in range(DEPTH):
        j = layer // 2
        mod = mod_all[layer]
        n1 = norm1_w[layer].reshape(1, D)
        n2 = norm2_w[layer].reshape(1, D)
        if layer % 2 == 0:
            w_gates = jnp.concatenate([_pad_cols(m_w_in[j, :, n_main:n_main + N_HEADS], LANES),
                                       _pad_cols(m_w_in[j, :, n_main + N_HEADS:], LANES)], axis=1).astype(BF16)
            proj, gates = in_proj(xt, mod, n1, m_w_in_b, w_gates, layer=j, n=n_main,
                                  seq=seq, tm=tm_in, tn=n_main // 2)
            gate_bias = jnp.concatenate([_pad_cols(m_i_bias[j].reshape(1, -1), LANES),
                                         _pad_cols(m_f_bias[j].reshape(1, -1), LANES)], axis=1)
            y = mlstm_mix(proj, gates, m_conv_w[j], m_conv_b[j].reshape(1, -1), gate_bias,
                          m_norm_w[j].reshape(1, -1), bsz=bsz, seq=seq)
            xt = ffn(xt, y, m_w_out_b, mod, n2, ffn_g, ffn_u, ffn_d, layer=j, seq=seq, tm=tm, tf=1408)
        else:
            n_h = h_w_in.shape[2]
            (proj,) = in_proj(xt, mod, n1, h_w_in_b, None, layer=j, n=n_h, seq=seq, tm=tm_in, tn=n_h // 2)
            y = hgrn_mix(proj, lower_bounds[j].reshape(1, D), h_norm_w[j].reshape(1, D), bsz=bsz, seq=seq)
            wr = _pad_cols(moe_router[j], LANES)
            wr_hi = wr.astype(BF16)
            wr_split = jnp.concatenate([wr_hi, (wr - wr_hi.astype(F32)).astype(BF16)], axis=1)
            xt = moe(xt, y, h_w_out_b, mod, n2, wr_split, moe_g, moe_u, moe_d, final_w,
                     layer=j, seq=seq, tm=tm, tg=tm, final=layer == DEPTH - 1)
    return xt.reshape(bsz, seq, D)
```

```python
import functools

import jax
import jax.numpy as jnp
from jax import lax
from jax.experimental import pallas as pl
from jax.experimental.pallas import tpu as pltpu

F32 = jnp.float32
BF16 = jnp.bfloat16

DEPTH = 4
D = 1024
N_HEADS = 8
HEAD_DV = 128
M_DQK = 64
CONV_K = 4
IGATE_CAP = 15.0
N_EXPERTS = 8
EPS = 1e-6
NEG = -1e30
F_FLOOR = 1e-30
SHIFT1, SCALE1, GATE1, SHIFT2, SCALE2, GATE2 = range(6)
N_MOD = 6
LANES = 128
SUBLANES = 8
ROW_TILES = D // LANES
MOD_ROWS = SUBLANES
CHUNK = 128
VMEM_LIMIT = 56 * 1024 * 1024


def _tiles(seq):
    return dict(tm=min(512, seq),
                tm_in=min(1024, seq))


def _cparams(*sem):
    return pltpu.CompilerParams(dimension_semantics=sem, vmem_limit_bytes=VMEM_LIMIT)


def _sigmoid(x):
    return 1.0 / (1.0 + jnp.exp(-x))


def _silu(x):
    return x * _sigmoid(x)


def _dot(a, b):
    return jnp.dot(a, b, preferred_element_type=F32)


def _dot_f32(a, b):
    return jnp.dot(a, b, preferred_element_type=F32, precision=lax.Precision.HIGHEST)


def _cumsum_rows(x):
    n = x.shape[0]
    tri = (lax.broadcasted_iota(jnp.int32, (n, n), 1) <= lax.broadcasted_iota(jnp.int32, (n, n), 0)).astype(BF16)
    x1 = x.astype(BF16)
    r1 = x - x1.astype(F32)
    x2 = r1.astype(BF16)
    x3 = (r1 - x2.astype(F32)).astype(BF16)
    return _dot(tri, x1) + _dot(tri, x2) + _dot(tri, x3)


def _mod(mod_ref, row):
    return mod_ref[0, row:row + 1, :]


def _norm_mod(x, norm_w, scale, shift):
    y = x * lax.rsqrt(jnp.mean(x * x, axis=-1, keepdims=True) + EPS)
    return (y * norm_w) * (1.0 + scale) + shift


def _ada_kernel(c_ref, w_ref, b_ref, o_ref):
    c = c_ref[...]
    o_ref[0] = _dot_f32(_silu(c), w_ref[0]) + b_ref[0]


def ada_modulation(c, ada_w, ada_b):
    bsz = c.shape[0]
    tn = N_MOD * D // 4
    return pl.pallas_call(
        _ada_kernel,
        grid=(DEPTH, N_MOD * D // tn),
        in_specs=[pl.BlockSpec((bsz, D), lambda l, j: (0, 0)),
                  pl.BlockSpec((1, D, tn), lambda l, j: (l, 0, j)),
                  pl.BlockSpec((1, 1, tn), lambda l, j: (l, 0, j))],
        out_specs=pl.BlockSpec((1, bsz, tn), lambda l, j: (l, 0, j)),
        out_shape=jax.ShapeDtypeStruct((DEPTH, bsz, N_MOD * D), F32),
        compiler_params=_cparams("arbitrary", "arbitrary"),
        name="ada_modulation",
    )(c, ada_w, ada_b.reshape(DEPTH, 1, N_MOD * D))


def _in_proj_kernel(x_ref, mod_ref, nw_ref, w_ref, *rest, has_extra):
    if has_extra:
        wx_ref, o_ref, ox_ref, hn_ref = rest
    else:
        o_ref, hn_ref = rest

    @pl.when(pl.program_id(1) == 0)
    def _():
        hn = _norm_mod(x_ref[...], nw_ref[...], _mod(mod_ref, SCALE1), _mod(mod_ref, SHIFT1))
        hn_ref[...] = hn.astype(BF16)
        if has_extra:
            ox_ref[...] = _dot(hn_ref[...], wx_ref[...])

    o_ref[...] = _dot(hn_ref[...], w_ref[0]).astype(o_ref.dtype)


def in_proj(x, mod, norm_w, w, w_extra, *, layer, n, seq, tm, tn):
    t = x.shape[0]
    per_b = seq // tm
    has_extra = w_extra is not None
    in_specs = [pl.BlockSpec((tm, D), lambda i, j: (i, 0)),
                pl.BlockSpec((1, MOD_ROWS, D), lambda i, j: (i // per_b, 0, 0)),
                pl.BlockSpec((1, D), lambda i, j: (0, 0)),
                pl.BlockSpec((1, D, tn), lambda i, j: (layer, 0, j))]
    out_specs = [pl.BlockSpec((tm, tn), lambda i, j: (i, j))]
    out_shape = [jax.ShapeDtypeStruct((t, n), BF16)]
    args = [x, mod, norm_w, w]
    if has_extra:
        nx = w_extra.shape[1]
        in_specs.append(pl.BlockSpec((D, nx), lambda i, j: (0, 0)))
        out_specs.append(pl.BlockSpec((tm, nx), lambda i, j: (i, 0)))
        out_shape.append(jax.ShapeDtypeStruct((t, nx), F32))
        args.append(w_extra)
    return pl.pallas_call(
        functools.partial(_in_proj_kernel, has_extra=has_extra),
        grid=(t // tm, n // tn),
        in_specs=in_specs,
        out_specs=out_specs,
        out_shape=out_shape,
        scratch_shapes=[pltpu.VMEM((tm, D), BF16)],
        compiler_params=_cparams("arbitrary", "arbitrary"),
        name="in_proj",
    )(*args)


def _interleave(chunks):
    live = list(chunks)
    while live:
        for gen in list(live):
            if next(gen, StopIteration) is StopIteration:
                live.remove(gen)


def _mlstm_kernel(qk_ref, v_ref, o_ref, g_ref, cw_ref, cb_ref, gb_ref, nw_ref, out_ref,
                  tail_ref, c_ref, m_ref):
    @pl.when(pl.program_id(1) == 0)
    def _():
        tail_ref[...] = jnp.zeros_like(tail_ref)
        c_ref[...] = jnp.zeros_like(c_ref)
        m_ref[...] = jnp.zeros_like(m_ref)

    _interleave([_mlstm_chunk(qk_ref.at[i], v_ref.at[i], o_ref.at[i], g_ref.at[i], cw_ref, cb_ref, gb_ref, nw_ref,
                              out_ref.at[i], tail_ref.at[i], c_ref.at[i], m_ref.at[i])
                 for i in range(qk_ref.shape[0])])


def _mlstm_chunk(qk_ref, v_ref, o_ref, g_ref, cw_ref, cb_ref, gb_ref, nw_ref, out_ref, tail_ref, c_ref, m_ref):
    L = CHUNK

    raw = qk_ref[...].astype(F32)
    xx = jnp.concatenate([tail_ref[...], raw], axis=0)
    tail_ref[...] = raw[L - SUBLANES:, :]
    y = cb_ref[...] + cw_ref[CONV_K - 1:CONV_K, :] * raw
    for j in range(CONV_K - 1):
        y = y + cw_ref[j:j + 1, :] * pltpu.roll(xx, CONV_K - 1 - j, axis=0)[SUBLANES:, :]
    qk = _silu(y)
    half = N_HEADS * M_DQK
    q_all = qk[:, :half]
    k_all = qk[:, half:] * (M_DQK ** -0.5)

    g = g_ref[...] + gb_ref[...]
    li = IGATE_CAP * jnp.tanh(g[:, :LANES] * (1.0 / IGATE_CAP))
    gf = g[:, LANES:]
    b = _cumsum_rows(jnp.minimum(gf, 0.0) - jnp.log1p(jnp.exp(-jnp.abs(gf))))
    tril = lax.broadcasted_iota(jnp.int32, (L, L), 1) <= lax.broadcasted_iota(jnp.int32, (L, L), 0)

    a = li - b
    amax8 = a.reshape(L // SUBLANES, SUBLANES, LANES)
    sub = lax.broadcasted_iota(jnp.int32, (1, SUBLANES, LANES), 1)
    for sh in (1, 2, 4):
        amax8 = jnp.where(sub >= sh, jnp.maximum(amax8, pltpu.roll(amax8, sh, axis=1)), amax8)
    groups = [amax8[0]]
    for v in range(1, L // SUBLANES):
        carry = jnp.broadcast_to(groups[-1][SUBLANES - 1:SUBLANES, :], (SUBLANES, LANES))
        groups.append(jnp.maximum(amax8[v], carry))
    amax = jnp.concatenate(groups, axis=0)
    m_prev = m_ref[...]
    mx = jnp.maximum(m_prev, amax)
    mx_last = mx[L - 1:L, :]
    m_ref[...] = b[L - 1:L, :] + mx_last
    decay = jnp.exp(m_prev - mx_last)
    a_r = a.T[0:N_HEADS, :]
    wk_r = jnp.exp(a - mx_last).T[0:N_HEADS, :]

    lane = lax.broadcasted_iota(jnp.int32, (L, LANES), 1)
    ones_b = jnp.ones((L, LANES), BF16)
    kt_pairs = [k_all[:, p * LANES:(p + 1) * LANES].T for p in range(N_HEADS // 2)]
    qb_l, kb_l, ve_l, kw_l = [], [], [], []
    for h in range(N_HEADS):
        p, hf = h // 2, h % 2
        in_head = (lane >= hf * M_DQK) & (lane < (hf + 1) * M_DQK)
        qb_l.append(jnp.where(in_head, q_all[:, p * LANES:(p + 1) * LANES], 0.0).astype(BF16))
        kb_l.append(k_all[:, p * LANES:(p + 1) * LANES].astype(BF16))
        ve_l.append(jnp.concatenate([v_ref[:, h * HEAD_DV:(h + 1) * HEAD_DV].astype(BF16), ones_b], axis=-1))
        kw_l.append((kt_pairs[p] * wk_r[h:h + 1, :]).astype(BF16))
    qb = jnp.stack(qb_l)
    kb = jnp.stack(kb_l)
    v_ext = jnp.stack(ve_l)
    c_st = c_ref[...]

    qk_raw = jnp.einsum('hqd,hkd->hqk', qb, kb, preferred_element_type=F32)
    inter = jnp.einsum('hqd,hde->hqe', qb, c_st.astype(BF16), preferred_element_type=F32)
    for h in range(N_HEADS):
        c_ref[h] = decay[:, h:h + 1] * c_st[h] + _dot(kw_l[h], v_ext[h])
    yield

    def lane_bcast(cols):
        return jnp.stack([jnp.broadcast_to(cols[:, h:h + 1], (L, LANES)) for h in range(N_HEADS)])

    mx_b = lane_bcast(mx)
    m_prev_b = jnp.stack([jnp.broadcast_to(m_prev[:, h:h + 1], (1, LANES)) for h in range(N_HEADS)])
    w_inter = jnp.exp(m_prev_b - mx_b)
    pmat = jnp.exp(jnp.where(tril[None], a_r[:, None, :] - mx_b, NEG))
    s = qk_raw * pmat
    nd = (jnp.concatenate([w_inter, w_inter], axis=-1) * inter
          + jnp.einsum('hqk,hke->hqe', s.astype(BF16), v_ext, preferred_element_type=F32))
    floor = jnp.exp(-lane_bcast(b + mx))
    yield

    hh = nd[:, :, :HEAD_DV] / jnp.maximum(jnp.abs(nd[:, :, HEAD_DV:]), floor)
    ms = _dot((hh * hh).astype(BF16).reshape(N_HEADS * L, HEAD_DV), ones_b).reshape(hh.shape) * (1.0 / HEAD_DV)
    yield

    hn = hh * lax.rsqrt(ms + EPS)
    merged = jnp.concatenate([hn[h] for h in range(N_HEADS)], axis=-1) * nw_ref[...]
    out_ref[...] = (merged * _sigmoid(o_ref[...].astype(F32))).astype(out_ref.dtype)


def mlstm_mix(proj, gates, conv_w, conv_b, gate_bias, norm_w, *, bsz, seq):
    nblk = seq // CHUNK
    dv = N_HEADS * HEAD_DV
    ns = _seqs_per_step(bsz)
    proj = proj.reshape(bsz, seq, proj.shape[-1])
    gates = gates.reshape(bsz, seq, 2 * LANES)
    out = pl.pallas_call(
        _mlstm_kernel,
        grid=(bsz // ns, nblk),
        in_specs=[pl.BlockSpec((ns, CHUNK, D), lambda b, s: (b, s, 0)),
                  pl.BlockSpec((ns, CHUNK, dv), lambda b, s: (b, s, 1)),
                  pl.BlockSpec((ns, CHUNK, dv), lambda b, s: (b, s, 2)),
                  pl.BlockSpec((ns, CHUNK, 2 * LANES), lambda b, s: (b, s, 0)),
                  pl.BlockSpec((CONV_K, D), lambda b, s: (0, 0)),
                  pl.BlockSpec((1, D), lambda b, s: (0, 0)),
                  pl.BlockSpec((1, 2 * LANES), lambda b, s: (0, 0)),
                  pl.BlockSpec((1, dv), lambda b, s: (0, 0))],
        out_specs=pl.BlockSpec((ns, CHUNK, dv), lambda b, s: (b, s, 0)),
        out_shape=jax.ShapeDtypeStruct((bsz, seq, dv), BF16),
        scratch_shapes=[pltpu.VMEM((ns, SUBLANES, D), F32),
                        pltpu.VMEM((ns, N_HEADS, LANES, 2 * HEAD_DV), F32),
                        pltpu.VMEM((ns, 1, LANES), F32)],
        compiler_params=_cparams("arbitrary", "arbitrary"),
        name="mlstm_mix",
    )(proj, proj, proj, gates, conv_w, conv_b, gate_bias, norm_w)
    return out.reshape(bsz * seq, dv)


def _hgrn_kernel(q_ref, f_ref, v_ref, g_ref, lb_ref, nw_ref, out_ref, st_ref):
    @pl.when(pl.program_id(1) == 0)
    def _():
        st_ref[...] = jnp.zeros_like(st_ref)

    _interleave([_hgrn_chunk(q_ref.at[i], f_ref.at[i], v_ref.at[i], g_ref.at[i], lb_ref, nw_ref, out_ref.at[i],
                             st_ref.at[i]) for i in range(q_ref.shape[0])])


def _hgrn_chunk(q_ref, f_ref, v_ref, g_ref, lb_ref, nw_ref, out_ref, st_ref):
    L = CHUNK

    lb = lb_ref[...]
    f = f_ref[...].astype(F32)
    e = jnp.exp(-jnp.abs(f))
    r = 1.0 / (1.0 + e)
    er = e * r
    sig = jnp.where(f >= 0.0, r, er)
    nsig = jnp.where(f >= 0.0, er, r)
    dec_all = jnp.maximum(lb + (1.0 - lb) * sig, F_FLOOR)
    k_all = (1.0 - lb) * nsig
    q_all = _silu(q_ref[...].astype(F32))
    row = lax.broadcasted_iota(jnp.int32, (L, L), 0)
    col = lax.broadcasted_iota(jnp.int32, (L, L), 1)
    xor = row ^ col
    eye = row == col
    ones_b = jnp.ones((LANES, LANES), BF16)
    sub8 = lax.broadcasted_iota(jnp.int32, (1, SUBLANES, LANES), 1)
    H = N_HEADS
    G = L // SUBLANES

    def heads(x):
        return jnp.stack([x[:, h * HEAD_DV:(h + 1) * HEAD_DV] for h in range(H)])

    def bmm_nt(x, y):
        return jnp.einsum('hqd,hkd->hqk', x.astype(BF16), y.astype(BF16), preferred_element_type=F32)

    def rot8(x3, d):
        return pltpu.roll(x3, d % SUBLANES, axis=1)

    def level_mask(b):
        return ((xor >= b) & (xor < 2 * b))[None]

    q, k, dec = heads(q_all), heads(k_all), heads(dec_all)
    vb = heads(v_ref[...]).astype(BF16)
    st = st_ref[...]

    diag = _dot((q * k).astype(BF16).reshape(H * L, HEAD_DV), ones_b).reshape(H, L, L)

    n8 = H * G
    e8 = dec.reshape(n8, SUBLANES, LANES)
    q8 = q.reshape(n8, SUBLANES, LANES)
    k8 = k.reshape(n8, SUBLANES, LANES)
    ew = [e8]
    for j in range(1, 4):
        ew.append(ew[-1] * rot8(e8, j))
    eu = [rot8(e8, -1)]
    for j in range(1, 3):
        eu.append(eu[-1] * rot8(e8, -1 - j))
    pos2, pos4 = sub8 % 2, sub8 % 4
    pre2 = jnp.where(pos2 == 0, ew[0], ew[1])
    suf2 = jnp.where(pos2 == 0, eu[0], 1.0)
    pre4 = jnp.where(pos4 == 0, ew[0], jnp.where(pos4 == 1, ew[1], jnp.where(pos4 == 2, ew[2], ew[3])))
    suf4 = jnp.where(pos4 == 0, eu[2], jnp.where(pos4 == 1, eu[1], jnp.where(pos4 == 2, eu[0], 1.0)))
    small = [
        (4, jnp.where(sub8 >= 4, pre4, 0.0), jnp.where(sub8 < 4, suf4, 0.0)),
        (2, jnp.where(pos4 >= 2, pre2, 0.0), jnp.where(pos4 < 2, suf2, 0.0)),
        (1, jnp.where(pos2 == 1, e8, 0.0), jnp.where(pos2 == 0, 1.0, 0.0)),
    ]
    levels = [(b, bmm_nt((q8 * fq).reshape(H, L, LANES), (k8 * fk).reshape(H, L, LANES))) for b, fq, fk in small]

    pre = pre4 * jnp.where(sub8 >= 4, pre4[:, 3:4, :], 1.0)
    suf = suf4 * jnp.where(sub8 < 4, pre4[:, 7:8, :], 1.0)
    tot = pre[:, 7:8, :]
    q6 = q.reshape(H, G, 1, 1, SUBLANES, LANES)
    k6 = k.reshape(H, G, 1, 1, SUBLANES, LANES)
    m = 1
    while m < G:
        nb = G // (2 * m)
        shape = (H, nb, 2, m, SUBLANES, LANES)
        pre, suf, q6, k6 = (x.reshape(shape) for x in (pre, suf, q6, k6))
        tot = tot.reshape(H, nb, 2, 1, 1, LANES)
        zeros = jnp.zeros((H, nb, 1, m, SUBLANES, LANES), F32)
        q_l = jnp.concatenate([zeros, q6[:, :, 1:] * pre[:, :, 1:]], axis=2)
        k_l = jnp.concatenate([k6[:, :, :1] * suf[:, :, :1], zeros], axis=2)
        levels.append((SUBLANES * m, bmm_nt(q_l.reshape(H, L, LANES), k_l.reshape(H, L, LANES))))
        pre = jnp.concatenate([pre[:, :, :1], pre[:, :, 1:] * tot[:, :, :1]], axis=2)
        suf = jnp.concatenate([suf[:, :, :1] * tot[:, :, 1:], suf[:, :, 1:]], axis=2)
        tot = tot[:, :, :1] * tot[:, :, 1:]
        m *= 2
    pre = pre.reshape(H, L, LANES)
    suf = suf.reshape(H, L, LANES)
    tot = tot.reshape(H, 1, LANES)

    o_inter = bmm_nt(q * pre, st)
    k_dec = (k * suf).astype(BF16)
    for h in range(H):
        st_ref[h] = st[h] * tot[h] + lax.dot_general(
            vb[h], k_dec[h], (((0,), (0,)), ((), ())), preferred_element_type=F32)
    yield

    a = jnp.where(eye[None], diag, 0.0)
    for b, a_l in levels:
        a = jnp.where(level_mask(b), a_l, a)
    o = o_inter + jnp.einsum('hqk,hkv->hqv', a.astype(BF16), vb, preferred_element_type=F32)
    yield

    ms = _dot((o * o).astype(BF16).reshape(H * L, HEAD_DV), ones_b).reshape(o.shape) * (1.0 / HEAD_DV)
    yield

    on = o * lax.rsqrt(ms + EPS)
    merged = jnp.concatenate([on[h] for h in range(H)], axis=-1) * nw_ref[...]
    out_ref[...] = (merged * _silu(g_ref[...].astype(F32))).astype(out_ref.dtype)


def _seqs_per_step(bsz):
    return 2 if bsz % 2 == 0 else 1


def hgrn_mix(proj, lb, norm_w, *, bsz, seq):
    nblk = seq // CHUNK
    ns = _seqs_per_step(bsz)
    proj = proj.reshape(bsz, seq, proj.shape[-1])
    out = pl.pallas_call(
        _hgrn_kernel,
        grid=(bsz // ns, nblk),
        in_specs=[pl.BlockSpec((ns, CHUNK, D), lambda b, s: (b, s, 0)),
                  pl.BlockSpec((ns, CHUNK, D), lambda b, s: (b, s, 1)),
                  pl.BlockSpec((ns, CHUNK, D), lambda b, s: (b, s, 2)),
                  pl.BlockSpec((ns, CHUNK, D), lambda b, s: (b, s, 3)),
                  pl.BlockSpec((1, D), lambda b, s: (0, 0)),
                  pl.BlockSpec((1, D), lambda b, s: (0, 0))],
        out_specs=pl.BlockSpec((ns, CHUNK, D), lambda b, s: (b, s, 0)),
        out_shape=jax.ShapeDtypeStruct((bsz, seq, D), BF16),
        scratch_shapes=[pltpu.VMEM((ns, N_HEADS, HEAD_DV, HEAD_DV), F32)],
        compiler_params=_cparams("arbitrary", "arbitrary"),
        name="hgrn_mix",
    )(proj, proj, proj, proj, lb, norm_w)
    return out.reshape(bsz * seq, D)


def _mixer_residual(x_ref, y_ref, wo_ref, mod_ref):
    return x_ref[...] + _mod(mod_ref, GATE1) * _dot(y_ref[...], wo_ref[0])


def _ffn_kernel(x_ref, y_ref, wo_ref, mod_ref, nw_ref, wg_ref, wu_ref, wd_ref, o_ref, *, tf):
    x1 = _mixer_residual(x_ref, y_ref, wo_ref, mod_ref)
    hb = _norm_mod(x1, nw_ref[...], _mod(mod_ref, SCALE2), _mod(mod_ref, SHIFT2)).astype(BF16)
    acc = None
    for f0 in range(0, wg_ref.shape[2], tf):
        hid = _silu(_dot(hb, wg_ref[0, :, f0:f0 + tf])) * _dot(hb, wu_ref[0, :, f0:f0 + tf])
        part = _dot(hid.astype(BF16), wd_ref[0, f0:f0 + tf, :])
        acc = part if acc is None else acc + part
    o_ref[...] = x1 + _mod(mod_ref, GATE2) * acc


def ffn(x, y, w_out, mod, norm_w, wg, wu, wd, *, layer, seq, tm, tf):
    t = x.shape[0]
    dff = wg.shape[2]
    per_b = seq // tm
    resident = dict(pipeline_mode=pl.Buffered(1))
    return pl.pallas_call(
        functools.partial(_ffn_kernel, tf=tf),
        grid=(t // tm,),
        in_specs=[pl.BlockSpec((tm, D), lambda i: (i, 0)),
                  pl.BlockSpec((tm, D), lambda i: (i, 0)),
                  pl.BlockSpec((1, D, D), lambda i: (layer, 0, 0), **resident),
                  pl.BlockSpec((1, MOD_ROWS, D), lambda i: (i // per_b, 0, 0)),
                  pl.BlockSpec((1, D), lambda i: (0, 0)),
                  pl.BlockSpec((1, D, dff), lambda i: (layer, 0, 0), **resident),
                  pl.BlockSpec((1, D, dff), lambda i: (layer, 0, 0), **resident),
                  pl.BlockSpec((1, dff, D), lambda i: (layer, 0, 0), **resident)],
        out_specs=pl.BlockSpec((tm, D), lambda i: (i, 0)),
        out_shape=jax.ShapeDtypeStruct((t, D), F32),
        compiler_params=_cparams("arbitrary"),
        name="ffn",
    )(x, y, w_out, mod, norm_w, wg, wu, wd)


def _router_kernel(x_ref, y_ref, wo_ref, mod_ref, nw_ref, wr_ref, x1_ref, hn_ref, rw_ref, ri_ref):
    x1 = _mixer_residual(x_ref, y_ref, wo_ref, mod_ref)
    x1_ref[...] = x1
    hn = _norm_mod(x1, nw_ref[...], _mod(mod_ref, SCALE2), _mod(mod_ref, SHIFT2))
    _to_tile_rows(hn_ref, hn)
    h_hi = hn.astype(BF16)
    h_lo = (hn - h_hi.astype(F32)).astype(BF16)
    hw = _dot(h_hi, wr_ref[...])
    logits = hw[:, :LANES] + hw[:, LANES:] + _dot(h_lo, wr_ref[:, :LANES])
    lane = lax.broadcasted_iota(jnp.int32, logits.shape, 1)
    lg = jnp.where(lane < N_EXPERTS, logits, NEG)
    m1 = jnp.max(lg, axis=-1, keepdims=True)
    i1 = jnp.min(jnp.where(lg == m1, lane, LANES), axis=-1, keepdims=True)
    lg2 = jnp.where(lane == i1, NEG, lg)
    m2 = jnp.max(lg2, axis=-1, keepdims=True)
    i2 = jnp.min(jnp.where(lg2 == m2, lane, LANES), axis=-1, keepdims=True)
    e2 = jnp.exp(m2 - m1)
    w1 = 1.0 / (1.0 + e2)
    rw_ref[...] = jnp.where(lane == 0, w1, jnp.where(lane == 1, e2 * w1, 0.0))
    ri_ref[...] = jnp.where(lane == 0, i1, jnp.where(lane == 1, i2, 0))


def router(x, y, w_out, mod, norm_w, wr, *, layer, seq, tm):
    t = x.shape[0]
    per_b = seq // tm
    return pl.pallas_call(
        _router_kernel,
        grid=(t // tm,),
        in_specs=[pl.BlockSpec((tm, D), lambda i: (i, 0)),
                  pl.BlockSpec((tm, D), lambda i: (i, 0)),
                  pl.BlockSpec((1, D, D), lambda i: (layer, 0, 0), pipeline_mode=pl.Buffered(1)),
                  pl.BlockSpec((1, MOD_ROWS, D), lambda i: (i // per_b, 0, 0)),
                  pl.BlockSpec((1, D), lambda i: (0, 0)),
                  pl.BlockSpec((D, 2 * LANES), lambda i: (0, 0))],
        out_specs=[pl.BlockSpec((tm, D), lambda i: (i, 0)),
                   pl.BlockSpec((tm * ROW_TILES, LANES), lambda i: (i, 0)),
                   pl.BlockSpec((tm, LANES), lambda i: (i, 0)),
                   pl.BlockSpec((tm, LANES), lambda i: (i, 0))],
        out_shape=[jax.ShapeDtypeStruct((t, D), F32),
                   jax.ShapeDtypeStruct((t * ROW_TILES, LANES), F32),
                   jax.ShapeDtypeStruct((t, LANES), F32),
                   jax.ShapeDtypeStruct((t, LANES), jnp.int32)],
        compiler_params=_cparams("arbitrary"),
        name="router",
    )(x, y, w_out, mod, norm_w, wr)


def _to_tile_rows(ref, val):
    rows = val.shape[0]
    for s in range(ROW_TILES):
        ref[pl.ds(s, rows, stride=ROW_TILES), :] = val[:, s * LANES:(s + 1) * LANES]


def _from_tile_rows(ref, rows):
    return jnp.concatenate([ref[pl.ds(s, rows, stride=ROW_TILES), :] for s in range(ROW_TILES)], axis=-1)


def _tile_row(ref, row):
    return ref.at[pl.ds(pl.multiple_of(row * ROW_TILES, ROW_TILES), ROW_TILES)]


def _for_each_row_pair(tm, fn):
    def body(row, carry):
        for k in range(2):
            fn(row, 2 * row + k, k)
        return carry

    lax.fori_loop(0, tm, body, 0, unroll=8)


def _dispatch_kernel(ends_ref, dest_ref, hn_ref, xs_ref, zero_ref, stage_ref, sem):
    tm = hn_ref.shape[0] // ROW_TILES
    tgr = zero_ref.shape[0]
    i = pl.program_id(0)
    cur = lax.rem(i, 2)

    @pl.when(i == 0)
    def _():
        zero_ref[...] = jnp.zeros_like(zero_ref)

        def zero_tile(start):
            cp = pltpu.make_async_copy(zero_ref, xs_ref.at[pl.ds(pl.multiple_of(start, tgr), tgr)], sem.at[2])
            cp.start()
            cp.wait()

        n_tiles = xs_ref.shape[0] // tgr
        for e in range(N_EXPERTS):
            @pl.when(ends_ref[N_EXPERTS + e] > 0)
            def _():
                zero_tile(ends_ref[e] * ROW_TILES - tgr)

            @pl.when(n_tiles - 1 - e >= ends_ref[2 * N_EXPERTS])
            def _():
                zero_tile((n_tiles - 1 - e) * tgr)

    stage_ref[cur] = hn_ref[...]

    def scatter(row, slot, k):
        pltpu.make_async_copy(_tile_row(stage_ref.at[cur], row), _tile_row(xs_ref, dest_ref[0, 0, slot]),
                              sem.at[cur]).start(priority=k)

    _for_each_row_pair(tm, scatter)

    def drain(buf_slot):
        for _ in range(2):
            pltpu.make_async_copy(stage_ref.at[buf_slot], xs_ref.at[pl.ds(0, tm * ROW_TILES)],
                                  sem.at[buf_slot]).wait()

    @pl.when(i > 0)
    def _():
        drain(1 - cur)

    @pl.when(i == pl.num_programs(0) - 1)
    def _():
        drain(cur)


def dispatch(ends_padded, dest, hn, *, n_rows, tm, tg):
    t = hn.shape[0] // ROW_TILES
    grid_spec = pltpu.PrefetchScalarGridSpec(
        num_scalar_prefetch=1,
        grid=(t // tm,),
        in_specs=[pl.BlockSpec((1, 1, 2 * tm), lambda i, ends: (i, 0, 0), memory_space=pltpu.SMEM),
                  pl.BlockSpec((tm * ROW_TILES, LANES), lambda i, ends: (i, 0))],
        out_specs=pl.BlockSpec(memory_space=pl.ANY),
        scratch_shapes=[pltpu.VMEM((tg * ROW_TILES, LANES), F32),
                        pltpu.VMEM((2, tm * ROW_TILES, LANES), F32),
                        pltpu.SemaphoreType.DMA((3,))],
    )
    return pl.pallas_call(
        _dispatch_kernel,
        grid_spec=grid_spec,
        out_shape=jax.ShapeDtypeStruct((n_rows * ROW_TILES, LANES), F32),
        compiler_params=_cparams("arbitrary"),
        name="moe_dispatch",
    )(ends_padded, dest, hn)


def _expert_ffn_kernel(te_ref, nu_ref, x_ref, wg_ref, wu_ref, wd_ref, o_ref):
    del te_ref
    i = pl.program_id(0)

    @pl.when(i < nu_ref[0])
    def _():
        hb = _from_tile_rows(x_ref, x_ref.shape[0] // ROW_TILES).astype(BF16)
        hid = _silu(_dot(hb, wg_ref[0, 0])) * _dot(hb, wu_ref[0, 0])
        _to_tile_rows(o_ref, _dot(hid.astype(BF16), wd_ref[0, 0]))

    @pl.when(i >= nu_ref[0])
    def _():
        o_ref[...] = jnp.zeros_like(o_ref)


def expert_ffn(tile_expert, n_used, xs, wg, wu, wd, *, layer, tg):
    r = xs.shape[0] // ROW_TILES
    dfe = wg.shape[3]
    grid_spec = pltpu.PrefetchScalarGridSpec(
        num_scalar_prefetch=2,
        grid=(r // tg,),
        in_specs=[pl.BlockSpec((tg * ROW_TILES, LANES), lambda i, te, nu: (jnp.minimum(i, nu[0] - 1), 0)),
                  pl.BlockSpec((1, 1, D, dfe), lambda i, te, nu: (layer, te[i], 0, 0)),
                  pl.BlockSpec((1, 1, D, dfe), lambda i, te, nu: (layer, te[i], 0, 0)),
                  pl.BlockSpec((1, 1, dfe, D), lambda i, te, nu: (layer, te[i], 0, 0))],
        out_specs=pl.BlockSpec((tg * ROW_TILES, LANES), lambda i, te, nu: (i, 0)),
    )
    return pl.pallas_call(
        _expert_ffn_kernel,
        grid_spec=grid_spec,
        out_shape=jax.ShapeDtypeStruct((r * ROW_TILES, LANES), F32),
        compiler_params=_cparams("arbitrary"),
        name="moe_expert_ffn",
    )(tile_expert, n_used, xs, wg, wu, wd)


def _combine_kernel(dest_ref, dest_next_ref, x_ref, mod_ref, rw_ref, fw_ref, ys_ref, o_ref, buf_ref, sem, *, final):
    tm = x_ref.shape[0]
    i = pl.program_id(0)
    cur = lax.rem(i, 2)

    def issue(idx_ref, buf_slot):
        def gather(row, slot, k):
            pltpu.make_async_copy(_tile_row(ys_ref, idx_ref[0, 0, slot]), _tile_row(buf_ref.at[buf_slot, k], row),
                                  sem.at[buf_slot]).start(priority=k)

        _for_each_row_pair(tm, gather)

    @pl.when(i == 0)
    def _():
        issue(dest_ref, 0)

    @pl.when(i + 1 < pl.num_programs(0))
    def _():
        issue(dest_next_ref, 1 - cur)

    for k in range(2):
        pltpu.make_async_copy(ys_ref.at[pl.ds(0, tm * ROW_TILES)], buf_ref.at[cur, k], sem.at[cur]).wait()
    rw = rw_ref[...]
    y = (rw[:, 0:1] * _from_tile_rows(buf_ref.at[cur, 0], tm)
         + rw[:, 1:2] * _from_tile_rows(buf_ref.at[cur, 1], tm))
    out = x_ref[...] + _mod(mod_ref, GATE2) * y
    if final:
        out = out * lax.rsqrt(jnp.mean(out * out, axis=-1, keepdims=True) + EPS) * fw_ref[...]
    o_ref[...] = out


def combine(dest, x, mod, rw, final_w, ys, *, seq, tm, final):
    t = x.shape[0]
    per_b = seq // tm
    last = t // tm - 1
    return pl.pallas_call(
        functools.partial(_combine_kernel, final=final),
        grid=(t // tm,),
        in_specs=[pl.BlockSpec((1, 1, 2 * tm), lambda i: (i, 0, 0), memory_space=pltpu.SMEM),
                  pl.BlockSpec((1, 1, 2 * tm), lambda i: (jnp.minimum(i + 1, last), 0, 0), memory_space=pltpu.SMEM),
                  pl.BlockSpec((tm, D), lambda i: (i, 0)),
                  pl.BlockSpec((1, MOD_ROWS, D), lambda i: (i // per_b, 0, 0)),
                  pl.BlockSpec((tm, LANES), lambda i: (i, 0)),
                  pl.BlockSpec((1, D), lambda i: (0, 0)),
                  pl.BlockSpec(memory_space=pl.ANY)],
        out_specs=pl.BlockSpec((tm, D), lambda i: (i, 0)),
        out_shape=jax.ShapeDtypeStruct((t, D), F32),
        scratch_shapes=[pltpu.VMEM((2, 2, tm * ROW_TILES, LANES), F32), pltpu.SemaphoreType.DMA((2,))],
        compiler_params=_cparams("arbitrary"),
        name="moe_combine",
    )(dest, dest, x, mod, rw, final_w, ys)


def _routing_plan(route_i, *, tm, tg):
    t = route_i.shape[0]
    e_flat = route_i[:, :2].reshape(-1)
    onehot = (e_flat[:, None] == jnp.arange(N_EXPERTS, dtype=jnp.int32)[None, :]).astype(jnp.int32)
    csum = jnp.cumsum(onehot, axis=0)
    counts = csum[-1]
    padded = ((counts + tg - 1) // tg) * tg
    ends = jnp.cumsum(padded)
    starts = ends - padded
    dest = jnp.sum(onehot * (starts[None, :] + csum - onehot), axis=1)
    n_tiles = (2 * t) // tg + N_EXPERTS
    tile_start = jnp.arange(n_tiles, dtype=jnp.int32) * tg
    tile_expert = jnp.sum((tile_start[:, None] >= ends[None, :]).astype(jnp.int32), axis=1)
    n_used = (ends[-1] // tg).astype(jnp.int32)
    last_expert = jnp.take(tile_expert, jnp.maximum(n_used - 1, 0))
    tile_expert = jnp.where(tile_start < ends[-1], tile_expert, last_expert).astype(jnp.int32)
    ends_padded = jnp.concatenate([ends, padded, n_used.reshape(1)]).astype(jnp.int32)
    return (dest.reshape(t // tm, 1, 2 * tm).astype(jnp.int32), tile_expert, n_used.reshape(1), ends_padded,
            n_tiles * tg)


def moe(x, y, w_out, mod, norm_w, wr, wg, wu, wd, final_w, *, layer, seq, tm, tg, final):
    x1, hn, rw, ri = router(x, y, w_out, mod, norm_w, wr, layer=layer, seq=seq, tm=tm)
    dest, tile_expert, n_used, ends_padded, n_rows = _routing_plan(ri, tm=tm, tg=tg)
    xs = dispatch(ends_padded, dest, hn, n_rows=n_rows, tm=tm, tg=tg)
    ys = expert_ffn(tile_expert, n_used, xs, wg, wu, wd, layer=layer, tg=tg)
    return combine(dest, x1, mod, rw, final_w, ys, seq=seq, tm=tm, final=final)


def _pad_cols(w, n):
    return jnp.pad(w, ((0, 0), (0, n - w.shape[1])))


def kernel(x, c, ada_w, ada_b, norm1_w, norm2_w, final_norm_w, m_w_in, m_i_bias, m_f_bias, m_conv_w, m_conv_b, m_norm_w, m_w_out, h_w_in, h_lb_logits, h_norm_w, h_w_out, ffn_w_gate, ffn_w_up, ffn_w_down, moe_router, moe_w_gate, moe_w_up, moe_w_down):
    bsz, seq, _ = x.shape
    t = bsz * seq
    tiles = _tiles(seq)
    tm, tm_in = tiles['tm'], tiles['tm_in']
    xt = x.reshape(t, D)

    mod_all = ada_modulation(c, ada_w, ada_b)
    mod_all = jnp.pad(mod_all.reshape(DEPTH, bsz, N_MOD, D), ((0, 0), (0, 0), (0, MOD_ROWS - N_MOD), (0, 0)))

    gam = jax.nn.softmax(h_lb_logits.astype(F32), axis=0)
    lower_bounds = jnp.cumsum(gam, axis=0) - gam[:1]

    n_main = 2 * N_HEADS * M_DQK + 2 * N_HEADS * HEAD_DV
    final_w = final_norm_w.reshape(1, D)
    m_w_in_b, h_w_in_b = m_w_in.astype(BF16), h_w_in.astype(BF16)
    m_w_out_b, h_w_out_b = m_w_out.astype(BF16), h_w_out.astype(BF16)
    ffn_g, ffn_u, ffn_d = ffn_w_gate.astype(BF16), ffn_w_up.astype(BF16), ffn_w_down.astype(BF16)
    moe_g, moe_u, moe_d = moe_w_gate.astype(BF16), moe_w_up.astype(BF16), moe_w_down.astype(BF16)
    for layer in range(DEPTH):
        j = layer // 2
        mod = mod_all[layer]
        n1 = norm1_w[layer].reshape(1, D)
        n2 = norm2_w[layer].reshape(1, D)
        if layer % 2 == 0:
            w_gates = jnp.concatenate([_pad_cols(m_w_in[j, :, n_main:n_main + N_HEADS], LANES),
                                       _pad_cols(m_w_in[j, :, n_main + N_HEADS:], LANES)], axis=1).astype(BF16)
            proj, gates = in_proj(xt, mod, n1, m_w_in_b, w_gates, layer=j, n=n_main,
                                  seq=seq, tm=tm_in, tn=n_main // 2)
            gate_bias = jnp.concatenate([_pad_cols(m_i_bias[j].reshape(1, -1), LANES),
                                         _pad_cols(m_f_bias[j].reshape(1, -1), LANES)], axis=1)
            y = mlstm_mix(proj, gates, m_conv_w[j], m_conv_b[j].reshape(1, -1), gate_bias,
                          m_norm_w[j].reshape(1, -1), bsz=bsz, seq=seq)
            xt = ffn(xt, y, m_w_out_b, mod, n2, ffn_g, ffn_u, ffn_d, layer=j, seq=seq, tm=tm,
                     tf=ffn_w_gate.shape[2] // 2)
        else:
            n_h = h_w_in.shape[2]
            (proj,) = in_proj(xt, mod, n1, h_w_in_b, None, layer=j, n=n_h, seq=seq, tm=tm_in, tn=n_h // 2)
            y = hgrn_mix(proj, lower_bounds[j].reshape(1, D), h_norm_w[j].reshape(1, D), bsz=bsz, seq=seq)
            wr = _pad_cols(moe_router[j], LANES)
            wr_hi = wr.astype(BF16)
            wr_split = jnp.concatenate([wr_hi, (wr - wr_hi.astype(F32)).astype(BF16)], axis=1)
            xt = moe(xt, y, h_w_out_b, mod, n2, wr_split, moe_g, moe_u, moe_d, final_w,
                     layer=j, seq=seq, tm=tm, tg=tm, final=layer == DEPTH - 1)
    return xt.reshape(bsz, seq, D)
```

```python
import functools

import jax
import jax.numpy as jnp
from jax import lax
from jax.experimental import pallas as pl
from jax.experimental.pallas import tpu as pltpu

F32 = jnp.float32
BF16 = jnp.bfloat16

DEPTH = 4
D = 1024
N_HEADS = 8
HEAD_DV = 128
M_DQK = 64
CONV_K = 4
IGATE_CAP = 15.0
N_EXPERTS = 8
EPS = 1e-6
NEG = -1e30
F_FLOOR = 1e-30
SHIFT1, SCALE1, GATE1, SHIFT2, SCALE2, GATE2 = range(6)
N_MOD = 6
LANES = 128
SUBLANES = 8
ROW_TILES = D // LANES
MOD_ROWS = SUBLANES
CHUNK = 128
VMEM_LIMIT = 56 * 1024 * 1024


def _tiles(seq):
    return dict(tm=min(512, seq),
                tm_in=min(1024, seq))


def _cparams(*sem):
    return pltpu.CompilerParams(dimension_semantics=sem, vmem_limit_bytes=VMEM_LIMIT)


def _sigmoid(x):
    return 1.0 / (1.0 + jnp.exp(-x))


def _silu(x):
    return x * _sigmoid(x)


def _dot(a, b):
    return jnp.dot(a, b, preferred_element_type=F32)


def _dot_f32(a, b):
    return jnp.dot(a, b, preferred_element_type=F32, precision=lax.Precision.HIGHEST)


def _cumsum_rows(x):
    n = x.shape[0]
    tri = (lax.broadcasted_iota(jnp.int32, (n, n), 1) <= lax.broadcasted_iota(jnp.int32, (n, n), 0)).astype(BF16)
    x1 = x.astype(BF16)
    r1 = x - x1.astype(F32)
    x2 = r1.astype(BF16)
    x3 = (r1 - x2.astype(F32)).astype(BF16)
    return _dot(tri, x1) + _dot(tri, x2) + _dot(tri, x3)


def _mod(mod_ref, row):
    return mod_ref[0, row:row + 1, :]


def _norm_mod(x, norm_w, scale, shift):
    y = x * lax.rsqrt(jnp.mean(x * x, axis=-1, keepdims=True) + EPS)
    return (y * norm_w) * (1.0 + scale) + shift


def _ada_kernel(c_ref, w_ref, b_ref, o_ref):
    c = c_ref[...]
    o_ref[0] = _dot_f32(_silu(c), w_ref[0]) + b_ref[0]


def ada_modulation(c, ada_w, ada_b):
    bsz = c.shape[0]
    tn = N_MOD * D // 4
    return pl.pallas_call(
        _ada_kernel,
        grid=(DEPTH, N_MOD * D // tn),
        in_specs=[pl.BlockSpec((bsz, D), lambda l, j: (0, 0)),
                  pl.BlockSpec((1, D, tn), lambda l, j: (l, 0, j)),
                  pl.BlockSpec((1, 1, tn), lambda l, j: (l, 0, j))],
        out_specs=pl.BlockSpec((1, bsz, tn), lambda l, j: (l, 0, j)),
        out_shape=jax.ShapeDtypeStruct((DEPTH, bsz, N_MOD * D), F32),
        compiler_params=_cparams("arbitrary", "arbitrary"),
        name="ada_modulation",
    )(c, ada_w, ada_b.reshape(DEPTH, 1, N_MOD * D))


def _in_proj_kernel(x_ref, mod_ref, nw_ref, w_ref, *rest, has_extra):
    if has_extra:
        wx_ref, o_ref, ox_ref, hn_ref = rest
    else:
        o_ref, hn_ref = rest

    @pl.when(pl.program_id(1) == 0)
    def _():
        hn = _norm_mod(x_ref[...], nw_ref[...], _mod(mod_ref, SCALE1), _mod(mod_ref, SHIFT1))
        hn_ref[...] = hn.astype(BF16)
        if has_extra:
            ox_ref[...] = _dot(hn_ref[...], wx_ref[...])

    o_ref[...] = _dot(hn_ref[...], w_ref[0]).astype(o_ref.dtype)


def in_proj(x, mod, norm_w, w, w_extra, *, layer, n, seq, tm, tn):
    t = x.shape[0]
    per_b = seq // tm
    has_extra = w_extra is not None
    in_specs = [pl.BlockSpec((tm, D), lambda i, j: (i, 0)),
                pl.BlockSpec((1, MOD_ROWS, D), lambda i, j: (i // per_b, 0, 0)),
                pl.BlockSpec((1, D), lambda i, j: (0, 0)),
                pl.BlockSpec((1, D, tn), lambda i, j: (layer, 0, j))]
    out_specs = [pl.BlockSpec((tm, tn), lambda i, j: (i, j))]
    out_shape = [jax.ShapeDtypeStruct((t, n), BF16)]
    args = [x, mod, norm_w, w]
    if has_extra:
        nx = w_extra.shape[1]
        in_specs.append(pl.BlockSpec((D, nx), lambda i, j: (0, 0)))
        out_specs.append(pl.BlockSpec((tm, nx), lambda i, j: (i, 0)))
        out_shape.append(jax.ShapeDtypeStruct((t, nx), F32))
        args.append(w_extra)
    return pl.pallas_call(
        functools.partial(_in_proj_kernel, has_extra=has_extra),
        grid=(t // tm, n // tn),
        in_specs=in_specs,
        out_specs=out_specs,
        out_shape=out_shape,
        scratch_shapes=[pltpu.VMEM((tm, D), BF16)],
        compiler_params=_cparams("arbitrary", "arbitrary"),
        name="in_proj",
    )(*args)


def _interleave(chunks):
    live = list(chunks)
    while live:
        for gen in list(live):
            if next(gen, StopIteration) is StopIteration:
                live.remove(gen)


def _mlstm_kernel(qk_ref, v_ref, o_ref, g_ref, cw_ref, cb_ref, gb_ref, nw_ref, out_ref,
                  tail_ref, c_ref, m_ref):
    @pl.when(pl.program_id(1) == 0)
    def _():
        tail_ref[...] = jnp.zeros_like(tail_ref)
        c_ref[...] = jnp.zeros_like(c_ref)
        m_ref[...] = jnp.zeros_like(m_ref)

    _interleave([_mlstm_chunk(qk_ref.at[i], v_ref.at[i], o_ref.at[i], g_ref.at[i], cw_ref, cb_ref, gb_ref, nw_ref,
                              out_ref.at[i], tail_ref.at[i], c_ref.at[i], m_ref.at[i])
                 for i in range(qk_ref.shape[0])])


def _mlstm_chunk(qk_ref, v_ref, o_ref, g_ref, cw_ref, cb_ref, gb_ref, nw_ref, out_ref, tail_ref, c_ref, m_ref):
    L = CHUNK

    raw = qk_ref[...].astype(F32)
    xx = jnp.concatenate([tail_ref[...], raw], axis=0)
    tail_ref[...] = raw[L - SUBLANES:, :]
    y = cb_ref[...] + cw_ref[CONV_K - 1:CONV_K, :] * raw
    for j in range(CONV_K - 1):
        y = y + cw_ref[j:j + 1, :] * pltpu.roll(xx, CONV_K - 1 - j, axis=0)[SUBLANES:, :]
    qk = _silu(y)
    half = N_HEADS * M_DQK
    q_all = qk[:, :half]
    k_all = qk[:, half:] * (M_DQK ** -0.5)

    g = g_ref[...] + gb_ref[...]
    li = IGATE_CAP * jnp.tanh(g[:, :LANES] * (1.0 / IGATE_CAP))
    gf = g[:, LANES:]
    b = _cumsum_rows(jnp.minimum(gf, 0.0) - jnp.log1p(jnp.exp(-jnp.abs(gf))))
    tril = lax.broadcasted_iota(jnp.int32, (L, L), 1) <= lax.broadcasted_iota(jnp.int32, (L, L), 0)

    a = li - b
    amax8 = a.reshape(L // SUBLANES, SUBLANES, LANES)
    sub = lax.broadcasted_iota(jnp.int32, (1, SUBLANES, LANES), 1)
    for sh in (1, 2, 4):
        amax8 = jnp.where(sub >= sh, jnp.maximum(amax8, pltpu.roll(amax8, sh, axis=1)), amax8)
    groups = [amax8[0]]
    for v in range(1, L // SUBLANES):
        carry = jnp.broadcast_to(groups[-1][SUBLANES - 1:SUBLANES, :], (SUBLANES, LANES))
        groups.append(jnp.maximum(amax8[v], carry))
    amax = jnp.concatenate(groups, axis=0)
    m_prev = m_ref[...]
    mx = jnp.maximum(m_prev, amax)
    mx_last = mx[L - 1:L, :]
    m_ref[...] = b[L - 1:L, :] + mx_last
    decay = jnp.exp(m_prev - mx_last)
    a_r = a.T[0:N_HEADS, :]
    wk_r = jnp.exp(a - mx_last).T[0:N_HEADS, :]

    lane = lax.broadcasted_iota(jnp.int32, (L, LANES), 1)
    ones_b = jnp.ones((L, LANES), BF16)
    kt_pairs = [k_all[:, p * LANES:(p + 1) * LANES].T for p in range(N_HEADS // 2)]
    qb_l, kb_l, ve_l, kw_l = [], [], [], []
    for h in range(N_HEADS):
        p, hf = h // 2, h % 2
        in_head = (lane >= hf * M_DQK) & (lane < (hf + 1) * M_DQK)
        qb_l.append(jnp.where(in_head, q_all[:, p * LANES:(p + 1) * LANES], 0.0).astype(BF16))
        kb_l.append(k_all[:, p * LANES:(p + 1) * LANES].astype(BF16))
        ve_l.append(jnp.concatenate([v_ref[:, h * HEAD_DV:(h + 1) * HEAD_DV].astype(BF16), ones_b], axis=-1))
        kw_l.append((kt_pairs[p] * wk_r[h:h + 1, :]).astype(BF16))
    qb = jnp.stack(qb_l)
    kb = jnp.stack(kb_l)
    v_ext = jnp.stack(ve_l)
    c_st = c_ref[...]

    qk_raw = jnp.einsum('hqd,hkd->hqk', qb, kb, preferred_element_type=F32)
    inter = jnp.einsum('hqd,hde->hqe', qb, c_st.astype(BF16), preferred_element_type=F32)
    for h in range(N_HEADS):
        c_ref[h] = decay[:, h:h + 1] * c_st[h] + _dot(kw_l[h], v_ext[h])
    yield

    def lane_bcast(cols):
        return jnp.stack([jnp.broadcast_to(cols[:, h:h + 1], (L, LANES)) for h in range(N_HEADS)])

    mx_b = lane_bcast(mx)
    m_prev_b = jnp.stack([jnp.broadcast_to(m_prev[:, h:h + 1], (1, LANES)) for h in range(N_HEADS)])
    w_inter = jnp.exp(m_prev_b - mx_b)
    pmat = jnp.exp(jnp.where(tril[None], a_r[:, None, :] - mx_b, NEG))
    s = qk_raw * pmat
    nd = (jnp.concatenate([w_inter, w_inter], axis=-1) * inter
          + jnp.einsum('hqk,hke->hqe', s.astype(BF16), v_ext, preferred_element_type=F32))
    floor = jnp.exp(-lane_bcast(b + mx))
    yield

    hh = nd[:, :, :HEAD_DV] / jnp.maximum(jnp.abs(nd[:, :, HEAD_DV:]), floor)
    ms = _dot((hh * hh).astype(BF16).reshape(N_HEADS * L, HEAD_DV), ones_b).reshape(hh.shape) * (1.0 / HEAD_DV)
    yield

    hn = hh * lax.rsqrt(ms + EPS)
    merged = jnp.concatenate([hn[h] for h in range(N_HEADS)], axis=-1) * nw_ref[...]
    out_ref[...] = (merged * _sigmoid(o_ref[...].astype(F32))).astype(out_ref.dtype)


def mlstm_mix(proj, gates, conv_w, conv_b, gate_bias, norm_w, *, bsz, seq):
    nblk = seq // CHUNK
    dv = N_HEADS * HEAD_DV
    ns = _seqs_per_step(bsz)
    proj = proj.reshape(bsz, seq, proj.shape[-1])
    gates = gates.reshape(bsz, seq, 2 * LANES)
    out = pl.pallas_call(
        _mlstm_kernel,
        grid=(bsz // ns, nblk),
        in_specs=[pl.BlockSpec((ns, CHUNK, D), lambda b, s: (b, s, 0)),
                  pl.BlockSpec((ns, CHUNK, dv), lambda b, s: (b, s, 1)),
                  pl.BlockSpec((ns, CHUNK, dv), lambda b, s: (b, s, 2)),
                  pl.BlockSpec((ns, CHUNK, 2 * LANES), lambda b, s: (b, s, 0)),
                  pl.BlockSpec((CONV_K, D), lambda b, s: (0, 0)),
                  pl.BlockSpec((1, D), lambda b, s: (0, 0)),
                  pl.BlockSpec((1, 2 * LANES), lambda b, s: (0, 0)),
                  pl.BlockSpec((1, dv), lambda b, s: (0, 0))],
        out_specs=pl.BlockSpec((ns, CHUNK, dv), lambda b, s: (b, s, 0)),
        out_shape=jax.ShapeDtypeStruct((bsz, seq, dv), BF16),
        scratch_shapes=[pltpu.VMEM((ns, SUBLANES, D), F32),
                        pltpu.VMEM((ns, N_HEADS, LANES, 2 * HEAD_DV), F32),
                        pltpu.VMEM((ns, 1, LANES), F32)],
        compiler_params=_cparams("arbitrary", "arbitrary"),
        name="mlstm_mix",
    )(proj, proj, proj, gates, conv_w, conv_b, gate_bias, norm_w)
    return out.reshape(bsz * seq, dv)


def _hgrn_kernel(q_ref, f_ref, v_ref, g_ref, lb_ref, nw_ref, out_ref, st_ref):
    @pl.when(pl.program_id(1) == 0)
    def _():
        st_ref[...] = jnp.zeros_like(st_ref)

    _interleave([_hgrn_chunk(q_ref.at[i], f_ref.at[i], v_ref.at[i], g_ref.at[i], lb_ref, nw_ref, out_ref.at[i],
                             st_ref.at[i]) for i in range(q_ref.shape[0])])


def _hgrn_chunk(q_ref, f_ref, v_ref, g_ref, lb_ref, nw_ref, out_ref, st_ref):
    L = CHUNK

    lb = lb_ref[...]
    f = f_ref[...].astype(F32)
    e = jnp.exp(-jnp.abs(f))
    r = 1.0 / (1.0 + e)
    er = e * r
    sig = jnp.where(f >= 0.0, r, er)
    nsig = jnp.where(f >= 0.0, er, r)
    dec_all = jnp.maximum(lb + (1.0 - lb) * sig, F_FLOOR)
    k_all = (1.0 - lb) * nsig
    q_all = _silu(q_ref[...].astype(F32))
    row = lax.broadcasted_iota(jnp.int32, (L, L), 0)
    col = lax.broadcasted_iota(jnp.int32, (L, L), 1)
    xor = row ^ col
    eye = row == col
    ones_b = jnp.ones((LANES, LANES), BF16)
    sub8 = lax.broadcasted_iota(jnp.int32, (1, SUBLANES, LANES), 1)
    H = N_HEADS
    G = L // SUBLANES

    def heads(x):
        return jnp.stack([x[:, h * HEAD_DV:(h + 1) * HEAD_DV] for h in range(H)])

    def bmm_nt(x, y):
        return jnp.einsum('hqd,hkd->hqk', x.astype(BF16), y.astype(BF16), preferred_element_type=F32)

    def rot8(x3, d):
        return pltpu.roll(x3, d % SUBLANES, axis=1)

    def level_mask(b):
        return ((xor >= b) & (xor < 2 * b))[None]

    q, k, dec = heads(q_all), heads(k_all), heads(dec_all)
    vb = heads(v_ref[...]).astype(BF16)
    st = st_ref[...]

    diag = _dot((q * k).astype(BF16).reshape(H * L, HEAD_DV), ones_b).reshape(H, L, L)

    n8 = H * G
    e8 = dec.reshape(n8, SUBLANES, LANES)
    q8 = q.reshape(n8, SUBLANES, LANES)
    k8 = k.reshape(n8, SUBLANES, LANES)
    ew = [e8]
    for j in range(1, 4):
        ew.append(ew[-1] * rot8(e8, j))
    eu = [rot8(e8, -1)]
    for j in range(1, 3):
        eu.append(eu[-1] * rot8(e8, -1 - j))
    pos2, pos4 = sub8 % 2, sub8 % 4
    pre2 = jnp.where(pos2 == 0, ew[0], ew[1])
    suf2 = jnp.where(pos2 == 0, eu[0], 1.0)
    pre4 = jnp.where(pos4 == 0, ew[0], jnp.where(pos4 == 1, ew[1], jnp.where(pos4 == 2, ew[2], ew[3])))
    suf4 = jnp.where(pos4 == 0, eu[2], jnp.where(pos4 == 1, eu[1], jnp.where(pos4 == 2, eu[0], 1.0)))
    small = [
        (4, jnp.where(sub8 >= 4, pre4, 0.0), jnp.where(sub8 < 4, suf4, 0.0)),
        (2, jnp.where(pos4 >= 2, pre2, 0.0), jnp.where(pos4 < 2, suf2, 0.0)),
        (1, jnp.where(pos2 == 1, e8, 0.0), jnp.where(pos2 == 0, 1.0, 0.0)),
    ]
    levels = [(b, bmm_nt((q8 * fq).reshape(H, L, LANES), (k8 * fk).reshape(H, L, LANES))) for b, fq, fk in small]

    pre = pre4 * jnp.where(sub8 >= 4, pre4[:, 3:4, :], 1.0)
    suf = suf4 * jnp.where(sub8 < 4, pre4[:, 7:8, :], 1.0)
    tot = pre[:, 7:8, :]
    q6 = q.reshape(H, G, 1, 1, SUBLANES, LANES)
    k6 = k.reshape(H, G, 1, 1, SUBLANES, LANES)
    m = 1
    while m < G:
        nb = G // (2 * m)
        shape = (H, nb, 2, m, SUBLANES, LANES)
        pre, suf, q6, k6 = (x.reshape(shape) for x in (pre, suf, q6, k6))
        tot = tot.reshape(H, nb, 2, 1, 1, LANES)
        zeros = jnp.zeros((H, nb, 1, m, SUBLANES, LANES), F32)
        q_l = jnp.concatenate([zeros, q6[:, :, 1:] * pre[:, :, 1:]], axis=2)
        k_l = jnp.concatenate([k6[:, :, :1] * suf[:, :, :1], zeros], axis=2)
        levels.append((SUBLANES * m, bmm_nt(q_l.reshape(H, L, LANES), k_l.reshape(H, L, LANES))))
        pre = jnp.concatenate([pre[:, :, :1], pre[:, :, 1:] * tot[:, :, :1]], axis=2)
        suf = jnp.concatenate([suf[:, :, :1] * tot[:, :, 1:], suf[:, :, 1:]], axis=2)
        tot = tot[:, :, :1] * tot[:, :, 1:]
        m *= 2
    pre = pre.reshape(H, L, LANES)
    suf = suf.reshape(H, L, LANES)
    tot = tot.reshape(H, 1, LANES)

    o_inter = bmm_nt(q * pre, st)
    k_dec = (k * suf).astype(BF16)
    for h in range(H):
        st_ref[h] = st[h] * tot[h] + lax.dot_general(
            vb[h], k_dec[h], (((0,), (0,)), ((), ())), preferred_element_type=F32)
    yield

    a = jnp.where(eye[None], diag, 0.0)
    for b, a_l in levels:
        a = jnp.where(level_mask(b), a_l, a)
    o = o_inter + jnp.einsum('hqk,hkv->hqv', a.astype(BF16), vb, preferred_element_type=F32)
    yield

    ms = _dot((o * o).astype(BF16).reshape(H * L, HEAD_DV), ones_b).reshape(o.shape) * (1.0 / HEAD_DV)
    yield

    on = o * lax.rsqrt(ms + EPS)
    merged = jnp.concatenate([on[h] for h in range(H)], axis=-1) * nw_ref[...]
    out_ref[...] = (merged * _silu(g_ref[...].astype(F32))).astype(out_ref.dtype)


def _seqs_per_step(bsz):
    return next(n for n in (4, 2, 1) if bsz % n == 0)


def hgrn_mix(proj, lb, norm_w, *, bsz, seq):
    nblk = seq // CHUNK
    ns = _seqs_per_step(bsz)
    proj = proj.reshape(bsz, seq, proj.shape[-1])
    out = pl.pallas_call(
        _hgrn_kernel,
        grid=(bsz // ns, nblk),
        in_specs=[pl.BlockSpec((ns, CHUNK, D), lambda b, s: (b, s, 0)),
                  pl.BlockSpec((ns, CHUNK, D), lambda b, s: (b, s, 1)),
                  pl.BlockSpec((ns, CHUNK, D), lambda b, s: (b, s, 2)),
                  pl.BlockSpec((ns, CHUNK, D), lambda b, s: (b, s, 3)),
                  pl.BlockSpec((1, D), lambda b, s: (0, 0)),
                  pl.BlockSpec((1, D), lambda b, s: (0, 0))],
        out_specs=pl.BlockSpec((ns, CHUNK, D), lambda b, s: (b, s, 0)),
        out_shape=jax.ShapeDtypeStruct((bsz, seq, D), BF16),
        scratch_shapes=[pltpu.VMEM((ns, N_HEADS, HEAD_DV, HEAD_DV), F32)],
        compiler_params=_cparams("arbitrary", "arbitrary"),
        name="hgrn_mix",
    )(proj, proj, proj, proj, lb, norm_w)
    return out.reshape(bsz * seq, D)


def _mixer_residual(x_ref, y_ref, wo_ref, mod_ref):
    return x_ref[...] + _mod(mod_ref, GATE1) * _dot(y_ref[...], wo_ref[0])


def _ffn_kernel(x_ref, y_ref, wo_ref, mod_ref, nw_ref, wg_ref, wu_ref, wd_ref, o_ref, *, tf):
    x1 = _mixer_residual(x_ref, y_ref, wo_ref, mod_ref)
    hb = _norm_mod(x1, nw_ref[...], _mod(mod_ref, SCALE2), _mod(mod_ref, SHIFT2)).astype(BF16)
    acc = None
    for f0 in range(0, wg_ref.shape[2], tf):
        hid = _silu(_dot(hb, wg_ref[0, :, f0:f0 + tf])) * _dot(hb, wu_ref[0, :, f0:f0 + tf])
        part = _dot(hid.astype(BF16), wd_ref[0, f0:f0 + tf, :])
        acc = part if acc is None else acc + part
    o_ref[...] = x1 + _mod(mod_ref, GATE2) * acc


def ffn(x, y, w_out, mod, norm_w, wg, wu, wd, *, layer, seq, tm, tf):
    t = x.shape[0]
    dff = wg.shape[2]
    per_b = seq // tm
    resident = dict(pipeline_mode=pl.Buffered(1))
    return pl.pallas_call(
        functools.partial(_ffn_kernel, tf=tf),
        grid=(t // tm,),
        in_specs=[pl.BlockSpec((tm, D), lambda i: (i, 0)),
                  pl.BlockSpec((tm, D), lambda i: (i, 0)),
                  pl.BlockSpec((1, D, D), lambda i: (layer, 0, 0), **resident),
                  pl.BlockSpec((1, MOD_ROWS, D), lambda i: (i // per_b, 0, 0)),
                  pl.BlockSpec((1, D), lambda i: (0, 0)),
                  pl.BlockSpec((1, D, dff), lambda i: (layer, 0, 0), **resident),
                  pl.BlockSpec((1, D, dff), lambda i: (layer, 0, 0), **resident),
                  pl.BlockSpec((1, dff, D), lambda i: (layer, 0, 0), **resident)],
        out_specs=pl.BlockSpec((tm, D), lambda i: (i, 0)),
        out_shape=jax.ShapeDtypeStruct((t, D), F32),
        compiler_params=_cparams("arbitrary"),
        name="ffn",
    )(x, y, w_out, mod, norm_w, wg, wu, wd)


def _router_kernel(x_ref, y_ref, wo_ref, mod_ref, nw_ref, wr_ref, x1_ref, hn_ref, rw_ref, ri_ref):
    x1 = _mixer_residual(x_ref, y_ref, wo_ref, mod_ref)
    x1_ref[...] = x1
    hn = _norm_mod(x1, nw_ref[...], _mod(mod_ref, SCALE2), _mod(mod_ref, SHIFT2))
    _to_tile_rows(hn_ref, hn)
    h_hi = hn.astype(BF16)
    h_lo = (hn - h_hi.astype(F32)).astype(BF16)
    hw = _dot(h_hi, wr_ref[...])
    logits = hw[:, :LANES] + hw[:, LANES:] + _dot(h_lo, wr_ref[:, :LANES])
    lane = lax.broadcasted_iota(jnp.int32, logits.shape, 1)
    lg = jnp.where(lane < N_EXPERTS, logits, NEG)
    m1 = jnp.max(lg, axis=-1, keepdims=True)
    i1 = jnp.min(jnp.where(lg == m1, lane, LANES), axis=-1, keepdims=True)
    lg2 = jnp.where(lane == i1, NEG, lg)
    m2 = jnp.max(lg2, axis=-1, keepdims=True)
    i2 = jnp.min(jnp.where(lg2 == m2, lane, LANES), axis=-1, keepdims=True)
    e2 = jnp.exp(m2 - m1)
    w1 = 1.0 / (1.0 + e2)
    rw_ref[...] = jnp.where(lane == 0, w1, jnp.where(lane == 1, e2 * w1, 0.0))
    ri_ref[...] = jnp.where(lane == 0, i1, jnp.where(lane == 1, i2, 0))


def router(x, y, w_out, mod, norm_w, wr, *, layer, seq, tm):
    t = x.shape[0]
    per_b = seq // tm
    return pl.pallas_call(
        _router_kernel,
        grid=(t // tm,),
        in_specs=[pl.BlockSpec((tm, D), lambda i: (i, 0)),
                  pl.BlockSpec((tm, D), lambda i: (i, 0)),
                  pl.BlockSpec((1, D, D), lambda i: (layer, 0, 0), pipeline_mode=pl.Buffered(1)),
                  pl.BlockSpec((1, MOD_ROWS, D), lambda i: (i // per_b, 0, 0)),
                  pl.BlockSpec((1, D), lambda i: (0, 0)),
                  pl.BlockSpec((D, 2 * LANES), lambda i: (0, 0))],
        out_specs=[pl.BlockSpec((tm, D), lambda i: (i, 0)),
                   pl.BlockSpec((tm * ROW_TILES, LANES), lambda i: (i, 0)),
                   pl.BlockSpec((tm, LANES), lambda i: (i, 0)),
                   pl.BlockSpec((tm, LANES), lambda i: (i, 0))],
        out_shape=[jax.ShapeDtypeStruct((t, D), F32),
                   jax.ShapeDtypeStruct((t * ROW_TILES, LANES), F32),
                   jax.ShapeDtypeStruct((t, LANES), F32),
                   jax.ShapeDtypeStruct((t, LANES), jnp.int32)],
        compiler_params=_cparams("arbitrary"),
        name="router",
    )(x, y, w_out, mod, norm_w, wr)


def _to_tile_rows(ref, val):
    rows = val.shape[0]
    for s in range(ROW_TILES):
        ref[pl.ds(s, rows, stride=ROW_TILES), :] = val[:, s * LANES:(s + 1) * LANES]


def _from_tile_rows(ref, rows):
    return jnp.concatenate([ref[pl.ds(s, rows, stride=ROW_TILES), :] for s in range(ROW_TILES)], axis=-1)


def _tile_row(ref, row):
    return ref.at[pl.ds(pl.multiple_of(row * ROW_TILES, ROW_TILES), ROW_TILES)]


def _for_each_row_pair(tm, fn):
    def body(row, carry):
        for k in range(2):
            fn(row, 2 * row + k, k)
        return carry

    lax.fori_loop(0, tm, body, 0, unroll=8)


def _dispatch_kernel(ends_ref, dest_ref, hn_ref, xs_ref, zero_ref, stage_ref, sem):
    tm = hn_ref.shape[0] // ROW_TILES
    tgr = zero_ref.shape[0]
    i = pl.program_id(0)
    cur = lax.rem(i, 2)

    @pl.when(i == 0)
    def _():
        zero_ref[...] = jnp.zeros_like(zero_ref)

        def zero_tile(start):
            cp = pltpu.make_async_copy(zero_ref, xs_ref.at[pl.ds(pl.multiple_of(start, tgr), tgr)], sem.at[2])
            cp.start()
            cp.wait()

        n_tiles = xs_ref.shape[0] // tgr
        for e in range(N_EXPERTS):
            @pl.when(ends_ref[N_EXPERTS + e] > 0)
            def _():
                zero_tile(ends_ref[e] * ROW_TILES - tgr)

            @pl.when(n_tiles - 1 - e >= ends_ref[2 * N_EXPERTS])
            def _():
                zero_tile((n_tiles - 1 - e) * tgr)

    stage_ref[cur] = hn_ref[...]

    def scatter(row, slot, k):
        pltpu.make_async_copy(_tile_row(stage_ref.at[cur], row), _tile_row(xs_ref, dest_ref[0, 0, slot]),
                              sem.at[cur]).start(priority=k)

    _for_each_row_pair(tm, scatter)

    def drain(buf_slot):
        for _ in range(2):
            pltpu.make_async_copy(stage_ref.at[buf_slot], xs_ref.at[pl.ds(0, tm * ROW_TILES)],
                                  sem.at[buf_slot]).wait()

    @pl.when(i > 0)
    def _():
        drain(1 - cur)

    @pl.when(i == pl.num_programs(0) - 1)
    def _():
        drain(cur)


def dispatch(ends_padded, dest, hn, *, n_rows, tm, tg):
    t = hn.shape[0] // ROW_TILES
    grid_spec = pltpu.PrefetchScalarGridSpec(
        num_scalar_prefetch=1,
        grid=(t // tm,),
        in_specs=[pl.BlockSpec((1, 1, 2 * tm), lambda i, ends: (i, 0, 0), memory_space=pltpu.SMEM),
                  pl.BlockSpec((tm * ROW_TILES, LANES), lambda i, ends: (i, 0))],
        out_specs=pl.BlockSpec(memory_space=pl.ANY),
        scratch_shapes=[pltpu.VMEM((tg * ROW_TILES, LANES), F32),
                        pltpu.VMEM((2, tm * ROW_TILES, LANES), F32),
                        pltpu.SemaphoreType.DMA((3,))],
    )
    return pl.pallas_call(
        _dispatch_kernel,
        grid_spec=grid_spec,
        out_shape=jax.ShapeDtypeStruct((n_rows * ROW_TILES, LANES), F32),
        compiler_params=_cparams("arbitrary"),
        name="moe_dispatch",
    )(ends_padded, dest, hn)


def _expert_ffn_kernel(te_ref, nu_ref, x_ref, wg_ref, wu_ref, wd_ref, o_ref):
    del te_ref
    i = pl.program_id(0)

    @pl.when(i < nu_ref[0])
    def _():
        hb = _from_tile_rows(x_ref, x_ref.shape[0] // ROW_TILES).astype(BF16)
        hid = _silu(_dot(hb, wg_ref[0, 0])) * _dot(hb, wu_ref[0, 0])
        _to_tile_rows(o_ref, _dot(hid.astype(BF16), wd_ref[0, 0]))

    @pl.when(i >= nu_ref[0])
    def _():
        o_ref[...] = jnp.zeros_like(o_ref)


def expert_ffn(tile_expert, n_used, xs, wg, wu, wd, *, layer, tg):
    r = xs.shape[0] // ROW_TILES
    dfe = wg.shape[3]
    grid_spec = pltpu.PrefetchScalarGridSpec(
        num_scalar_prefetch=2,
        grid=(r // tg,),
        in_specs=[pl.BlockSpec((tg * ROW_TILES, LANES), lambda i, te, nu: (jnp.minimum(i, nu[0] - 1), 0)),
                  pl.BlockSpec((1, 1, D, dfe), lambda i, te, nu: (layer, te[i], 0, 0)),
                  pl.BlockSpec((1, 1, D, dfe), lambda i, te, nu: (layer, te[i], 0, 0)),
                  pl.BlockSpec((1, 1, dfe, D), lambda i, te, nu: (layer, te[i], 0, 0))],
        out_specs=pl.BlockSpec((tg * ROW_TILES, LANES), lambda i, te, nu: (i, 0)),
    )
    return pl.pallas_call(
        _expert_ffn_kernel,
        grid_spec=grid_spec,
        out_shape=jax.ShapeDtypeStruct((r * ROW_TILES, LANES), F32),
        compiler_params=_cparams("arbitrary"),
        name="moe_expert_ffn",
    )(tile_expert, n_used, xs, wg, wu, wd)


def _combine_kernel(dest_ref, dest_next_ref, x_ref, mod_ref, rw_ref, fw_ref, ys_ref, o_ref, buf_ref, sem, *, final):
    tm = x_ref.shape[0]
    i = pl.program_id(0)
    cur = lax.rem(i, 2)

    def issue(idx_ref, buf_slot):
        def gather(row, slot, k):
            pltpu.make_async_copy(_tile_row(ys_ref, idx_ref[0, 0, slot]), _tile_row(buf_ref.at[buf_slot, k], row),
                                  sem.at[buf_slot]).start(priority=k)

        _for_each_row_pair(tm, gather)

    @pl.when(i == 0)
    def _():
        issue(dest_ref, 0)

    @pl.when(i + 1 < pl.num_programs(0))
    def _():
        issue(dest_next_ref, 1 - cur)

    for k in range(2):
        pltpu.make_async_copy(ys_ref.at[pl.ds(0, tm * ROW_TILES)], buf_ref.at[cur, k], sem.at[cur]).wait()
    rw = rw_ref[...]
    y = (rw[:, 0:1] * _from_tile_rows(buf_ref.at[cur, 0], tm)
         + rw[:, 1:2] * _from_tile_rows(buf_ref.at[cur, 1], tm))
    out = x_ref[...] + _mod(mod_ref, GATE2) * y
    if final:
        out = out * lax.rsqrt(jnp.mean(out * out, axis=-1, keepdims=True) + EPS) * fw_ref[...]
    o_ref[...] = out


def combine(dest, x, mod, rw, final_w, ys, *, seq, tm, final):
    t = x.shape[0]
    per_b = seq // tm
    last = t // tm - 1
    return pl.pallas_call(
        functools.partial(_combine_kernel, final=final),
        grid=(t // tm,),
        in_specs=[pl.BlockSpec((1, 1, 2 * tm), lambda i: (i, 0, 0), memory_space=pltpu.SMEM),
                  pl.BlockSpec((1, 1, 2 * tm), lambda i: (jnp.minimum(i + 1, last), 0, 0), memory_space=pltpu.SMEM),
                  pl.BlockSpec((tm, D), lambda i: (i, 0)),
                  pl.BlockSpec((1, MOD_ROWS, D), lambda i: (i // per_b, 0, 0)),
                  pl.BlockSpec((tm, LANES), lambda i: (i, 0)),
                  pl.BlockSpec((1, D), lambda i: (0, 0)),
                  pl.BlockSpec(memory_space=pl.ANY)],
        out_specs=pl.BlockSpec((tm, D), lambda i: (i, 0)),
        out_shape=jax.ShapeDtypeStruct((t, D), F32),
        scratch_shapes=[pltpu.VMEM((2, 2, tm * ROW_TILES, LANES), F32), pltpu.SemaphoreType.DMA((2,))],
        compiler_params=_cparams("arbitrary"),
        name="moe_combine",
    )(dest, dest, x, mod, rw, final_w, ys)


def _routing_plan(route_i, *, tm, tg):
    t = route_i.shape[0]
    e_flat = route_i[:, :2].reshape(-1)
    onehot = (e_flat[:, None] == jnp.arange(N_EXPERTS, dtype=jnp.int32)[None, :]).astype(jnp.int32)
    csum = jnp.cumsum(onehot, axis=0)
    counts = csum[-1]
    padded = ((counts + tg - 1) // tg) * tg
    ends = jnp.cumsum(padded)
    starts = ends - padded
    dest = jnp.sum(onehot * (starts[None, :] + csum - onehot), axis=1)
    n_tiles = (2 * t) // tg + N_EXPERTS
    tile_start = jnp.arange(n_tiles, dtype=jnp.int32) * tg
    tile_expert = jnp.sum((tile_start[:, None] >= ends[None, :]).astype(jnp.int32), axis=1)
    n_used = (ends[-1] // tg).astype(jnp.int32)
    last_expert = jnp.take(tile_expert, jnp.maximum(n_used - 1, 0))
    tile_expert = jnp.where(tile_start < ends[-1], tile_expert, last_expert).astype(jnp.int32)
    ends_padded = jnp.concatenate([ends, padded, n_used.reshape(1)]).astype(jnp.int32)
    return (dest.reshape(t // tm, 1, 2 * tm).astype(jnp.int32), tile_expert, n_used.reshape(1), ends_padded,
            n_tiles * tg)


def moe(x, y, w_out, mod, norm_w, wr, wg, wu, wd, final_w, *, layer, seq, tm, tg, final):
    x1, hn, rw, ri = router(x, y, w_out, mod, norm_w, wr, layer=layer, seq=seq, tm=tm)
    dest, tile_expert, n_used, ends_padded, n_rows = _routing_plan(ri, tm=tm, tg=tg)
    xs = dispatch(ends_padded, dest, hn, n_rows=n_rows, tm=tm, tg=tg)
    ys = expert_ffn(tile_expert, n_used, xs, wg, wu, wd, layer=layer, tg=tg)
    return combine(dest, x1, mod, rw, final_w, ys, seq=seq, tm=tm, final=final)


def _pad_cols(w, n):
    return jnp.pad(w, ((0, 0), (0, n - w.shape[1])))


def kernel(x, c, ada_w, ada_b, norm1_w, norm2_w, final_norm_w, m_w_in, m_i_bias, m_f_bias, m_conv_w, m_conv_b, m_norm_w, m_w_out, h_w_in, h_lb_logits, h_norm_w, h_w_out, ffn_w_gate, ffn_w_up, ffn_w_down, moe_router, moe_w_gate, moe_w_up, moe_w_down):
    bsz, seq, _ = x.shape
    t = bsz * seq
    tiles = _tiles(seq)
    tm, tm_in = tiles['tm'], tiles['tm_in']
    xt = x.reshape(t, D)

    mod_all = ada_modulation(c, ada_w, ada_b)
    mod_all = jnp.pad(mod_all.reshape(DEPTH, bsz, N_MOD, D), ((0, 0), (0, 0), (0, MOD_ROWS - N_MOD), (0, 0)))

    gam = jax.nn.softmax(h_lb_logits.astype(F32), axis=0)
    lower_bounds = jnp.cumsum(gam, axis=0) - gam[:1]

    n_main = 2 * N_HEADS * M_DQK + 2 * N_HEADS * HEAD_DV
    final_w = final_norm_w.reshape(1, D)
    m_w_in_b, h_w_in_b = m_w_in.astype(BF16), h_w_in.astype(BF16)
    m_w_out_b, h_w_out_b = m_w_out.astype(BF16), h_w_out.astype(BF16)
    ffn_g, ffn_u, ffn_d = ffn_w_gate.astype(BF16), ffn_w_up.astype(BF16), ffn_w_down.astype(BF16)
    moe_g, moe_u, moe_d = moe_w_gate.astype(BF16), moe_w_up.astype(BF16), moe_w_down.astype(BF16)
    for layer in range(DEPTH):
        j = layer // 2
        mod = mod_all[layer]
        n1 = norm1_w[layer].reshape(1, D)
        n2 = norm2_w[layer].reshape(1, D)
        if layer % 2 == 0:
            w_gates = jnp.concatenate([_pad_cols(m_w_in[j, :, n_main:n_main + N_HEADS], LANES),
                                       _pad_cols(m_w_in[j, :, n_main + N_HEADS:], LANES)], axis=1).astype(BF16)
            proj, gates = in_proj(xt, mod, n1, m_w_in_b, w_gates, layer=j, n=n_main,
                                  seq=seq, tm=tm_in, tn=n_main // 2)
            gate_bias = jnp.concatenate([_pad_cols(m_i_bias[j].reshape(1, -1), LANES),
                                         _pad_cols(m_f_bias[j].reshape(1, -1), LANES)], axis=1)
            y = mlstm_mix(proj, gates, m_conv_w[j], m_conv_b[j].reshape(1, -1), gate_bias,
                          m_norm_w[j].reshape(1, -1), bsz=bsz, seq=seq)
            xt = ffn(xt, y, m_w_out_b, mod, n2, ffn_g, ffn_u, ffn_d, layer=j, seq=seq, tm=tm,
                     tf=ffn_w_gate.shape[2] // 2)
        else:
            n_h = h_w_in.shape[2]
            (proj,) = in_proj(xt, mod, n1, h_w_in_b, None, layer=j, n=n_h, seq=seq, tm=tm_in, tn=n_h // 2)
            y = hgrn_mix(proj, lower_bounds[j].reshape(1, D), h_norm_w[j].reshape(1, D), bsz=bsz, seq=seq)
            wr = _pad_cols(moe_router[j], LANES)
            wr_hi = wr.astype(BF16)
            wr_split = jnp.concatenate([wr_hi, (wr - wr_hi.astype(F32)).astype(BF16)], axis=1)
            xt = moe(xt, y, h_w_out_b, mod, n2, wr_split, moe_g, moe_u, moe_d, final_w,
                     layer=j, seq=seq, tm=tm, tg=tm, final=layer == DEPTH - 1)
    return xt.reshape(bsz, seq, D)
```

```python
import functools

import jax
import jax.numpy as jnp
from jax import lax
from jax.experimental import pallas as pl
from jax.experimental.pallas import tpu as pltpu

F32 = jnp.float32
BF16 = jnp.bfloat16

DEPTH = 4
D = 1024
N_HEADS = 8
HEAD_DV = 128
M_DQK = 64
CONV_K = 4
IGATE_CAP = 15.0
N_EXPERTS = 8
EPS = 1e-6
NEG = -1e30
F_FLOOR = 1e-30
SHIFT1, SCALE1, GATE1, SHIFT2, SCALE2, GATE2 = range(6)
N_MOD = 6
LANES = 128
SUBLANES = 8
ROW_TILES = D // LANES
MOD_ROWS = SUBLANES
CHUNK = 128
VMEM_LIMIT = 56 * 1024 * 1024


def _tiles(seq):
    return dict(tm=min(512, seq),
                tm_moe=min(1024, seq),
                tm_in=min(1024, seq))


def _cparams(*sem):
    return pltpu.CompilerParams(dimension_semantics=sem, vmem_limit_bytes=VMEM_LIMIT)


def _sigmoid(x):
    return 1.0 / (1.0 + jnp.exp(-x))


def _silu(x):
    return x * _sigmoid(x)


def _dot(a, b):
    return jnp.dot(a, b, preferred_element_type=F32)


def _dot_f32(a, b):
    return jnp.dot(a, b, preferred_element_type=F32, precision=lax.Precision.HIGHEST)


def _cumsum_rows(x):
    n = x.shape[0]
    tri = (lax.broadcasted_iota(jnp.int32, (n, n), 1) <= lax.broadcasted_iota(jnp.int32, (n, n), 0)).astype(BF16)
    x1 = x.astype(BF16)
    r1 = x - x1.astype(F32)
    x2 = r1.astype(BF16)
    x3 = (r1 - x2.astype(F32)).astype(BF16)
    return _dot(tri, x1) + _dot(tri, x2) + _dot(tri, x3)


def _mod(mod_ref, row):
    return mod_ref[0, row:row + 1, :]


def _norm_mod(x, norm_w, scale, shift):
    y = x * lax.rsqrt(jnp.mean(x * x, axis=-1, keepdims=True) + EPS)
    return (y * norm_w) * (1.0 + scale) + shift


def _ada_kernel(c_ref, w_ref, b_ref, o_ref):
    c = c_ref[...]
    o_ref[0] = _dot_f32(_silu(c), w_ref[0]) + b_ref[0]


def ada_modulation(c, ada_w, ada_b):
    bsz = c.shape[0]
    tn = N_MOD * D // 4
    return pl.pallas_call(
        _ada_kernel,
        grid=(DEPTH, N_MOD * D // tn),
        in_specs=[pl.BlockSpec((bsz, D), lambda l, j: (0, 0)),
                  pl.BlockSpec((1, D, tn), lambda l, j: (l, 0, j)),
                  pl.BlockSpec((1, 1, tn), lambda l, j: (l, 0, j))],
        out_specs=pl.BlockSpec((1, bsz, tn), lambda l, j: (l, 0, j)),
        out_shape=jax.ShapeDtypeStruct((DEPTH, bsz, N_MOD * D), F32),
        compiler_params=_cparams("arbitrary", "arbitrary"),
        name="ada_modulation",
    )(c, ada_w, ada_b.reshape(DEPTH, 1, N_MOD * D))


def _in_proj_kernel(x_ref, mod_ref, nw_ref, w_ref, *rest, has_extra):
    if has_extra:
        wx_ref, o_ref, ox_ref, hn_ref = rest
    else:
        o_ref, hn_ref = rest

    @pl.when(pl.program_id(1) == 0)
    def _():
        hn = _norm_mod(x_ref[...], nw_ref[...], _mod(mod_ref, SCALE1), _mod(mod_ref, SHIFT1))
        hn_ref[...] = hn.astype(BF16)
        if has_extra:
            ox_ref[...] = _dot(hn_ref[...], wx_ref[...])

    o_ref[...] = _dot(hn_ref[...], w_ref[0]).astype(o_ref.dtype)


def in_proj(x, mod, norm_w, w, w_extra, *, layer, n, seq, tm, tn):
    t = x.shape[0]
    per_b = seq // tm
    has_extra = w_extra is not None
    in_specs = [pl.BlockSpec((tm, D), lambda i, j: (i, 0)),
                pl.BlockSpec((1, MOD_ROWS, D), lambda i, j: (i // per_b, 0, 0)),
                pl.BlockSpec((1, D), lambda i, j: (0, 0)),
                pl.BlockSpec((1, D, tn), lambda i, j: (layer, 0, j))]
    out_specs = [pl.BlockSpec((tm, tn), lambda i, j: (i, j))]
    out_shape = [jax.ShapeDtypeStruct((t, n), BF16)]
    args = [x, mod, norm_w, w]
    if has_extra:
        nx = w_extra.shape[1]
        in_specs.append(pl.BlockSpec((D, nx), lambda i, j: (0, 0)))
        out_specs.append(pl.BlockSpec((tm, nx), lambda i, j: (i, 0)))
        out_shape.append(jax.ShapeDtypeStruct((t, nx), F32))
        args.append(w_extra)
    return pl.pallas_call(
        functools.partial(_in_proj_kernel, has_extra=has_extra),
        grid=(t // tm, n // tn),
        in_specs=in_specs,
        out_specs=out_specs,
        out_shape=out_shape,
        scratch_shapes=[pltpu.VMEM((tm, D), BF16)],
        compiler_params=_cparams("arbitrary", "arbitrary"),
        name="in_proj",
    )(*args)


def _interleave(chunks):
    live = list(chunks)
    while live:
        for gen in list(live):
            if next(gen, StopIteration) is StopIteration:
                live.remove(gen)


def _mlstm_kernel(qk_ref, v_ref, o_ref, g_ref, cw_ref, cb_ref, gb_ref, nw_ref, out_ref,
                  tail_ref, c_ref, m_ref):
    @pl.when(pl.program_id(1) == 0)
    def _():
        tail_ref[...] = jnp.zeros_like(tail_ref)
        c_ref[...] = jnp.zeros_like(c_ref)
        m_ref[...] = jnp.zeros_like(m_ref)

    _interleave([_mlstm_chunk(qk_ref.at[i], v_ref.at[i], o_ref.at[i], g_ref.at[i], cw_ref, cb_ref, gb_ref, nw_ref,
                              out_ref.at[i], tail_ref.at[i], c_ref.at[i], m_ref.at[i])
                 for i in range(qk_ref.shape[0])])


def _mlstm_chunk(qk_ref, v_ref, o_ref, g_ref, cw_ref, cb_ref, gb_ref, nw_ref, out_ref, tail_ref, c_ref, m_ref):
    L = CHUNK

    raw = qk_ref[...].astype(F32)
    xx = jnp.concatenate([tail_ref[...], raw], axis=0)
    tail_ref[...] = raw[L - SUBLANES:, :]
    y = cb_ref[...] + cw_ref[CONV_K - 1:CONV_K, :] * raw
    for j in range(CONV_K - 1):
        y = y + cw_ref[j:j + 1, :] * pltpu.roll(xx, CONV_K - 1 - j, axis=0)[SUBLANES:, :]
    qk = _silu(y)
    half = N_HEADS * M_DQK
    q_all = qk[:, :half]
    k_all = qk[:, half:] * (M_DQK ** -0.5)

    g = g_ref[...] + gb_ref[...]
    li = IGATE_CAP * jnp.tanh(g[:, :LANES] * (1.0 / IGATE_CAP))
    gf = g[:, LANES:]
    b = _cumsum_rows(jnp.minimum(gf, 0.0) - jnp.log1p(jnp.exp(-jnp.abs(gf))))
    tril = lax.broadcasted_iota(jnp.int32, (L, L), 1) <= lax.broadcasted_iota(jnp.int32, (L, L), 0)

    a = li - b
    amax8 = a.reshape(L // SUBLANES, SUBLANES, LANES)
    sub = lax.broadcasted_iota(jnp.int32, (1, SUBLANES, LANES), 1)
    for sh in (1, 2, 4):
        amax8 = jnp.where(sub >= sh, jnp.maximum(amax8, pltpu.roll(amax8, sh, axis=1)), amax8)
    groups = [amax8[0]]
    for v in range(1, L // SUBLANES):
        carry = jnp.broadcast_to(groups[-1][SUBLANES - 1:SUBLANES, :], (SUBLANES, LANES))
        groups.append(jnp.maximum(amax8[v], carry))
    amax = jnp.concatenate(groups, axis=0)
    m_prev = m_ref[...]
    mx = jnp.maximum(m_prev, amax)
    mx_last = mx[L - 1:L, :]
    m_ref[...] = b[L - 1:L, :] + mx_last
    decay = jnp.exp(m_prev - mx_last)
    a_r = a.T[0:N_HEADS, :]
    wk_r = jnp.exp(a - mx_last).T[0:N_HEADS, :]

    lane = lax.broadcasted_iota(jnp.int32, (L, LANES), 1)
    ones_b = jnp.ones((L, LANES), BF16)
    kt_pairs = [k_all[:, p * LANES:(p + 1) * LANES].T for p in range(N_HEADS // 2)]
    qb_l, kb_l, ve_l, kw_l = [], [], [], []
    for h in range(N_HEADS):
        p, hf = h // 2, h % 2
        in_head = (lane >= hf * M_DQK) & (lane < (hf + 1) * M_DQK)
        qb_l.append(jnp.where(in_head, q_all[:, p * LANES:(p + 1) * LANES], 0.0).astype(BF16))
        kb_l.append(k_all[:, p * LANES:(p + 1) * LANES].astype(BF16))
        ve_l.append(jnp.concatenate([v_ref[:, h * HEAD_DV:(h + 1) * HEAD_DV].astype(BF16), ones_b], axis=-1))
        kw_l.append((kt_pairs[p] * wk_r[h:h + 1, :]).astype(BF16))
    qb = jnp.stack(qb_l)
    kb = jnp.stack(kb_l)
    v_ext = jnp.stack(ve_l)
    c_st = c_ref[...]

    qk_raw = jnp.einsum('hqd,hkd->hqk', qb, kb, preferred_element_type=F32)
    inter = jnp.einsum('hqd,hde->hqe', qb, c_st.astype(BF16), preferred_element_type=F32)
    for h in range(N_HEADS):
        c_ref[h] = decay[:, h:h + 1] * c_st[h] + _dot(kw_l[h], v_ext[h])
    yield

    def lane_bcast(cols):
        return jnp.stack([jnp.broadcast_to(cols[:, h:h + 1], (L, LANES)) for h in range(N_HEADS)])

    mx_b = lane_bcast(mx)
    m_prev_b = jnp.stack([jnp.broadcast_to(m_prev[:, h:h + 1], (1, LANES)) for h in range(N_HEADS)])
    w_inter = jnp.exp(m_prev_b - mx_b)
    pmat = jnp.exp(jnp.where(tril[None], a_r[:, None, :] - mx_b, NEG))
    s = qk_raw * pmat
    nd = (jnp.concatenate([w_inter, w_inter], axis=-1) * inter
          + jnp.einsum('hqk,hke->hqe', s.astype(BF16), v_ext, preferred_element_type=F32))
    floor = jnp.exp(-lane_bcast(b + mx))
    yield

    hh = nd[:, :, :HEAD_DV] / jnp.maximum(jnp.abs(nd[:, :, HEAD_DV:]), floor)
    ms = _dot((hh * hh).astype(BF16).reshape(N_HEADS * L, HEAD_DV), ones_b).reshape(hh.shape) * (1.0 / HEAD_DV)
    yield

    hn = hh * lax.rsqrt(ms + EPS)
    merged = jnp.concatenate([hn[h] for h in range(N_HEADS)], axis=-1) * nw_ref[...]
    out_ref[...] = (merged * _sigmoid(o_ref[...].astype(F32))).astype(out_ref.dtype)


def mlstm_mix(proj, gates, conv_w, conv_b, gate_bias, norm_w, *, bsz, seq):
    nblk = seq // CHUNK
    dv = N_HEADS * HEAD_DV
    ns = _seqs_per_step(bsz)
    proj = proj.reshape(bsz, seq, proj.shape[-1])
    gates = gates.reshape(bsz, seq, 2 * LANES)
    out = pl.pallas_call(
        _mlstm_kernel,
        grid=(bsz // ns, nblk),
        in_specs=[pl.BlockSpec((ns, CHUNK, D), lambda b, s: (b, s, 0)),
                  pl.BlockSpec((ns, CHUNK, dv), lambda b, s: (b, s, 1)),
                  pl.BlockSpec((ns, CHUNK, dv), lambda b, s: (b, s, 2)),
                  pl.BlockSpec((ns, CHUNK, 2 * LANES), lambda b, s: (b, s, 0)),
                  pl.BlockSpec((CONV_K, D), lambda b, s: (0, 0)),
                  pl.BlockSpec((1, D), lambda b, s: (0, 0)),
                  pl.BlockSpec((1, 2 * LANES), lambda b, s: (0, 0)),
                  pl.BlockSpec((1, dv), lambda b, s: (0, 0))],
        out_specs=pl.BlockSpec((ns, CHUNK, dv), lambda b, s: (b, s, 0)),
        out_shape=jax.ShapeDtypeStruct((bsz, seq, dv), BF16),
        scratch_shapes=[pltpu.VMEM((ns, SUBLANES, D), F32),
                        pltpu.VMEM((ns, N_HEADS, LANES, 2 * HEAD_DV), F32),
                        pltpu.VMEM((ns, 1, LANES), F32)],
        compiler_params=_cparams("arbitrary", "arbitrary"),
        name="mlstm_mix",
    )(proj, proj, proj, gates, conv_w, conv_b, gate_bias, norm_w)
    return out.reshape(bsz * seq, dv)


def _hgrn_kernel(q_ref, f_ref, v_ref, g_ref, lb_ref, nw_ref, out_ref, st_ref):
    @pl.when(pl.program_id(1) == 0)
    def _():
        st_ref[...] = jnp.zeros_like(st_ref)

    _interleave([_hgrn_chunk(q_ref.at[i], f_ref.at[i], v_ref.at[i], g_ref.at[i], lb_ref, nw_ref, out_ref.at[i],
                             st_ref.at[i]) for i in range(q_ref.shape[0])])


def _hgrn_chunk(q_ref, f_ref, v_ref, g_ref, lb_ref, nw_ref, out_ref, st_ref):
    L = CHUNK

    lb = lb_ref[...]
    f = f_ref[...].astype(F32)
    e = jnp.exp(-jnp.abs(f))
    r = 1.0 / (1.0 + e)
    er = e * r
    sig = jnp.where(f >= 0.0, r, er)
    nsig = jnp.where(f >= 0.0, er, r)
    dec_all = jnp.maximum(lb + (1.0 - lb) * sig, F_FLOOR)
    k_all = (1.0 - lb) * nsig
    q_all = _silu(q_ref[...].astype(F32))
    row = lax.broadcasted_iota(jnp.int32, (L, L), 0)
    col = lax.broadcasted_iota(jnp.int32, (L, L), 1)
    xor = row ^ col
    eye = row == col
    ones_b = jnp.ones((LANES, LANES), BF16)
    sub8 = lax.broadcasted_iota(jnp.int32, (1, SUBLANES, LANES), 1)
    H = N_HEADS
    G = L // SUBLANES

    def heads(x):
        return jnp.stack([x[:, h * HEAD_DV:(h + 1) * HEAD_DV] for h in range(H)])

    def bmm_nt(x, y):
        return jnp.einsum('hqd,hkd->hqk', x.astype(BF16), y.astype(BF16), preferred_element_type=F32)

    def rot8(x3, d):
        return pltpu.roll(x3, d % SUBLANES, axis=1)

    def level_mask(b):
        return ((xor >= b) & (xor < 2 * b))[None]

    q, k, dec = heads(q_all), heads(k_all), heads(dec_all)
    vb = heads(v_ref[...]).astype(BF16)
    st = st_ref[...]

    diag = _dot((q * k).astype(BF16).reshape(H * L, HEAD_DV), ones_b).reshape(H, L, L)

    n8 = H * G
    e8 = dec.reshape(n8, SUBLANES, LANES)
    q8 = q.reshape(n8, SUBLANES, LANES)
    k8 = k.reshape(n8, SUBLANES, LANES)
    ew = [e8]
    for j in range(1, 4):
        ew.append(ew[-1] * rot8(e8, j))
    eu = [rot8(e8, -1)]
    for j in range(1, 3):
        eu.append(eu[-1] * rot8(e8, -1 - j))
    pos2, pos4 = sub8 % 2, sub8 % 4
    pre2 = jnp.where(pos2 == 0, ew[0], ew[1])
    suf2 = jnp.where(pos2 == 0, eu[0], 1.0)
    pre4 = jnp.where(pos4 == 0, ew[0], jnp.where(pos4 == 1, ew[1], jnp.where(pos4 == 2, ew[2], ew[3])))
    suf4 = jnp.where(pos4 == 0, eu[2], jnp.where(pos4 == 1, eu[1], jnp.where(pos4 == 2, eu[0], 1.0)))
    small = [
        (4, jnp.where(sub8 >= 4, pre4, 0.0), jnp.where(sub8 < 4, suf4, 0.0)),
        (2, jnp.where(pos4 >= 2, pre2, 0.0), jnp.where(pos4 < 2, suf2, 0.0)),
        (1, jnp.where(pos2 == 1, e8, 0.0), jnp.where(pos2 == 0, 1.0, 0.0)),
    ]
    levels = [(b, bmm_nt((q8 * fq).reshape(H, L, LANES), (k8 * fk).reshape(H, L, LANES))) for b, fq, fk in small]

    pre = pre4 * jnp.where(sub8 >= 4, pre4[:, 3:4, :], 1.0)
    suf = suf4 * jnp.where(sub8 < 4, pre4[:, 7:8, :], 1.0)
    tot = pre[:, 7:8, :]
    q6 = q.reshape(H, G, 1, 1, SUBLANES, LANES)
    k6 = k.reshape(H, G, 1, 1, SUBLANES, LANES)
    m = 1
    while m < G:
        nb = G // (2 * m)
        shape = (H, nb, 2, m, SUBLANES, LANES)
        pre, suf, q6, k6 = (x.reshape(shape) for x in (pre, suf, q6, k6))
        tot = tot.reshape(H, nb, 2, 1, 1, LANES)
        zeros = jnp.zeros((H, nb, 1, m, SUBLANES, LANES), F32)
        q_l = jnp.concatenate([zeros, q6[:, :, 1:] * pre[:, :, 1:]], axis=2)
        k_l = jnp.concatenate([k6[:, :, :1] * suf[:, :, :1], zeros], axis=2)
        levels.append((SUBLANES * m, bmm_nt(q_l.reshape(H, L, LANES), k_l.reshape(H, L, LANES))))
        pre = jnp.concatenate([pre[:, :, :1], pre[:, :, 1:] * tot[:, :, :1]], axis=2)
        suf = jnp.concatenate([suf[:, :, :1] * tot[:, :, 1:], suf[:, :, 1:]], axis=2)
        tot = tot[:, :, :1] * tot[:, :, 1:]
        m *= 2
    pre = pre.reshape(H, L, LANES)
    suf = suf.reshape(H, L, LANES)
    tot = tot.reshape(H, 1, LANES)

    o_inter = bmm_nt(q * pre, st)
    k_dec = (k * suf).astype(BF16)
    for h in range(H):
        st_ref[h] = st[h] * tot[h] + lax.dot_general(
            vb[h], k_dec[h], (((0,), (0,)), ((), ())), preferred_element_type=F32)
    yield

    a = jnp.where(eye[None], diag, 0.0)
    for b, a_l in levels:
        a = jnp.where(level_mask(b), a_l, a)
    o = o_inter + jnp.einsum('hqk,hkv->hqv', a.astype(BF16), vb, preferred_element_type=F32)
    yield

    ms = _dot((o * o).astype(BF16).reshape(H * L, HEAD_DV), ones_b).reshape(o.shape) * (1.0 / HEAD_DV)
    yield

    on = o * lax.rsqrt(ms + EPS)
    merged = jnp.concatenate([on[h] for h in range(H)], axis=-1) * nw_ref[...]
    out_ref[...] = (merged * _silu(g_ref[...].astype(F32))).astype(out_ref.dtype)


def _seqs_per_step(bsz):
    return next(n for n in (4, 2, 1) if bsz % n == 0)


def hgrn_mix(proj, lb, norm_w, *, bsz, seq):
    nblk = seq // CHUNK
    ns = _seqs_per_step(bsz)
    proj = proj.reshape(bsz, seq, proj.shape[-1])
    out = pl.pallas_call(
        _hgrn_kernel,
        grid=(bsz // ns, nblk),
        in_specs=[pl.BlockSpec((ns, CHUNK, D), lambda b, s: (b, s, 0)),
                  pl.BlockSpec((ns, CHUNK, D), lambda b, s: (b, s, 1)),
                  pl.BlockSpec((ns, CHUNK, D), lambda b, s: (b, s, 2)),
                  pl.BlockSpec((ns, CHUNK, D), lambda b, s: (b, s, 3)),
                  pl.BlockSpec((1, D), lambda b, s: (0, 0)),
                  pl.BlockSpec((1, D), lambda b, s: (0, 0))],
        out_specs=pl.BlockSpec((ns, CHUNK, D), lambda b, s: (b, s, 0)),
        out_shape=jax.ShapeDtypeStruct((bsz, seq, D), BF16),
        scratch_shapes=[pltpu.VMEM((ns, N_HEADS, HEAD_DV, HEAD_DV), F32)],
        compiler_params=_cparams("arbitrary", "arbitrary"),
        name="hgrn_mix",
    )(proj, proj, proj, proj, lb, norm_w)
    return out.reshape(bsz * seq, D)


def _mixer_residual(x_ref, y_ref, wo_ref, mod_ref):
    return x_ref[...] + _mod(mod_ref, GATE1) * _dot(y_ref[...], wo_ref[0])


def _ffn_kernel(x_ref, y_ref, wo_ref, mod_ref, nw_ref, wg_ref, wu_ref, wd_ref, o_ref, *, tf):
    x1 = _mixer_residual(x_ref, y_ref, wo_ref, mod_ref)
    hb = _norm_mod(x1, nw_ref[...], _mod(mod_ref, SCALE2), _mod(mod_ref, SHIFT2)).astype(BF16)
    acc = None
    for f0 in range(0, wg_ref.shape[2], tf):
        hid = _silu(_dot(hb, wg_ref[0, :, f0:f0 + tf])) * _dot(hb, wu_ref[0, :, f0:f0 + tf])
        part = _dot(hid.astype(BF16), wd_ref[0, f0:f0 + tf, :])
        acc = part if acc is None else acc + part
    o_ref[...] = x1 + _mod(mod_ref, GATE2) * acc


def ffn(x, y, w_out, mod, norm_w, wg, wu, wd, *, layer, seq, tm, tf):
    t = x.shape[0]
    dff = wg.shape[2]
    per_b = seq // tm
    resident = dict(pipeline_mode=pl.Buffered(1))
    return pl.pallas_call(
        functools.partial(_ffn_kernel, tf=tf),
        grid=(t // tm,),
        in_specs=[pl.BlockSpec((tm, D), lambda i: (i, 0)),
                  pl.BlockSpec((tm, D), lambda i: (i, 0)),
                  pl.BlockSpec((1, D, D), lambda i: (layer, 0, 0), **resident),
                  pl.BlockSpec((1, MOD_ROWS, D), lambda i: (i // per_b, 0, 0)),
                  pl.BlockSpec((1, D), lambda i: (0, 0)),
                  pl.BlockSpec((1, D, dff), lambda i: (layer, 0, 0), **resident),
                  pl.BlockSpec((1, D, dff), lambda i: (layer, 0, 0), **resident),
                  pl.BlockSpec((1, dff, D), lambda i: (layer, 0, 0), **resident)],
        out_specs=pl.BlockSpec((tm, D), lambda i: (i, 0)),
        out_shape=jax.ShapeDtypeStruct((t, D), F32),
        compiler_params=_cparams("arbitrary"),
        name="ffn",
    )(x, y, w_out, mod, norm_w, wg, wu, wd)


def _router_kernel(x_ref, y_ref, wo_ref, mod_ref, nw_ref, wr_ref, x1_ref, hn_ref, rw_ref, ri_ref):
    x1 = _mixer_residual(x_ref, y_ref, wo_ref, mod_ref)
    x1_ref[...] = x1
    hn = _norm_mod(x1, nw_ref[...], _mod(mod_ref, SCALE2), _mod(mod_ref, SHIFT2))
    _to_tile_rows(hn_ref, hn)
    h_hi = hn.astype(BF16)
    h_lo = (hn - h_hi.astype(F32)).astype(BF16)
    hw = _dot(h_hi, wr_ref[...])
    logits = hw[:, :LANES] + hw[:, LANES:] + _dot(h_lo, wr_ref[:, :LANES])
    lane = lax.broadcasted_iota(jnp.int32, logits.shape, 1)
    lg = jnp.where(lane < N_EXPERTS, logits, NEG)
    m1 = jnp.max(lg, axis=-1, keepdims=True)
    i1 = jnp.min(jnp.where(lg == m1, lane, LANES), axis=-1, keepdims=True)
    lg2 = jnp.where(lane == i1, NEG, lg)
    m2 = jnp.max(lg2, axis=-1, keepdims=True)
    i2 = jnp.min(jnp.where(lg2 == m2, lane, LANES), axis=-1, keepdims=True)
    e2 = jnp.exp(m2 - m1)
    w1 = 1.0 / (1.0 + e2)
    rw_ref[...] = jnp.where(lane == 0, w1, jnp.where(lane == 1, e2 * w1, 0.0))
    ri_ref[...] = jnp.where(lane == 0, i1, jnp.where(lane == 1, i2, 0))


def router(x, y, w_out, mod, norm_w, wr, *, layer, seq, tm):
    t = x.shape[0]
    per_b = seq // tm
    return pl.pallas_call(
        _router_kernel,
        grid=(t // tm,),
        in_specs=[pl.BlockSpec((tm, D), lambda i: (i, 0)),
                  pl.BlockSpec((tm, D), lambda i: (i, 0)),
                  pl.BlockSpec((1, D, D), lambda i: (layer, 0, 0), pipeline_mode=pl.Buffered(1)),
                  pl.BlockSpec((1, MOD_ROWS, D), lambda i: (i // per_b, 0, 0)),
                  pl.BlockSpec((1, D), lambda i: (0, 0)),
                  pl.BlockSpec((D, 2 * LANES), lambda i: (0, 0))],
        out_specs=[pl.BlockSpec((tm, D), lambda i: (i, 0)),
                   pl.BlockSpec((tm * ROW_TILES, LANES), lambda i: (i, 0)),
                   pl.BlockSpec((tm, LANES), lambda i: (i, 0)),
                   pl.BlockSpec((tm, LANES), lambda i: (i, 0))],
        out_shape=[jax.ShapeDtypeStruct((t, D), F32),
                   jax.ShapeDtypeStruct((t * ROW_TILES, LANES), F32),
                   jax.ShapeDtypeStruct((t, LANES), F32),
                   jax.ShapeDtypeStruct((t, LANES), jnp.int32)],
        compiler_params=_cparams("arbitrary"),
        name="router",
    )(x, y, w_out, mod, norm_w, wr)


def _to_tile_rows(ref, val):
    rows = val.shape[0]
    for s in range(ROW_TILES):
        ref[pl.ds(s, rows, stride=ROW_TILES), :] = val[:, s * LANES:(s + 1) * LANES]


def _from_tile_rows(ref, rows):
    return jnp.concatenate([ref[pl.ds(s, rows, stride=ROW_TILES), :] for s in range(ROW_TILES)], axis=-1)


def _tile_row(ref, row):
    return ref.at[pl.ds(pl.multiple_of(row * ROW_TILES, ROW_TILES), ROW_TILES)]


def _for_each_row_pair(tm, fn):
    def body(row, carry):
        for k in range(2):
            fn(row, 2 * row + k, k)
        return carry

    lax.fori_loop(0, tm, body, 0, unroll=8)


def _dispatch_kernel(ends_ref, dest_ref, hn_ref, xs_ref, zero_ref, stage_ref, sem):
    tm = hn_ref.shape[0] // ROW_TILES
    tgr = zero_ref.shape[0]
    i = pl.program_id(0)
    cur = lax.rem(i, 2)

    @pl.when(i == 0)
    def _():
        zero_ref[...] = jnp.zeros_like(zero_ref)

        def zero_tile(start):
            cp = pltpu.make_async_copy(zero_ref, xs_ref.at[pl.ds(pl.multiple_of(start, tgr), tgr)], sem.at[2])
            cp.start()
            cp.wait()

        n_tiles = xs_ref.shape[0] // tgr
        for e in range(N_EXPERTS):
            @pl.when(ends_ref[N_EXPERTS + e] > 0)
            def _():
                zero_tile(ends_ref[e] * ROW_TILES - tgr)

            @pl.when(n_tiles - 1 - e >= ends_ref[2 * N_EXPERTS])
            def _():
                zero_tile((n_tiles - 1 - e) * tgr)

    stage_ref[cur] = hn_ref[...]

    def scatter(row, slot, k):
        pltpu.make_async_copy(_tile_row(stage_ref.at[cur], row), _tile_row(xs_ref, dest_ref[0, 0, slot]),
                              sem.at[cur]).start(priority=k)

    _for_each_row_pair(tm, scatter)

    def drain(buf_slot):
        for _ in range(2):
            pltpu.make_async_copy(stage_ref.at[buf_slot], xs_ref.at[pl.ds(0, tm * ROW_TILES)],
                                  sem.at[buf_slot]).wait()

    @pl.when(i > 0)
    def _():
        drain(1 - cur)

    @pl.when(i == pl.num_programs(0) - 1)
    def _():
        drain(cur)


def dispatch(ends_padded, dest, hn, *, n_rows, tm, tg):
    t = hn.shape[0] // ROW_TILES
    grid_spec = pltpu.PrefetchScalarGridSpec(
        num_scalar_prefetch=1,
        grid=(t // tm,),
        in_specs=[pl.BlockSpec((1, 1, 2 * tm), lambda i, ends: (i, 0, 0), memory_space=pltpu.SMEM),
                  pl.BlockSpec((tm * ROW_TILES, LANES), lambda i, ends: (i, 0))],
        out_specs=pl.BlockSpec(memory_space=pl.ANY),
        scratch_shapes=[pltpu.VMEM((tg * ROW_TILES, LANES), F32),
                        pltpu.VMEM((2, tm * ROW_TILES, LANES), F32),
                        pltpu.SemaphoreType.DMA((3,))],
    )
    return pl.pallas_call(
        _dispatch_kernel,
        grid_spec=grid_spec,
        out_shape=jax.ShapeDtypeStruct((n_rows * ROW_TILES, LANES), F32),
        compiler_params=_cparams("arbitrary"),
        name="moe_dispatch",
    )(ends_padded, dest, hn)


def _expert_ffn_kernel(te_ref, nu_ref, x_ref, wg_ref, wu_ref, wd_ref, o_ref):
    del te_ref
    i = pl.program_id(0)

    @pl.when(i < nu_ref[0])
    def _():
        hb = _from_tile_rows(x_ref, x_ref.shape[0] // ROW_TILES).astype(BF16)
        hid = _silu(_dot(hb, wg_ref[0, 0])) * _dot(hb, wu_ref[0, 0])
        _to_tile_rows(o_ref, _dot(hid.astype(BF16), wd_ref[0, 0]))

    @pl.when(i >= nu_ref[0])
    def _():
        o_ref[...] = jnp.zeros_like(o_ref)


def expert_ffn(tile_expert, n_used, xs, wg, wu, wd, *, layer, tg):
    r = xs.shape[0] // ROW_TILES
    dfe = wg.shape[3]
    grid_spec = pltpu.PrefetchScalarGridSpec(
        num_scalar_prefetch=2,
        grid=(r // tg,),
        in_specs=[pl.BlockSpec((tg * ROW_TILES, LANES), lambda i, te, nu: (jnp.minimum(i, nu[0] - 1), 0)),
                  pl.BlockSpec((1, 1, D, dfe), lambda i, te, nu: (layer, te[i], 0, 0)),
                  pl.BlockSpec((1, 1, D, dfe), lambda i, te, nu: (layer, te[i], 0, 0)),
                  pl.BlockSpec((1, 1, dfe, D), lambda i, te, nu: (layer, te[i], 0, 0))],
        out_specs=pl.BlockSpec((tg * ROW_TILES, LANES), lambda i, te, nu: (i, 0)),
    )
    return pl.pallas_call(
        _expert_ffn_kernel,
        grid_spec=grid_spec,
        out_shape=jax.ShapeDtypeStruct((r * ROW_TILES, LANES), F32),
        compiler_params=_cparams("arbitrary"),
        name="moe_expert_ffn",
    )(tile_expert, n_used, xs, wg, wu, wd)


def _combine_kernel(dest_ref, dest_next_ref, x_ref, mod_ref, rw_ref, fw_ref, ys_ref, o_ref, buf_ref, sem, *, final):
    tm = x_ref.shape[0]
    i = pl.program_id(0)
    cur = lax.rem(i, 2)

    def issue(idx_ref, buf_slot):
        def gather(row, slot, k):
            pltpu.make_async_copy(_tile_row(ys_ref, idx_ref[0, 0, slot]), _tile_row(buf_ref.at[buf_slot, k], row),
                                  sem.at[buf_slot]).start(priority=k)

        _for_each_row_pair(tm, gather)

    @pl.when(i == 0)
    def _():
        issue(dest_ref, 0)

    @pl.when(i + 1 < pl.num_programs(0))
    def _():
        issue(dest_next_ref, 1 - cur)

    for k in range(2):
        pltpu.make_async_copy(ys_ref.at[pl.ds(0, tm * ROW_TILES)], buf_ref.at[cur, k], sem.at[cur]).wait()
    rw = rw_ref[...]
    y = (rw[:, 0:1] * _from_tile_rows(buf_ref.at[cur, 0], tm)
         + rw[:, 1:2] * _from_tile_rows(buf_ref.at[cur, 1], tm))
    out = x_ref[...] + _mod(mod_ref, GATE2) * y
    if final:
        out = out * lax.rsqrt(jnp.mean(out * out, axis=-1, keepdims=True) + EPS) * fw_ref[...]
    o_ref[...] = out


def combine(dest, x, mod, rw, final_w, ys, *, seq, tm, final):
    t = x.shape[0]
    per_b = seq // tm
    last = t // tm - 1
    return pl.pallas_call(
        functools.partial(_combine_kernel, final=final),
        grid=(t // tm,),
        in_specs=[pl.BlockSpec((1, 1, 2 * tm), lambda i: (i, 0, 0), memory_space=pltpu.SMEM),
                  pl.BlockSpec((1, 1, 2 * tm), lambda i: (jnp.minimum(i + 1, last), 0, 0), memory_space=pltpu.SMEM),
                  pl.BlockSpec((tm, D), lambda i: (i, 0)),
                  pl.BlockSpec((1, MOD_ROWS, D), lambda i: (i // per_b, 0, 0)),
                  pl.BlockSpec((tm, LANES), lambda i: (i, 0)),
                  pl.BlockSpec((1, D), lambda i: (0, 0)),
                  pl.BlockSpec(memory_space=pl.ANY)],
        out_specs=pl.BlockSpec((tm, D), lambda i: (i, 0)),
        out_shape=jax.ShapeDtypeStruct((t, D), F32),
        scratch_shapes=[pltpu.VMEM((2, 2, tm * ROW_TILES, LANES), F32), pltpu.SemaphoreType.DMA((2,))],
        compiler_params=_cparams("arbitrary"),
        name="moe_combine",
    )(dest, dest, x, mod, rw, final_w, ys)


def _routing_plan(route_i, *, tm, tg):
    t = route_i.shape[0]
    e_flat = route_i[:, :2].reshape(-1)
    onehot = (e_flat[:, None] == jnp.arange(N_EXPERTS, dtype=jnp.int32)[None, :]).astype(jnp.int32)
    csum = jnp.cumsum(onehot, axis=0)
    counts = csum[-1]
    padded = ((counts + tg - 1) // tg) * tg
    ends = jnp.cumsum(padded)
    starts = ends - padded
    dest = jnp.sum(onehot * (starts[None, :] + csum - onehot), axis=1)
    n_tiles = (2 * t) // tg + N_EXPERTS
    tile_start = jnp.arange(n_tiles, dtype=jnp.int32) * tg
    tile_expert = jnp.sum((tile_start[:, None] >= ends[None, :]).astype(jnp.int32), axis=1)
    n_used = (ends[-1] // tg).astype(jnp.int32)
    last_expert = jnp.take(tile_expert, jnp.maximum(n_used - 1, 0))
    tile_expert = jnp.where(tile_start < ends[-1], tile_expert, last_expert).astype(jnp.int32)
    ends_padded = jnp.concatenate([ends, padded, n_used.reshape(1)]).astype(jnp.int32)
    return (dest.reshape(t // tm, 1, 2 * tm).astype(jnp.int32), tile_expert, n_used.reshape(1), ends_padded,
            n_tiles * tg)


def moe(x, y, w_out, mod, norm_w, wr, wg, wu, wd, final_w, *, layer, seq, tm, tg, final):
    x1, hn, rw, ri = router(x, y, w_out, mod, norm_w, wr, layer=layer, seq=seq, tm=tm)
    dest, tile_expert, n_used, ends_padded, n_rows = _routing_plan(ri, tm=tm, tg=tg)
    xs = dispatch(ends_padded, dest, hn, n_rows=n_rows, tm=tm, tg=tg)
    ys = expert_ffn(tile_expert, n_used, xs, wg, wu, wd, layer=layer, tg=tg)
    return combine(dest, x1, mod, rw, final_w, ys, seq=seq, tm=tm, final=final)


def _pad_cols(w, n):
    return jnp.pad(w, ((0, 0), (0, n - w.shape[1])))


def kernel(x, c, ada_w, ada_b, norm1_w, norm2_w, final_norm_w, m_w_in, m_i_bias, m_f_bias, m_conv_w, m_conv_b, m_norm_w, m_w_out, h_w_in, h_lb_logits, h_norm_w, h_w_out, ffn_w_gate, ffn_w_up, ffn_w_down, moe_router, moe_w_gate, moe_w_up, moe_w_down):
    bsz, seq, _ = x.shape
    t = bsz * seq
    tiles = _tiles(seq)
    tm, tm_in = tiles['tm'], tiles['tm_in']
    xt = x.reshape(t, D)

    mod_all = ada_modulation(c, ada_w, ada_b)
    mod_all = jnp.pad(mod_all.reshape(DEPTH, bsz, N_MOD, D), ((0, 0), (0, 0), (0, MOD_ROWS - N_MOD), (0, 0)))

    gam = jax.nn.softmax(h_lb_logits.astype(F32), axis=0)
    lower_bounds = jnp.cumsum(gam, axis=0) - gam[:1]

    n_main = 2 * N_HEADS * M_DQK + 2 * N_HEADS * HEAD_DV
    final_w = final_norm_w.reshape(1, D)
    m_w_in_b, h_w_in_b = m_w_in.astype(BF16), h_w_in.astype(BF16)
    m_w_out_b, h_w_out_b = m_w_out.astype(BF16), h_w_out.astype(BF16)
    ffn_g, ffn_u, ffn_d = ffn_w_gate.astype(BF16), ffn_w_up.astype(BF16), ffn_w_down.astype(BF16)
    moe_g, moe_u, moe_d = moe_w_gate.astype(BF16), moe_w_up.astype(BF16), moe_w_down.astype(BF16)
    for layer in range(DEPTH):
        j = layer // 2
        mod = mod_all[layer]
        n1 = norm1_w[layer].reshape(1, D)
        n2 = norm2_w[layer].reshape(1, D)
        if layer % 2 == 0:
            w_gates = jnp.concatenate([_pad_cols(m_w_in[j, :, n_main:n_main + N_HEADS], LANES),
                                       _pad_cols(m_w_in[j, :, n_main + N_HEADS:], LANES)], axis=1).astype(BF16)
            proj, gates = in_proj(xt, mod, n1, m_w_in_b, w_gates, layer=j, n=n_main,
                                  seq=seq, tm=tm_in, tn=n_main // 2)
            gate_bias = jnp.concatenate([_pad_cols(m_i_bias[j].reshape(1, -1), LANES),
                                         _pad_cols(m_f_bias[j].reshape(1, -1), LANES)], axis=1)
            y = mlstm_mix(proj, gates, m_conv_w[j], m_conv_b[j].reshape(1, -1), gate_bias,
                          m_norm_w[j].reshape(1, -1), bsz=bsz, seq=seq)
            xt = ffn(xt, y, m_w_out_b, mod, n2, ffn_g, ffn_u, ffn_d, layer=j, seq=seq, tm=tm,
                     tf=ffn_w_gate.shape[2] // 2)
        else:
            n_h = h_w_in.shape[2]
            (proj,) = in_proj(xt, mod, n1, h_w_in_b, None, layer=j, n=n_h, seq=seq, tm=tm_in, tn=n_h // 2)
            y = hgrn_mix(proj, lower_bounds[j].reshape(1, D), h_norm_w[j].reshape(1, D), bsz=bsz, seq=seq)
            wr = _pad_cols(moe_router[j], LANES)
            wr_hi = wr.astype(BF16)
            wr_split = jnp.concatenate([wr_hi, (wr - wr_hi.astype(F32)).astype(BF16)], axis=1)
            xt = moe(xt, y, h_w_out_b, mod, n2, wr_split, moe_g, moe_u, moe_d, final_w,
                     layer=j, seq=seq, tm=tiles['tm_moe'], tg=tm, final=layer == DEPTH - 1)
    return xt.reshape(bsz, seq, D)
```

```python
import functools

import jax
import jax.numpy as jnp
from jax import lax
from jax.experimental import pallas as pl
from jax.experimental.pallas import tpu as pltpu

F32 = jnp.float32
BF16 = jnp.bfloat16

DEPTH = 4
D = 1024
N_HEADS = 8
HEAD_DV = 128
M_DQK = 64
CONV_K = 4
IGATE_CAP = 15.0
N_EXPERTS = 8
EPS = 1e-6
NEG = -1e30
F_FLOOR = 1e-30
SHIFT1, SCALE1, GATE1, SHIFT2, SCALE2, GATE2 = range(6)
N_MOD = 6
LANES = 128
SUBLANES = 8
ROW_TILES = D // LANES
MOD_ROWS = SUBLANES
CHUNK = 128
VMEM_LIMIT = 56 * 1024 * 1024


def _tiles(seq):
    return dict(tm=min(512, seq),
                tm_moe=min(1024, seq),
                tm_in=min(1024, seq))


def _cparams(*sem):
    return pltpu.CompilerParams(dimension_semantics=sem, vmem_limit_bytes=VMEM_LIMIT)


def _sigmoid(x):
    return 1.0 / (1.0 + jnp.exp(-x))


def _silu(x):
    return x * _sigmoid(x)


def _dot(a, b):
    return jnp.dot(a, b, preferred_element_type=F32)


def _dot_f32(a, b):
    return jnp.dot(a, b, preferred_element_type=F32, precision=lax.Precision.HIGHEST)


def _cumsum_rows(x):
    n = x.shape[0]
    tri = (lax.broadcasted_iota(jnp.int32, (n, n), 1) <= lax.broadcasted_iota(jnp.int32, (n, n), 0)).astype(BF16)
    x1 = x.astype(BF16)
    r1 = x - x1.astype(F32)
    x2 = r1.astype(BF16)
    x3 = (r1 - x2.astype(F32)).astype(BF16)
    return _dot(tri, x1) + _dot(tri, x2) + _dot(tri, x3)


def _mod(mod_ref, row):
    return mod_ref[0, row:row + 1, :]


def _norm_mod(x, norm_w, scale, shift):
    y = x * lax.rsqrt(jnp.mean(x * x, axis=-1, keepdims=True) + EPS)
    return (y * norm_w) * (1.0 + scale) + shift


def _ada_kernel(c_ref, w_ref, b_ref, o_ref):
    c = c_ref[...]
    o_ref[0] = _dot_f32(_silu(c), w_ref[0]) + b_ref[0]


def ada_modulation(c, ada_w, ada_b):
    bsz = c.shape[0]
    tn = N_MOD * D // 4
    return pl.pallas_call(
        _ada_kernel,
        grid=(DEPTH, N_MOD * D // tn),
        in_specs=[pl.BlockSpec((bsz, D), lambda l, j: (0, 0)),
                  pl.BlockSpec((1, D, tn), lambda l, j: (l, 0, j)),
                  pl.BlockSpec((1, 1, tn), lambda l, j: (l, 0, j))],
        out_specs=pl.BlockSpec((1, bsz, tn), lambda l, j: (l, 0, j)),
        out_shape=jax.ShapeDtypeStruct((DEPTH, bsz, N_MOD * D), F32),
        compiler_params=_cparams("arbitrary", "arbitrary"),
        name="ada_modulation",
    )(c, ada_w, ada_b.reshape(DEPTH, 1, N_MOD * D))


def _in_proj_kernel(x_ref, mod_ref, nw_ref, w_ref, *rest, has_extra):
    if has_extra:
        wx_ref, o_ref, ox_ref, hn_ref = rest
    else:
        o_ref, hn_ref = rest

    @pl.when(pl.program_id(1) == 0)
    def _():
        hn = _norm_mod(x_ref[...], nw_ref[...], _mod(mod_ref, SCALE1), _mod(mod_ref, SHIFT1))
        hn_ref[...] = hn.astype(BF16)
        if has_extra:
            ox_ref[...] = _dot(hn_ref[...], wx_ref[...])

    o_ref[...] = _dot(hn_ref[...], w_ref[0]).astype(o_ref.dtype)


def in_proj(x, mod, norm_w, w, w_extra, *, layer, n, seq, tm, tn):
    t = x.shape[0]
    per_b = seq // tm
    has_extra = w_extra is not None
    in_specs = [pl.BlockSpec((tm, D), lambda i, j: (i, 0)),
                pl.BlockSpec((1, MOD_ROWS, D), lambda i, j: (i // per_b, 0, 0)),
                pl.BlockSpec((1, D), lambda i, j: (0, 0)),
                pl.BlockSpec((1, D, tn), lambda i, j: (layer, 0, j))]
    out_specs = [pl.BlockSpec((tm, tn), lambda i, j: (i, j))]
    out_shape = [jax.ShapeDtypeStruct((t, n), BF16)]
    args = [x, mod, norm_w, w]
    if has_extra:
        nx = w_extra.shape[1]
        in_specs.append(pl.BlockSpec((D, nx), lambda i, j: (0, 0)))
        out_specs.append(pl.BlockSpec((tm, nx), lambda i, j: (i, 0)))
        out_shape.append(jax.ShapeDtypeStruct((t, nx), F32))
        args.append(w_extra)
    return pl.pallas_call(
        functools.partial(_in_proj_kernel, has_extra=has_extra),
        grid=(t // tm, n // tn),
        in_specs=in_specs,
        out_specs=out_specs,
        out_shape=out_shape,
        scratch_shapes=[pltpu.VMEM((tm, D), BF16)],
        compiler_params=_cparams("arbitrary", "arbitrary"),
        name="in_proj",
    )(*args)


def _interleave(chunks):
    live = list(chunks)
    while live:
        for gen in list(live):
            if next(gen, StopIteration) is StopIteration:
                live.remove(gen)


def _mlstm_kernel(qk_ref, v_ref, o_ref, g_ref, cw_ref, cb_ref, gb_ref, nw_ref, out_ref,
                  tail_ref, c_ref, m_ref):
    @pl.when(pl.program_id(1) == 0)
    def _():
        tail_ref[...] = jnp.zeros_like(tail_ref)
        c_ref[...] = jnp.zeros_like(c_ref)
        m_ref[...] = jnp.zeros_like(m_ref)

    _interleave([_mlstm_chunk(qk_ref.at[i], v_ref.at[i], o_ref.at[i], g_ref.at[i], cw_ref, cb_ref, gb_ref, nw_ref,
                              out_ref.at[i], tail_ref.at[i], c_ref.at[i], m_ref.at[i])
                 for i in range(qk_ref.shape[0])])


def _mlstm_chunk(qk_ref, v_ref, o_ref, g_ref, cw_ref, cb_ref, gb_ref, nw_ref, out_ref, tail_ref, c_ref, m_ref):
    L = CHUNK

    raw = qk_ref[...].astype(F32)
    xx = jnp.concatenate([tail_ref[...], raw], axis=0)
    tail_ref[...] = raw[L - SUBLANES:, :]
    y = cb_ref[...] + cw_ref[CONV_K - 1:CONV_K, :] * raw
    for j in range(CONV_K - 1):
        y = y + cw_ref[j:j + 1, :] * pltpu.roll(xx, CONV_K - 1 - j, axis=0)[SUBLANES:, :]
    qk = _silu(y)
    half = N_HEADS * M_DQK
    q_all = qk[:, :half]
    k_all = qk[:, half:] * (M_DQK ** -0.5)

    g = g_ref[...] + gb_ref[...]
    li = IGATE_CAP * jnp.tanh(g[:, :LANES] * (1.0 / IGATE_CAP))
    gf = g[:, LANES:]
    b = _cumsum_rows(jnp.minimum(gf, 0.0) - jnp.log1p(jnp.exp(-jnp.abs(gf))))
    tril = lax.broadcasted_iota(jnp.int32, (L, L), 1) <= lax.broadcasted_iota(jnp.int32, (L, L), 0)

    a = li - b
    amax8 = a.reshape(L // SUBLANES, SUBLANES, LANES)
    sub = lax.broadcasted_iota(jnp.int32, (1, SUBLANES, LANES), 1)
    for sh in (1, 2, 4):
        amax8 = jnp.where(sub >= sh, jnp.maximum(amax8, pltpu.roll(amax8, sh, axis=1)), amax8)
    groups = [amax8[0]]
    for v in range(1, L // SUBLANES):
        carry = jnp.broadcast_to(groups[-1][SUBLANES - 1:SUBLANES, :], (SUBLANES, LANES))
        groups.append(jnp.maximum(amax8[v], carry))
    amax = jnp.concatenate(groups, axis=0)
    m_prev = m_ref[...]
    mx = jnp.maximum(m_prev, amax)
    mx_last = mx[L - 1:L, :]
    m_ref[...] = b[L - 1:L, :] + mx_last
    decay = jnp.exp(m_prev - mx_last)
    a_r = a.T[0:N_HEADS, :]
    wk_r = jnp.exp(a - mx_last).T[0:N_HEADS, :]

    lane = lax.broadcasted_iota(jnp.int32, (L, LANES), 1)
    ones_b = jnp.ones((L, LANES), BF16)
    kt_pairs = [k_all[:, p * LANES:(p + 1) * LANES].T for p in range(N_HEADS // 2)]
    qb_l, kb_l, ve_l, kw_l = [], [], [], []
    for h in range(N_HEADS):
        p, hf = h // 2, h % 2
        in_head = (lane >= hf * M_DQK) & (lane < (hf + 1) * M_DQK)
        qb_l.append(jnp.where(in_head, q_all[:, p * LANES:(p + 1) * LANES], 0.0).astype(BF16))
        kb_l.append(k_all[:, p * LANES:(p + 1) * LANES].astype(BF16))
        ve_l.append(jnp.concatenate([v_ref[:, h * HEAD_DV:(h + 1) * HEAD_DV].astype(BF16), ones_b], axis=-1))
        kw_l.append((kt_pairs[p] * wk_r[h:h + 1, :]).astype(BF16))
    qb = jnp.stack(qb_l)
    kb = jnp.stack(kb_l)
    v_ext = jnp.stack(ve_l)
    c_st = c_ref[...]

    qk_raw = jnp.einsum('hqd,hkd->hqk', qb, kb, preferred_element_type=F32)
    inter = jnp.einsum('hqd,hde->hqe', qb, c_st.astype(BF16), preferred_element_type=F32)
    for h in range(N_HEADS):
        c_ref[h] = decay[:, h:h + 1] * c_st[h] + _dot(kw_l[h], v_ext[h])
    yield

    def lane_bcast(cols):
        return jnp.stack([jnp.broadcast_to(cols[:, h:h + 1], (L, LANES)) for h in range(N_HEADS)])

    mx_b = lane_bcast(mx)
    m_prev_b = jnp.stack([jnp.broadcast_to(m_prev[:, h:h + 1], (1, LANES)) for h in range(N_HEADS)])
    w_inter = jnp.exp(m_prev_b - mx_b)
    pmat = jnp.exp(jnp.where(tril[None], a_r[:, None, :] - mx_b, NEG))
    s = qk_raw * pmat
    nd = (jnp.concatenate([w_inter, w_inter], axis=-1) * inter
          + jnp.einsum('hqk,hke->hqe', s.astype(BF16), v_ext, preferred_element_type=F32))
    floor = jnp.exp(-lane_bcast(b + mx))
    yield

    hh = nd[:, :, :HEAD_DV] / jnp.maximum(jnp.abs(nd[:, :, HEAD_DV:]), floor)
    ms = _dot((hh * hh).astype(BF16).reshape(N_HEADS * L, HEAD_DV), ones_b).reshape(hh.shape) * (1.0 / HEAD_DV)
    yield

    hn = hh * lax.rsqrt(ms + EPS)
    merged = jnp.concatenate([hn[h] for h in range(N_HEADS)], axis=-1) * nw_ref[...]
    out_ref[...] = (merged * _sigmoid(o_ref[...].astype(F32))).astype(out_ref.dtype)


def mlstm_mix(proj, gates, conv_w, conv_b, gate_bias, norm_w, *, bsz, seq):
    nblk = seq // CHUNK
    dv = N_HEADS * HEAD_DV
    ns = _seqs_per_step(bsz)
    proj = proj.reshape(bsz, seq, proj.shape[-1])
    gates = gates.reshape(bsz, seq, 2 * LANES)
    out = pl.pallas_call(
        _mlstm_kernel,
        grid=(bsz // ns, nblk),
        in_specs=[pl.BlockSpec((ns, CHUNK, D), lambda b, s: (b, s, 0)),
                  pl.BlockSpec((ns, CHUNK, dv), lambda b, s: (b, s, 1)),
                  pl.BlockSpec((ns, CHUNK, dv), lambda b, s: (b, s, 2)),
                  pl.BlockSpec((ns, CHUNK, 2 * LANES), lambda b, s: (b, s, 0)),
                  pl.BlockSpec((CONV_K, D), lambda b, s: (0, 0)),
                  pl.BlockSpec((1, D), lambda b, s: (0, 0)),
                  pl.BlockSpec((1, 2 * LANES), lambda b, s: (0, 0)),
                  pl.BlockSpec((1, dv), lambda b, s: (0, 0))],
        out_specs=pl.BlockSpec((ns, CHUNK, dv), lambda b, s: (b, s, 0)),
        out_shape=jax.ShapeDtypeStruct((bsz, seq, dv), BF16),
        scratch_shapes=[pltpu.VMEM((ns, SUBLANES, D), F32),
                        pltpu.VMEM((ns, N_HEADS, LANES, 2 * HEAD_DV), F32),
                        pltpu.VMEM((ns, 1, LANES), F32)],
        compiler_params=_cparams("arbitrary", "arbitrary"),
        name="mlstm_mix",
    )(proj, proj, proj, gates, conv_w, conv_b, gate_bias, norm_w)
    return out.reshape(bsz * seq, dv)


def _hgrn_kernel(q_ref, f_ref, v_ref, g_ref, lb_ref, nw_ref, out_ref, st_ref):
    @pl.when(pl.program_id(1) == 0)
    def _():
        st_ref[...] = jnp.zeros_like(st_ref)

    _interleave([_hgrn_chunk(q_ref.at[i], f_ref.at[i], v_ref.at[i], g_ref.at[i], lb_ref, nw_ref, out_ref.at[i],
                             st_ref.at[i]) for i in range(q_ref.shape[0])])


def _hgrn_chunk(q_ref, f_ref, v_ref, g_ref, lb_ref, nw_ref, out_ref, st_ref):
    L = CHUNK

    lb = lb_ref[...]
    f = f_ref[...].astype(F32)
    e = jnp.exp(-jnp.abs(f))
    r = 1.0 / (1.0 + e)
    er = e * r
    sig = jnp.where(f >= 0.0, r, er)
    nsig = jnp.where(f >= 0.0, er, r)
    dec_all = jnp.maximum(lb + (1.0 - lb) * sig, F_FLOOR)
    k_all = (1.0 - lb) * nsig
    q_all = _silu(q_ref[...].astype(F32))
    row = lax.broadcasted_iota(jnp.int32, (L, L), 0)
    col = lax.broadcasted_iota(jnp.int32, (L, L), 1)
    xor = row ^ col
    eye = row == col
    ones_b = jnp.ones((LANES, LANES), BF16)
    sub8 = lax.broadcasted_iota(jnp.int32, (1, SUBLANES, LANES), 1)
    H = N_HEADS
    G = L // SUBLANES

    def heads(x):
        return jnp.stack([x[:, h * HEAD_DV:(h + 1) * HEAD_DV] for h in range(H)])

    def bmm_nt(x, y):
        return jnp.einsum('hqd,hkd->hqk', x.astype(BF16), y.astype(BF16), preferred_element_type=F32)

    def rot8(x3, d):
        return pltpu.roll(x3, d % SUBLANES, axis=1)

    def level_mask(b):
        return ((xor >= b) & (xor < 2 * b))[None]

    q, k, dec = heads(q_all), heads(k_all), heads(dec_all)
    vb = heads(v_ref[...]).astype(BF16)
    st = st_ref[...]

    diag = _dot((q * k).astype(BF16).reshape(H * L, HEAD_DV), ones_b).reshape(H, L, L)

    n8 = H * G
    e8 = dec.reshape(n8, SUBLANES, LANES)
    q8 = q.reshape(n8, SUBLANES, LANES)
    k8 = k.reshape(n8, SUBLANES, LANES)
    ew = [e8]
    for j in range(1, 4):
        ew.append(ew[-1] * rot8(e8, j))
    eu = [rot8(e8, -1)]
    for j in range(1, 3):
        eu.append(eu[-1] * rot8(e8, -1 - j))
    pos2, pos4 = sub8 % 2, sub8 % 4
    pre2 = jnp.where(pos2 == 0, ew[0], ew[1])
    suf2 = jnp.where(pos2 == 0, eu[0], 1.0)
    pre4 = jnp.where(pos4 == 0, ew[0], jnp.where(pos4 == 1, ew[1], jnp.where(pos4 == 2, ew[2], ew[3])))
    suf4 = jnp.where(pos4 == 0, eu[2], jnp.where(pos4 == 1, eu[1], jnp.where(pos4 == 2, eu[0], 1.0)))
    small = [
        (4, jnp.where(sub8 >= 4, pre4, 0.0), jnp.where(sub8 < 4, suf4, 0.0)),
        (2, jnp.where(pos4 >= 2, pre2, 0.0), jnp.where(pos4 < 2, suf2, 0.0)),
        (1, jnp.where(pos2 == 1, e8, 0.0), jnp.where(pos2 == 0, 1.0, 0.0)),
    ]
    levels = [(b, bmm_nt((q8 * fq).reshape(H, L, LANES), (k8 * fk).reshape(H, L, LANES))) for b, fq, fk in small]

    pre = pre4 * jnp.where(sub8 >= 4, pre4[:, 3:4, :], 1.0)
    suf = suf4 * jnp.where(sub8 < 4, pre4[:, 7:8, :], 1.0)
    tot = pre[:, 7:8, :]
    q6 = q.reshape(H, G, 1, 1, SUBLANES, LANES)
    k6 = k.reshape(H, G, 1, 1, SUBLANES, LANES)
    m = 1
    while m < G:
        nb = G // (2 * m)
        shape = (H, nb, 2, m, SUBLANES, LANES)
        pre, suf, q6, k6 = (x.reshape(shape) for x in (pre, suf, q6, k6))
        tot = tot.reshape(H, nb, 2, 1, 1, LANES)
        zeros = jnp.zeros((H, nb, 1, m, SUBLANES, LANES), F32)
        q_l = jnp.concatenate([zeros, q6[:, :, 1:] * pre[:, :, 1:]], axis=2)
        k_l = jnp.concatenate([k6[:, :, :1] * suf[:, :, :1], zeros], axis=2)
        levels.append((SUBLANES * m, bmm_nt(q_l.reshape(H, L, LANES), k_l.reshape(H, L, LANES))))
        pre = jnp.concatenate([pre[:, :, :1], pre[:, :, 1:] * tot[:, :, :1]], axis=2)
        suf = jnp.concatenate([suf[:, :, :1] * tot[:, :, 1:], suf[:, :, 1:]], axis=2)
        tot = tot[:, :, :1] * tot[:, :, 1:]
        m *= 2
    pre = pre.reshape(H, L, LANES)
    suf = suf.reshape(H, L, LANES)
    tot = tot.reshape(H, 1, LANES)

    o_inter = bmm_nt(q * pre, st)
    k_dec = (k * suf).astype(BF16)
    for h in range(H):
        st_ref[h] = st[h] * tot[h] + lax.dot_general(
            vb[h], k_dec[h], (((0,), (0,)), ((), ())), preferred_element_type=F32)
    yield

    a = jnp.where(eye[None], diag, 0.0)
    for b, a_l in levels:
        a = jnp.where(level_mask(b), a_l, a)
    o = o_inter + jnp.einsum('hqk,hkv->hqv', a.astype(BF16), vb, preferred_element_type=F32)
    yield

    ms = _dot((o * o).astype(BF16).reshape(H * L, HEAD_DV), ones_b).reshape(o.shape) * (1.0 / HEAD_DV)
    yield

    on = o * lax.rsqrt(ms + EPS)
    merged = jnp.concatenate([on[h] for h in range(H)], axis=-1) * nw_ref[...]
    out_ref[...] = (merged * _silu(g_ref[...].astype(F32))).astype(out_ref.dtype)


def _seqs_per_step(bsz):
    return next(n for n in (4, 2, 1) if bsz % n == 0)


def hgrn_mix(proj, lb, norm_w, *, bsz, seq):
    nblk = seq // CHUNK
    ns = _seqs_per_step(bsz)
    proj = proj.reshape(bsz, seq, proj.shape[-1])
    out = pl.pallas_call(
        _hgrn_kernel,
        grid=(bsz // ns, nblk),
        in_specs=[pl.BlockSpec((ns, CHUNK, D), lambda b, s: (b, s, 0)),
                  pl.BlockSpec((ns, CHUNK, D), lambda b, s: (b, s, 1)),
                  pl.BlockSpec((ns, CHUNK, D), lambda b, s: (b, s, 2)),
                  pl.BlockSpec((ns, CHUNK, D), lambda b, s: (b, s, 3)),
                  pl.BlockSpec((1, D), lambda b, s: (0, 0)),
                  pl.BlockSpec((1, D), lambda b, s: (0, 0))],
        out_specs=pl.BlockSpec((ns, CHUNK, D), lambda b, s: (b, s, 0)),
        out_shape=jax.ShapeDtypeStruct((bsz, seq, D), BF16),
        scratch_shapes=[pltpu.VMEM((ns, N_HEADS, HEAD_DV, HEAD_DV), F32)],
        compiler_params=_cparams("arbitrary", "arbitrary"),
        name="hgrn_mix",
    )(proj, proj, proj, proj, lb, norm_w)
    return out.reshape(bsz * seq, D)


def _mixer_residual(x_ref, y_ref, wo_ref, mod_ref):
    return x_ref[...] + _mod(mod_ref, GATE1) * _dot(y_ref[...], wo_ref[0])


def _ffn_kernel(x_ref, y_ref, wo_ref, mod_ref, nw_ref, wg_ref, wu_ref, wd_ref, o_ref, *, tf):
    x1 = _mixer_residual(x_ref, y_ref, wo_ref, mod_ref)
    hb = _norm_mod(x1, nw_ref[...], _mod(mod_ref, SCALE2), _mod(mod_ref, SHIFT2)).astype(BF16)
    acc = None
    for f0 in range(0, wg_ref.shape[2], tf):
        hid = _silu(_dot(hb, wg_ref[0, :, f0:f0 + tf])) * _dot(hb, wu_ref[0, :, f0:f0 + tf])
        part = _dot(hid.astype(BF16), wd_ref[0, f0:f0 + tf, :])
        acc = part if acc is None else acc + part
    o_ref[...] = x1 + _mod(mod_ref, GATE2) * acc


def ffn(x, y, w_out, mod, norm_w, wg, wu, wd, *, layer, seq, tm, tf):
    t = x.shape[0]
    dff = wg.shape[2]
    per_b = seq // tm
    resident = dict(pipeline_mode=pl.Buffered(1))
    return pl.pallas_call(
        functools.partial(_ffn_kernel, tf=tf),
        grid=(t // tm,),
        in_specs=[pl.BlockSpec((tm, D), lambda i: (i, 0)),
                  pl.BlockSpec((tm, D), lambda i: (i, 0)),
                  pl.BlockSpec((1, D, D), lambda i: (layer, 0, 0), **resident),
                  pl.BlockSpec((1, MOD_ROWS, D), lambda i: (i // per_b, 0, 0)),
                  pl.BlockSpec((1, D), lambda i: (0, 0)),
                  pl.BlockSpec((1, D, dff), lambda i: (layer, 0, 0), **resident),
                  pl.BlockSpec((1, D, dff), lambda i: (layer, 0, 0), **resident),
                  pl.BlockSpec((1, dff, D), lambda i: (layer, 0, 0), **resident)],
        out_specs=pl.BlockSpec((tm, D), lambda i: (i, 0)),
        out_shape=jax.ShapeDtypeStruct((t, D), F32),
        compiler_params=_cparams("arbitrary"),
        name="ffn",
    )(x, y, w_out, mod, norm_w, wg, wu, wd)


def _router_kernel(x_ref, y_ref, wo_ref, mod_ref, nw_ref, wr_ref, x1_ref, hn_ref, rw_ref, ri_ref):
    x1 = _mixer_residual(x_ref, y_ref, wo_ref, mod_ref)
    x1_ref[...] = x1
    hn = _norm_mod(x1, nw_ref[...], _mod(mod_ref, SCALE2), _mod(mod_ref, SHIFT2))
    _to_tile_rows(hn_ref, hn)
    h_hi = hn.astype(BF16)
    h_lo = (hn - h_hi.astype(F32)).astype(BF16)
    hw = _dot(h_hi, wr_ref[...])
    logits = hw[:, :LANES] + hw[:, LANES:] + _dot(h_lo, wr_ref[:, :LANES])
    lane = lax.broadcasted_iota(jnp.int32, logits.shape, 1)
    lg = jnp.where(lane < N_EXPERTS, logits, NEG)
    m1 = jnp.max(lg, axis=-1, keepdims=True)
    i1 = jnp.min(jnp.where(lg == m1, lane, LANES), axis=-1, keepdims=True)
    lg2 = jnp.where(lane == i1, NEG, lg)
    m2 = jnp.max(lg2, axis=-1, keepdims=True)
    i2 = jnp.min(jnp.where(lg2 == m2, lane, LANES), axis=-1, keepdims=True)
    e2 = jnp.exp(m2 - m1)
    w1 = 1.0 / (1.0 + e2)
    rw_ref[...] = jnp.where(lane == 0, w1, jnp.where(lane == 1, e2 * w1, 0.0))
    ri_ref[...] = jnp.where(lane == 0, i1, jnp.where(lane == 1, i2, 0))


def router(x, y, w_out, mod, norm_w, wr, *, layer, seq, tm):
    t = x.shape[0]
    per_b = seq // tm
    return pl.pallas_call(
        _router_kernel,
        grid=(t // tm,),
        in_specs=[pl.BlockSpec((tm, D), lambda i: (i, 0)),
                  pl.BlockSpec((tm, D), lambda i: (i, 0)),
                  pl.BlockSpec((1, D, D), lambda i: (layer, 0, 0), pipeline_mode=pl.Buffered(1)),
                  pl.BlockSpec((1, MOD_ROWS, D), lambda i: (i // per_b, 0, 0)),
                  pl.BlockSpec((1, D), lambda i: (0, 0)),
                  pl.BlockSpec((D, 2 * LANES), lambda i: (0, 0))],
        out_specs=[pl.BlockSpec((tm, D), lambda i: (i, 0)),
                   pl.BlockSpec((tm * ROW_TILES, LANES), lambda i: (i, 0)),
                   pl.BlockSpec((tm, LANES), lambda i: (i, 0)),
                   pl.BlockSpec((tm, LANES), lambda i: (i, 0))],
        out_shape=[jax.ShapeDtypeStruct((t, D), F32),
                   jax.ShapeDtypeStruct((t * ROW_TILES, LANES), F32),
                   jax.ShapeDtypeStruct((t, LANES), F32),
                   jax.ShapeDtypeStruct((t, LANES), jnp.int32)],
        compiler_params=_cparams("arbitrary"),
        name="router",
    )(x, y, w_out, mod, norm_w, wr)


def _to_tile_rows(ref, val):
    rows = val.shape[0]
    for s in range(ROW_TILES):
        ref[pl.ds(s, rows, stride=ROW_TILES), :] = val[:, s * LANES:(s + 1) * LANES]


def _from_tile_rows(ref, rows):
    return jnp.concatenate([ref[pl.ds(s, rows, stride=ROW_TILES), :] for s in range(ROW_TILES)], axis=-1)


def _tile_row(ref, row):
    return ref.at[pl.ds(pl.multiple_of(row * ROW_TILES, ROW_TILES), ROW_TILES)]


def _for_each_row_pair(tm, fn):
    def body(row, carry):
        for k in range(2):
            fn(row, 2 * row + k, k)
        return carry

    lax.fori_loop(0, tm, body, 0, unroll=8)


def _dispatch_kernel(ends_ref, dest_ref, hn_ref, xs_ref, zero_ref, stage_ref, sem):
    tm = hn_ref.shape[0] // ROW_TILES
    tgr = zero_ref.shape[0]
    i = pl.program_id(0)
    cur = lax.rem(i, 2)

    @pl.when(i == 0)
    def _():
        zero_ref[...] = jnp.zeros_like(zero_ref)

        def zero_tile(start):
            cp = pltpu.make_async_copy(zero_ref, xs_ref.at[pl.ds(pl.multiple_of(start, tgr), tgr)], sem.at[2])
            cp.start()
            cp.wait()

        n_tiles = xs_ref.shape[0] // tgr
        for e in range(N_EXPERTS):
            @pl.when(ends_ref[N_EXPERTS + e] > 0)
            def _():
                zero_tile(ends_ref[e] * ROW_TILES - tgr)

            @pl.when(n_tiles - 1 - e >= ends_ref[2 * N_EXPERTS])
            def _():
                zero_tile((n_tiles - 1 - e) * tgr)

    stage_ref[cur] = hn_ref[...]

    def scatter(row, slot, k):
        pltpu.make_async_copy(_tile_row(stage_ref.at[cur], row), _tile_row(xs_ref, dest_ref[0, 0, slot]),
                              sem.at[cur]).start(priority=k)

    _for_each_row_pair(tm, scatter)

    def drain(buf_slot):
        for _ in range(2):
            pltpu.make_async_copy(stage_ref.at[buf_slot], xs_ref.at[pl.ds(0, tm * ROW_TILES)],
                                  sem.at[buf_slot]).wait()

    @pl.when(i > 0)
    def _():
        drain(1 - cur)

    @pl.when(i == pl.num_programs(0) - 1)
    def _():
        drain(cur)


def dispatch(ends_padded, dest, hn, *, n_rows, tm, tg):
    t = hn.shape[0] // ROW_TILES
    grid_spec = pltpu.PrefetchScalarGridSpec(
        num_scalar_prefetch=1,
        grid=(t // tm,),
        in_specs=[pl.BlockSpec((1, 1, 2 * tm), lambda i, ends: (i, 0, 0), memory_space=pltpu.SMEM),
                  pl.BlockSpec((tm * ROW_TILES, LANES), lambda i, ends: (i, 0))],
        out_specs=pl.BlockSpec(memory_space=pl.ANY),
        scratch_shapes=[pltpu.VMEM((tg * ROW_TILES, LANES), F32),
                        pltpu.VMEM((2, tm * ROW_TILES, LANES), F32),
                        pltpu.SemaphoreType.DMA((3,))],
    )
    return pl.pallas_call(
        _dispatch_kernel,
        grid_spec=grid_spec,
        out_shape=jax.ShapeDtypeStruct((n_rows * ROW_TILES, LANES), F32),
        compiler_params=_cparams("arbitrary"),
        name="moe_dispatch",
    )(ends_padded, dest, hn)


def _expert_ffn_kernel(te_ref, nu_ref, x_ref, wg_ref, wu_ref, wd_ref, o_ref):
    del te_ref
    i = pl.program_id(0)

    @pl.when(i < nu_ref[0])
    def _():
        hb = _from_tile_rows(x_ref, x_ref.shape[0] // ROW_TILES).astype(BF16)
        hid = _silu(_dot(hb, wg_ref[0, 0])) * _dot(hb, wu_ref[0, 0])
        _to_tile_rows(o_ref, _dot(hid.astype(BF16), wd_ref[0, 0]))

    @pl.when(i >= nu_ref[0])
    def _():
        o_ref[...] = jnp.zeros_like(o_ref)


def expert_ffn(tile_expert, n_used, xs, wg, wu, wd, *, layer, tg):
    r = xs.shape[0] // ROW_TILES
    dfe = wg.shape[3]
    grid_spec = pltpu.PrefetchScalarGridSpec(
        num_scalar_prefetch=2,
        grid=(r // tg,),
        in_specs=[pl.BlockSpec((tg * ROW_TILES, LANES), lambda i, te, nu: (jnp.minimum(i, nu[0] - 1), 0)),
                  pl.BlockSpec((1, 1, D, dfe), lambda i, te, nu: (layer, te[i], 0, 0)),
                  pl.BlockSpec((1, 1, D, dfe), lambda i, te, nu: (layer, te[i], 0, 0)),
                  pl.BlockSpec((1, 1, dfe, D), lambda i, te, nu: (layer, te[i], 0, 0))],
        out_specs=pl.BlockSpec((tg * ROW_TILES, LANES), lambda i, te, nu: (i, 0)),
    )
    return pl.pallas_call(
        _expert_ffn_kernel,
        grid_spec=grid_spec,
        out_shape=jax.ShapeDtypeStruct((r * ROW_TILES, LANES), F32),
        compiler_params=_cparams("arbitrary"),
        name="moe_expert_ffn",
    )(tile_expert, n_used, xs, wg, wu, wd)


def _combine_kernel(dest_ref, dest_next_ref, x_ref, mod_ref, rw_ref, fw_ref, ys_ref, o_ref, buf_ref, sem, *, final):
    tm = x_ref.shape[0]
    i = pl.program_id(0)
    cur = lax.rem(i, 2)

    def issue(idx_ref, buf_slot):
        def gather(row, slot, k):
            pltpu.make_async_copy(_tile_row(ys_ref, idx_ref[0, 0, slot]), _tile_row(buf_ref.at[buf_slot, k], row),
                                  sem.at[buf_slot]).start(priority=k)

        _for_each_row_pair(tm, gather)

    @pl.when(i == 0)
    def _():
        issue(dest_ref, 0)

    @pl.when(i + 1 < pl.num_programs(0))
    def _():
        issue(dest_next_ref, 1 - cur)

    for k in range(2):
        pltpu.make_async_copy(ys_ref.at[pl.ds(0, tm * ROW_TILES)], buf_ref.at[cur, k], sem.at[cur]).wait()
    rw = rw_ref[...]
    y = (rw[:, 0:1] * _from_tile_rows(buf_ref.at[cur, 0], tm)
         + rw[:, 1:2] * _from_tile_rows(buf_ref.at[cur, 1], tm))
    out = x_ref[...] + _mod(mod_ref, GATE2) * y
    if final:
        out = out * lax.rsqrt(jnp.mean(out * out, axis=-1, keepdims=True) + EPS) * fw_ref[...]
    o_ref[...] = out


def combine(dest, x, mod, rw, final_w, ys, *, seq, tm, final):
    t = x.shape[0]
    per_b = seq // tm
    last = t // tm - 1
    return pl.pallas_call(
        functools.partial(_combine_kernel, final=final),
        grid=(t // tm,),
        in_specs=[pl.BlockSpec((1, 1, 2 * tm), lambda i: (i, 0, 0), memory_space=pltpu.SMEM),
                  pl.BlockSpec((1, 1, 2 * tm), lambda i: (jnp.minimum(i + 1, last), 0, 0), memory_space=pltpu.SMEM),
                  pl.BlockSpec((tm, D), lambda i: (i, 0)),
                  pl.BlockSpec((1, MOD_ROWS, D), lambda i: (i // per_b, 0, 0)),
                  pl.BlockSpec((tm, LANES), lambda i: (i, 0)),
                  pl.BlockSpec((1, D), lambda i: (0, 0)),
                  pl.BlockSpec(memory_space=pl.ANY)],
        out_specs=pl.BlockSpec((tm, D), lambda i: (i, 0)),
        out_shape=jax.ShapeDtypeStruct((t, D), F32),
        scratch_shapes=[pltpu.VMEM((2, 2, tm * ROW_TILES, LANES), F32), pltpu.SemaphoreType.DMA((2,))],
        compiler_params=_cparams("arbitrary"),
        name="moe_combine",
    )(dest, dest, x, mod, rw, final_w, ys)


def _routing_plan(route_i, *, tm, tg):
    t = route_i.shape[0]
    e_flat = route_i[:, :2].reshape(-1)
    onehot = (e_flat[:, None] == jnp.arange(N_EXPERTS, dtype=jnp.int32)[None, :]).astype(jnp.int32)
    csum = jnp.cumsum(onehot, axis=0)
    counts = csum[-1]
    padded = ((counts + tg - 1) // tg) * tg
    ends = jnp.cumsum(padded)
    starts = ends - padded
    dest = jnp.sum(onehot * (starts[None, :] + csum - onehot), axis=1)
    n_tiles = (2 * t) // tg + N_EXPERTS
    tile_start = jnp.arange(n_tiles, dtype=jnp.int32) * tg
    tile_expert = jnp.sum((tile_start[:, None] >= ends[None, :]).astype(jnp.int32), axis=1)
    n_used = (ends[-1] // tg).astype(jnp.int32)
    last_expert = jnp.take(tile_expert, jnp.maximum(n_used - 1, 0))
    tile_expert = jnp.where(tile_start < ends[-1], tile_expert, last_expert).astype(jnp.int32)
    ends_padded = jnp.concatenate([ends, padded, n_used.reshape(1)]).astype(jnp.int32)
    return (dest.reshape(t // tm, 1, 2 * tm).astype(jnp.int32), tile_expert, n_used.reshape(1), ends_padded,
            n_tiles * tg)


def moe(x, y, w_out, mod, norm_w, wr, wg, wu, wd, final_w, *, layer, seq, tm, tg, final):
    x1, hn, rw, ri = router(x, y, w_out, mod, norm_w, wr, layer=layer, seq=seq, tm=tm)
    dest, tile_expert, n_used, ends_padded, n_rows = _routing_plan(ri, tm=tm, tg=tg)
    xs = dispatch(ends_padded, dest, hn, n_rows=n_rows, tm=tm, tg=tg)
    ys = expert_ffn(tile_expert, n_used, xs, wg, wu, wd, layer=layer, tg=tg)
    return combine(dest.reshape(-1, 1, 2 * tg), x1, mod, rw, final_w, ys, seq=seq, tm=tg, final=final)


def _pad_cols(w, n):
    return jnp.pad(w, ((0, 0), (0, n - w.shape[1])))


def kernel(x, c, ada_w, ada_b, norm1_w, norm2_w, final_norm_w, m_w_in, m_i_bias, m_f_bias, m_conv_w, m_conv_b, m_norm_w, m_w_out, h_w_in, h_lb_logits, h_norm_w, h_w_out, ffn_w_gate, ffn_w_up, ffn_w_down, moe_router, moe_w_gate, moe_w_up, moe_w_down):
    bsz, seq, _ = x.shape
    t = bsz * seq
    tiles = _tiles(seq)
    tm, tm_in = tiles['tm'], tiles['tm_in']
    xt = x.reshape(t, D)

    mod_all = ada_modulation(c, ada_w, ada_b)
    mod_all = jnp.pad(mod_all.reshape(DEPTH, bsz, N_MOD, D), ((0, 0), (0, 0), (0, MOD_ROWS - N_MOD), (0, 0)))

    gam = jax.nn.softmax(h_lb_logits.astype(F32), axis=0)
    lower_bounds = jnp.cumsum(gam, axis=0) - gam[:1]

    n_main = 2 * N_HEADS * M_DQK + 2 * N_HEADS * HEAD_DV
    final_w = final_norm_w.reshape(1, D)
    m_w_in_b, h_w_in_b = m_w_in.astype(BF16), h_w_in.astype(BF16)
    m_w_out_b, h_w_out_b = m_w_out.astype(BF16), h_w_out.astype(BF16)
    ffn_g, ffn_u, ffn_d = ffn_w_gate.astype(BF16), ffn_w_up.astype(BF16), ffn_w_down.astype(BF16)
    moe_g, moe_u, moe_d = moe_w_gate.astype(BF16), moe_w_up.astype(BF16), moe_w_down.astype(BF16)
    for layer in range(DEPTH):
        j = layer // 2
        mod = mod_all[layer]
        n1 = norm1_w[layer].reshape(1, D)
        n2 = norm2_w[layer].reshape(1, D)
        if layer % 2 == 0:
            w_gates = jnp.concatenate([_pad_cols(m_w_in[j, :, n_main:n_main + N_HEADS], LANES),
                                       _pad_cols(m_w_in[j, :, n_main + N_HEADS:], LANES)], axis=1).astype(BF16)
            proj, gates = in_proj(xt, mod, n1, m_w_in_b, w_gates, layer=j, n=n_main,
                                  seq=seq, tm=tm_in, tn=n_main // 2)
            gate_bias = jnp.concatenate([_pad_cols(m_i_bias[j].reshape(1, -1), LANES),
                                         _pad_cols(m_f_bias[j].reshape(1, -1), LANES)], axis=1)
            y = mlstm_mix(proj, gates, m_conv_w[j], m_conv_b[j].reshape(1, -1), gate_bias,
                          m_norm_w[j].reshape(1, -1), bsz=bsz, seq=seq)
            xt = ffn(xt, y, m_w_out_b, mod, n2, ffn_g, ffn_u, ffn_d, layer=j, seq=seq, tm=tm,
                     tf=ffn_w_gate.shape[2] // 2)
        else:
            n_h = h_w_in.shape[2]
            (proj,) = in_proj(xt, mod, n1, h_w_in_b, None, layer=j, n=n_h, seq=seq, tm=tm_in, tn=n_h // 2)
            y = hgrn_mix(proj, lower_bounds[j].reshape(1, D), h_norm_w[j].reshape(1, D), bsz=bsz, seq=seq)
            wr = _pad_cols(moe_router[j], LANES)
            wr_hi = wr.astype(BF16)
            wr_split = jnp.concatenate([wr_hi, (wr - wr_hi.astype(F32)).astype(BF16)], axis=1)
            xt = moe(xt, y, h_w_out_b, mod, n2, wr_split, moe_g, moe_u, moe_d, final_w,
                     layer=j, seq=seq, tm=tiles['tm_moe'], tg=tm, final=layer == DEPTH - 1)
    return xt.reshape(bsz, seq, D)
```
